```python
import math
import jax, jax.numpy as jnp
from jax import lax
import numpy as np

D_MODEL = 2048
BATCH = 4
SEQ = 2048
DEPTH = 2
DEC_BATCH = 32
DEC_SEQ = 1
PAST_LEN = 16384
PAGE_SIZE = 128

N_HEADS = 16
KV_HEADS = 4
HEAD_DIM = 64
GROUP = N_HEADS // KV_HEADS
ATTN_W = N_HEADS * HEAD_DIM
KV_W = KV_HEADS * HEAD_DIM
WINDOW = 128
ATTN_SCALE = HEAD_DIM ** -0.5
NEG_INF = -1e30
N_BUCKETS = 32
MAX_DISTANCE = 128
CONV_W = D_MODEL // 4
CONV_K = 3
POOL_W = D_MODEL // 4
POOL_WINDOWS = (2, 4, 8, 16)
N_POOL_GROUPS = 4
POOL_GROUP = POOL_W // N_POOL_GROUPS
POOL_PREV = 15
IN_W = ATTN_W + 2 * KV_W + 3 * CONV_W + POOL_W
D_FF = 5632
DEEPNORM_ALPHA = (2 * DEPTH) ** 0.25
DEEPNORM_BETA = (8 * DEPTH) ** -0.25
LN_EPS = 1e-5

kernel_name = "hymba_swa_conv_pool_deepnorm_step"


def _ln(x, g, b):
    xf = x.astype(jnp.float32)
    mu = jnp.mean(xf, -1, keepdims=True)
    var = jnp.mean(jnp.square(xf - mu), -1, keepdims=True)
    y = (xf - mu) * lax.rsqrt(var + LN_EPS) * g.astype(jnp.float32) + b.astype(jnp.float32)
    return y.astype(x.dtype)


def _t5_bucket(n):
    max_exact = N_BUCKETS // 2
    nf = jnp.maximum(n, 1).astype(jnp.float32)
    large = max_exact + (jnp.log(nf / max_exact) / math.log(MAX_DISTANCE / max_exact)
                         * (N_BUCKETS - max_exact)).astype(jnp.int32)
    large = jnp.minimum(large, N_BUCKETS - 1)
    return jnp.where(n < max_exact, n, large)


def _sink_attend(q, k, v, dist, valid, bias_by_dist, sinks):
    s = jnp.einsum('...qhgd,...khd->...hgqk', q, k).astype(jnp.float32) * ATTN_SCALE
    bias = bias_by_dist[jnp.clip(dist, 0, WINDOW - 1)]
    bias = jnp.moveaxis(bias, -1, 0).reshape(KV_HEADS, GROUP, *dist.shape)
    s = jnp.where(valid, s + bias, NEG_INF)
    sink = sinks.astype(jnp.float32).reshape(KV_HEADS, GROUP, 1, 1)
    m = jnp.maximum(jnp.max(s, -1, keepdims=True), sink)
    p = jnp.exp(s - m)
    p = (p / (jnp.sum(p, -1, keepdims=True) + jnp.exp(sink - m))).astype(v.dtype)
    return jnp.einsum('...hgqk,...khd->...qhgd', p, v)


def _swa_prompt(q, k, v, bias_by_dist, sinks):
    b, s = q.shape[:2]
    nb = s // WINDOW
    qb = q.reshape(b, nb, WINDOW, KV_HEADS, GROUP, HEAD_DIM)

    def band(t):
        tb = t.reshape(b, nb, WINDOW, KV_HEADS, HEAD_DIM)
        prev = jnp.concatenate([jnp.zeros_like(tb[:, :1]), tb[:, :-1]], 1)
        return jnp.concatenate([prev, tb], 2)

    i = jnp.arange(WINDOW)[:, None]
    j = jnp.arange(2 * WINDOW)[None, :]
    dist = i + WINDOW - j
    first = (jnp.arange(nb) == 0)[:, None, None]
    valid = (dist >= 0) & (dist < WINDOW) & ~(first & (j < WINDOW))
    o = _sink_attend(qb, band(k), band(v), dist, valid[:, None, None], bias_by_dist, sinks)
    return o.reshape(b, s, ATTN_W)


def _swa_decode(q, k, v, ck, cv, bias_by_dist, sinks):
    b, t = q.shape[:2]
    wc = ck.shape[1]
    kk = jnp.concatenate([ck, k], 1)
    vv = jnp.concatenate([cv, v], 1)
    dist = jnp.arange(t)[:, None] + wc - jnp.arange(wc + t)[None, :]
    valid = (dist >= 0) & (dist < WINDOW)
    o = _sink_attend(q.reshape(b, t, KV_HEADS, GROUP, HEAD_DIM), kk, vv, dist, valid,
                     bias_by_dist, sinks)
    return o.reshape(b, t, ATTN_W), kk[:, -wc:], vv[:, -wc:]


def _dwconv3(ext, w):
    L = ext.shape[1] - (CONV_K - 1)
    y = ext[:, 0:L] * w[0]
    for kk in range(1, CONV_K):
        y = y + ext[:, kk:kk + L] * w[kk]
    return y


def _pool_mix(p_ext, pos0, pool_w, pool_scale):
    L = p_ext.shape[1] - POOL_PREV
    xf = p_ext.astype(jnp.float32)
    csz = jnp.concatenate([jnp.zeros_like(xf[:, :1]), jnp.cumsum(xf, 1)], 1)
    cur = xf[:, POOL_PREV:]
    pos = pos0 + jnp.arange(L)
    outs = []
    for g, w in enumerate(POOL_WINDOWS):
        lo, hi = g * POOL_GROUP, (g + 1) * POOL_GROUP
        win = (csz[:, POOL_PREV + 1:POOL_PREV + 1 + L, lo:hi]
               - csz[:, POOL_PREV + 1 - w:POOL_PREV + 1 - w + L, lo:hi])
        cnt = jnp.minimum(pos + 1, w).astype(jnp.float32)[None, :, None]
        d = (win / cnt - cur[..., lo:hi]).astype(p_ext.dtype)
        outs.append(d @ pool_w[g])
    return jnp.concatenate(outs, -1) * pool_scale


def _layer(x, pos0, kv_prev, conv_prev, pool_prev, ffn_prev, bias_by_dist,
           w_in, conv_w, pool_w, pool_scale, sinks, w_o, ln1_g, ln1_b,
           w_up, ffn_conv_w, w_down, ln2_g, ln2_b):
    b, L, _ = x.shape
    z = x @ w_in
    sizes = (ATTN_W, KV_W, KV_W, CONV_W, CONV_W, CONV_W, POOL_W)
    idx = []
    acc = 0
    for sz in sizes[:-1]:
        acc += sz
        idx.append(acc)
    q, k, v, gb, gc, h, pin = jnp.split(z, idx, -1)
    q = q.reshape(b, L, N_HEADS, HEAD_DIM)
    k = k.reshape(b, L, KV_HEADS, HEAD_DIM)
    v = v.reshape(b, L, KV_HEADS, HEAD_DIM)
    if kv_prev is None:
        a = _swa_prompt(q, k, v, bias_by_dist, sinks)
        nk, nv = k[:, -WINDOW:], v[:, -WINDOW:]
    else:
        a, nk, nv = _swa_decode(q, k, v, kv_prev[0], kv_prev[1], bias_by_dist, sinks)
    u_ext = jnp.concatenate([conv_prev, gc * h], 1)
    c = gb * _dwconv3(u_ext, conv_w)
    p_ext = jnp.concatenate([pool_prev, pin], 1)
    pm = _pool_mix(p_ext, pos0, pool_w, pool_scale)
    mix = jnp.concatenate([a, c, pm], -1) @ w_o
    x = _ln(DEEPNORM_ALPHA * x + mix, ln1_g, ln1_b)
    up_ext = jnp.concatenate([ffn_prev, x @ w_up], 1)
    hc = _dwconv3(up_ext, ffn_conv_w)
    g, val = jnp.split(hc, 2, -1)
    f = (jax.nn.silu(g) * val) @ w_down
    x = _ln(DEEPNORM_ALPHA * x + f, ln2_g, ln2_b)
    new = (nk, nv, u_ext[:, -(CONV_K - 1):], p_ext[:, -POOL_PREV:], up_ext[:, -(CONV_K - 1):])
    return x, new


def setup_inputs(seed: int = 0) -> dict:
    key = jax.random.key(seed)
    ks = jax.random.split(key, 24)
    nrm = lambda kk, shape, s=1.0: jax.random.normal(kk, shape, jnp.float32) * s
    wc = min(WINDOW, PAST_LEN)
    return {
        'x_prompt': nrm(ks[0], (BATCH, SEQ, D_MODEL)),
        'x_sample': nrm(ks[1], (DEC_BATCH, DEC_SEQ, D_MODEL)),
        'cache_k': nrm(ks[2], (DEPTH, DEC_BATCH, wc, KV_HEADS, HEAD_DIM)),
        'cache_v': nrm(ks[3], (DEPTH, DEC_BATCH, wc, KV_HEADS, HEAD_DIM)),
        'state_conv': nrm(ks[4], (DEPTH, DEC_BATCH, CONV_K - 1, CONV_W)),
        'state_pool': nrm(ks[5], (DEPTH, DEC_BATCH, POOL_PREV, POOL_W)),
        'state_ffn': nrm(ks[6], (DEPTH, DEC_BATCH, CONV_K - 1, 2 * D_FF)),
        'rel_table': nrm(ks[7], (N_BUCKETS, N_HEADS), 0.5),
        'w_in': nrm(ks[8], (DEPTH, D_MODEL, IN_W), D_MODEL ** -0.5),
        'conv_w': nrm(ks[9], (DEPTH, CONV_K, CONV_W), CONV_K ** -0.5),
        'pool_w': nrm(ks[10], (DEPTH, N_POOL_GROUPS, POOL_GROUP, POOL_GROUP), POOL_GROUP ** -0.5),
        'pool_scale': 1.0 + nrm(ks[11], (DEPTH, POOL_W), 0.1),
        'sinks': nrm(ks[12], (DEPTH, N_HEADS), 0.5),
        'w_o': nrm(ks[13], (DEPTH, D_MODEL, D_MODEL), D_MODEL ** -0.5 * DEEPNORM_BETA),
        'ln1_g': 1.0 + nrm(ks[14], (DEPTH, D_MODEL), 0.02),
        'ln1_b': nrm(ks[15], (DEPTH, D_MODEL), 0.02),
        'w_up': nrm(ks[16], (DEPTH, D_MODEL, 2 * D_FF), D_MODEL ** -0.5),
        'ffn_conv_w': nrm(ks[17], (DEPTH, CONV_K, 2 * D_FF), CONV_K ** -0.5),
        'w_down': nrm(ks[18], (DEPTH, D_FF, D_MODEL), D_FF ** -0.5 * DEEPNORM_BETA),
        'ln2_g': 1.0 + nrm(ks[19], (DEPTH, D_MODEL), 0.02),
        'ln2_b': nrm(ks[20], (DEPTH, D_MODEL), 0.02),
    }


def reference(x_prompt, x_sample, cache_k, cache_v, state_conv, state_pool, state_ffn,
              rel_table, w_in, conv_w, pool_w, pool_scale, sinks, w_o, ln1_g, ln1_b,
              w_up, ffn_conv_w, w_down, ln2_g, ln2_b):
    bias_by_dist = rel_table.astype(jnp.float32)[_t5_bucket(jnp.arange(WINDOW))]
    bp = x_prompt.shape[0]
    zc = jnp.zeros((bp, CONV_K - 1, CONV_W), x_prompt.dtype)
    zp = jnp.zeros((bp, POOL_PREV, POOL_W), x_prompt.dtype)
    zf = jnp.zeros((bp, CONV_K - 1, 2 * D_FF), x_prompt.dtype)
    xp, xs = x_prompt, x_sample
    sp, ss = [], []
    for l in range(DEPTH):
        wts = (w_in[l], conv_w[l], pool_w[l], pool_scale[l], sinks[l], w_o[l], ln1_g[l], ln1_b[l],
               w_up[l], ffn_conv_w[l], w_down[l], ln2_g[l], ln2_b[l])
        xp, np_ = _layer(xp, 0, None, zc, zp, zf, bias_by_dist, *wts)
        xs, ns_ = _layer(xs, PAST_LEN, (cache_k[l], cache_v[l]), state_conv[l], state_pool[l],
                         state_ffn[l], bias_by_dist, *wts)
        sp.append(np_)
        ss.append(ns_)
    st = lambda lst, i: jnp.stack([e[i] for e in lst], 0)
    return (xp, xs,
            st(sp, 0), st(sp, 1), st(sp, 2), st(sp, 3), st(sp, 4),
            st(ss, 0), st(ss, 1), st(ss, 2), st(ss, 3), st(ss, 4))
```

```python
import functools
import math

import jax
import jax.numpy as jnp
from jax import lax
from jax.experimental import pallas as pl
from jax.experimental.pallas import tpu as pltpu

D_MODEL = 2048
N_HEADS = 16
KV_HEADS = 4
HEAD_DIM = 64
GROUP = N_HEADS // KV_HEADS
ATTN_W = N_HEADS * HEAD_DIM
KV_W = KV_HEADS * HEAD_DIM
WINDOW = 128
ATTN_SCALE = HEAD_DIM ** -0.5
NEG_INF = -1e30
N_BUCKETS = 32
MAX_DISTANCE = 128
CONV_W = D_MODEL // 4
CONV_K = 3
POOL_W = D_MODEL // 4
POOL_WINDOWS = (2, 4, 8, 16)
POOL_GROUP = POOL_W // len(POOL_WINDOWS)
POOL_PREV = 15
IN_W = ATTN_W + 2 * KV_W + 3 * CONV_W + POOL_W
D_FF = 5632
PAST_LEN = 16384
LN_EPS = 1e-5

HALO = 16
SUBLANES = 8
Q_BLK = 0
K_BLK = ATTN_W // KV_W
V_BLK = K_BLK + 1
GB_BLK = (ATTN_W + 2 * KV_W) // CONV_W
GC_BLK = GB_BLK + 1
H_BLK = GB_BLK + 2
PIN_BLK = GB_BLK + 3

VMEM_LIMIT = 56 * 1024 * 1024

BF16 = jnp.bfloat16
F32 = jnp.float32


def _params(*sem):
    return pltpu.CompilerParams(dimension_semantics=sem, vmem_limit_bytes=VMEM_LIMIT)


def _dot(a, b):
    return jnp.dot(a, b, preferred_element_type=F32)


def _layer_norm(y, g, b):
    mu = jnp.mean(y, -1, keepdims=True)
    yc = y - mu
    var = jnp.mean(yc * yc, -1, keepdims=True)
    return yc * lax.rsqrt(var + LN_EPS) * g + b


def _cast_body(w_ref, o_ref):
    o_ref[...] = w_ref[...].astype(BF16)


def _cast_bf16(w, rows):
    depth, k, n = w.shape
    return pl.pallas_call(
        _cast_body,
        grid=(depth, k // rows),
        in_specs=[pl.BlockSpec((None, rows, n), lambda l, i: (l, i, 0))],
        out_specs=pl.BlockSpec((None, rows, n), lambda l, i: (l, i, 0)),
        out_shape=jax.ShapeDtypeStruct(w.shape, BF16),
        compiler_params=_params("arbitrary", "arbitrary"),
        name="cast_bf16",
    )(w)


def _matmul_body(x_ref, w_ref, o_ref):
    o_ref[...] = _dot(x_ref[...].astype(BF16), w_ref[...])


def _matmul(x, w, layer, tm, tn):
    m, k = x.shape
    n = w.shape[2]
    return pl.pallas_call(
        _matmul_body,
        grid=(m // tm, n // tn),
        in_specs=[pl.BlockSpec((tm, k), lambda i, j: (i, 0)),
                  pl.BlockSpec((None, k, tn), lambda i, j: (layer, 0, j))],
        out_specs=pl.BlockSpec((tm, tn), lambda i, j: (i, j)),
        out_shape=jax.ShapeDtypeStruct((m, n), F32),
        compiler_params=_params("arbitrary", "arbitrary"),
        name="matmul",
    )(x, w)


def _mixers_body(decode, tq, new_rows, pos0, *refs):
    if decode:
        (q_ref, kc_ref, vc_ref, kp_ref, vp_ref, gb_ref, gc_ref, h_ref, pin_ref, uh_ref, ph_ref,
         bp_ref, bc_ref, sink_ref, cw_ref, pw_ref, ps_ref,
         cat_ref, uo_ref, ext_u, ext_p) = refs
    else:
        (q_ref, kc_ref, vc_ref, kp_ref, vp_ref, gb_ref, gc_ref, h_ref, pin_ref,
         gcp_ref, hp_ref, pinp_ref,
         bp_ref, bc_ref, sink_ref, cw_ref, pw_ref, ps_ref,
         cat_ref, uo_ref, ext_u, ext_p) = refs
    n = pl.program_id(1)
    rows = GROUP * tq

    q = q_ref[...]
    kc = kc_ref[...].astype(BF16)
    vc = vc_ref[...].astype(BF16)
    kp = kp_ref[...].astype(BF16)
    vp = vp_ref[...].astype(BF16)
    qi = lax.broadcasted_iota(jnp.int32, (rows, WINDOW), 0) & (tq - 1)
    kj = lax.broadcasted_iota(jnp.int32, (rows, WINDOW), 1)
    mask_p = kj > qi
    if not decode:
        mask_p = jnp.logical_and(mask_p, n > 0)
    mask_c = kj <= qi
    contract_last = (((1,), (1,)), ((), ()))
    head_out = []
    for kvh in range(KV_HEADS):
        lo = kvh * HEAD_DIM
        qs = jnp.concatenate(
            [q[:, (kvh * GROUP + g) * HEAD_DIM:(kvh * GROUP + g + 1) * HEAD_DIM] for g in range(GROUP)],
            axis=0).astype(BF16)
        sp = lax.dot_general(qs, kp[:, lo:lo + HEAD_DIM], contract_last, preferred_element_type=F32)
        sc = lax.dot_general(qs, kc[:, lo:lo + HEAD_DIM], contract_last, preferred_element_type=F32)
        sp = jnp.where(mask_p, sp * ATTN_SCALE + bp_ref[kvh], NEG_INF)
        sc = jnp.where(mask_c, sc * ATTN_SCALE + bc_ref[kvh], NEG_INF)
        sink = sink_ref[kvh]
        m = jnp.maximum(jnp.maximum(jnp.max(sp, -1, keepdims=True), jnp.max(sc, -1, keepdims=True)), sink)
        pp = jnp.exp(sp - m)
        pc = jnp.exp(sc - m)
        den = jnp.sum(pp, -1, keepdims=True) + jnp.sum(pc, -1, keepdims=True) + jnp.exp(sink - m)
        inv = 1.0 / den
        o = (_dot((pp * inv).astype(BF16), vp[:, lo:lo + HEAD_DIM])
             + _dot((pc * inv).astype(BF16), vc[:, lo:lo + HEAD_DIM]))
        head_out.extend(o[g * tq:(g + 1) * tq] for g in range(GROUP))
    attn = jnp.concatenate(head_out, axis=1)

    u = gc_ref[...] * h_ref[...]
    if decode:
        ext_u[0:HALO] = uh_ref[...]
    else:
        ext_u[0:HALO] = jnp.where(n > 0, gcp_ref[...] * hp_ref[...], 0.0)
    ext_u[HALO:HALO + tq] = u
    cw = cw_ref[...]
    conv = ext_u[HALO - 2:HALO - 2 + tq] * cw[0:1]
    conv = conv + ext_u[HALO - 1:HALO - 1 + tq] * cw[1:2]
    conv = conv + u * cw[2:3]
    c = gb_ref[...] * conv
    uo_ref[...] = ext_u[new_rows:new_rows + HALO]

    pin = pin_ref[...]
    if decode:
        ext_p[0:HALO] = ph_ref[...]
    else:
        ext_p[0:HALO] = jnp.where(n > 0, pinp_ref[...], 0.0)
    ext_p[HALO:HALO + tq] = pin
    pos = pos0 + n * tq + lax.broadcasted_iota(jnp.int32, (tq, 1), 0)
    pooled = []
    for g, w in enumerate(POOL_WINDOWS):
        lo = g * POOL_GROUP
        cur = pin[:, lo:lo + POOL_GROUP]
        win = ext_p[HALO - (w - 1):HALO - (w - 1) + tq, lo:lo + POOL_GROUP]
        for k in range(w - 2, -1, -1):
            win = win + ext_p[HALO - k:HALO - k + tq, lo:lo + POOL_GROUP]
        cnt = jnp.minimum(pos + 1, w).astype(F32)
        d = (win / cnt - cur).astype(BF16)
        pooled.append(_dot(d, pw_ref[g]))
    pm = jnp.concatenate(pooled, axis=1) * ps_ref[...]

    cat_ref[...] = jnp.concatenate([attn, c, pm], axis=1).astype(BF16)


def _mixers_prompt(z, tables, cw, pw, ps):
    b, s, _ = z.shape
    tq = WINDOW
    nb = s // tq
    bp, bc, sink = tables
    cur = lambda blk: (lambda i, n: (i, n, blk))
    prev = lambda blk: (lambda i, n: (i, jnp.maximum(n - 1, 0), blk))
    halo = lambda blk: (lambda i, n: (i, jnp.maximum(n * (tq // HALO) - 1, 0), blk))
    const3 = lambda i, n: (0, 0, 0)
    const2 = lambda i, n: (0, 0)
    in_specs = [
        pl.BlockSpec((None, tq, ATTN_W), cur(Q_BLK)),
        pl.BlockSpec((None, tq, KV_W), cur(K_BLK)),
        pl.BlockSpec((None, tq, KV_W), cur(V_BLK)),
        pl.BlockSpec((None, tq, KV_W), prev(K_BLK)),
        pl.BlockSpec((None, tq, KV_W), prev(V_BLK)),
        pl.BlockSpec((None, tq, CONV_W), cur(GB_BLK)),
        pl.BlockSpec((None, tq, CONV_W), cur(GC_BLK)),
        pl.BlockSpec((None, tq, CONV_W), cur(H_BLK)),
        pl.BlockSpec((None, tq, POOL_W), cur(PIN_BLK)),
        pl.BlockSpec((None, HALO, CONV_W), halo(GC_BLK)),
        pl.BlockSpec((None, HALO, CONV_W), halo(H_BLK)),
        pl.BlockSpec((None, HALO, POOL_W), halo(PIN_BLK)),
        pl.BlockSpec(bp.shape, const3),
        pl.BlockSpec(bc.shape, const3),
        pl.BlockSpec(sink.shape, const3),
        pl.BlockSpec(cw.shape, const2),
        pl.BlockSpec(pw.shape, const3),
        pl.BlockSpec(ps.shape, const2),
    ]
    return pl.pallas_call(
        functools.partial(_mixers_body, False, tq, tq, 0),
        grid=(b, nb),
        in_specs=in_specs,
        out_specs=[pl.BlockSpec((None, tq, D_MODEL), lambda i, n: (i, n, 0)),
                   pl.BlockSpec((None, HALO, CONV_W), lambda i, n: (i, 0, 0))],
        out_shape=[jax.ShapeDtypeStruct((b, s, D_MODEL), BF16),
                   jax.ShapeDtypeStruct((b, HALO, CONV_W), F32)],
        scratch_shapes=[pltpu.VMEM((HALO + tq, CONV_W), F32), pltpu.VMEM((HALO + tq, POOL_W), F32)],
        compiler_params=_params("arbitrary", "arbitrary"),
        name="mixers_prompt",
    )(z, z, z, z, z, z, z, z, z, z, z, z, bp, bc, sink, cw, pw, ps)


def _mixers_decode(zs, kn, vn, ck, cv, uh, ph, tables, cw, pw, ps):
    b = zs.shape[0]
    tq = SUBLANES
    bp, bc, sink = tables
    cur = lambda blk: (lambda i, n: (i, 0, blk))
    whole = lambda i, n: (i, 0, 0)
    const3 = lambda i, n: (0, 0, 0)
    const2 = lambda i, n: (0, 0)
    in_specs = [
        pl.BlockSpec((None, tq, ATTN_W), cur(Q_BLK)),
        pl.BlockSpec((None, WINDOW, KV_W), whole),
        pl.BlockSpec((None, WINDOW, KV_W), whole),
        pl.BlockSpec((None, WINDOW, KV_W), whole),
        pl.BlockSpec((None, WINDOW, KV_W), whole),
        pl.BlockSpec((None, tq, CONV_W), cur(GB_BLK)),
        pl.BlockSpec((None, tq, CONV_W), cur(GC_BLK)),
        pl.BlockSpec((None, tq, CONV_W), cur(H_BLK)),
        pl.BlockSpec((None, tq, POOL_W), cur(PIN_BLK)),
        pl.BlockSpec((None, HALO, CONV_W), whole),
        pl.BlockSpec((None, HALO, POOL_W), whole),
        pl.BlockSpec(bp.shape, const3),
        pl.BlockSpec(bc.shape, const3),
        pl.BlockSpec(sink.shape, const3),
        pl.BlockSpec(cw.shape, const2),
        pl.BlockSpec(pw.shape, const3),
        pl.BlockSpec(ps.shape, const2),
    ]
    return pl.pallas_call(
        functools.partial(_mixers_body, True, tq, 1, PAST_LEN),
        grid=(b, 1),
        in_specs=in_specs,
        out_specs=[pl.BlockSpec((None, tq, D_MODEL), whole),
                   pl.BlockSpec((None, HALO, CONV_W), whole)],
        out_shape=[jax.ShapeDtypeStruct((b, tq, D_MODEL), BF16),
                   jax.ShapeDtypeStruct((b, HALO, CONV_W), F32)],
        scratch_shapes=[pltpu.VMEM((HALO + tq, CONV_W), F32), pltpu.VMEM((HALO + tq, POOL_W), F32)],
        compiler_params=_params("arbitrary", "arbitrary"),
        name="mixers_decode",
    )(zs, kn, vn, ck, cv, zs, zs, zs, zs, uh, ph, bp, bc, sink, cw, pw, ps)


def _attn_tables(bias_by_dist, sinks_l, tq):
    qi = jnp.arange(tq)[:, None]
    dist_p = jnp.clip(qi + WINDOW - jnp.arange(WINDOW)[None, :], 0, WINDOW - 1)
    dist_c = jnp.clip(qi - jnp.arange(WINDOW)[None, :], 0, WINDOW - 1)
    stack = lambda t: jnp.moveaxis(t, -1, 0).reshape(KV_HEADS, GROUP * tq, t.shape[1])
    bp = stack(bias_by_dist[dist_p])
    bc = stack(bias_by_dist[dist_c])
    sink = jnp.broadcast_to(sinks_l.astype(F32).reshape(KV_HEADS, GROUP, 1, 1),
                            (KV_HEADS, GROUP, tq, 1)).reshape(KV_HEADS, GROUP * tq, 1)
    return bp, bc, sink


def _t5_bucket(n):
    max_exact = N_BUCKETS // 2
    nf = jnp.maximum(n, 1).astype(F32)
    large = max_exact + (jnp.log(nf / max_exact) / math.log(MAX_DISTANCE / max_exact)
                         * (N_BUCKETS - max_exact)).astype(jnp.int32)
    large = jnp.minimum(large, N_BUCKETS - 1)
    return jnp.where(n < max_exact, n, large)


def _proj_ln_body(nk, alpha, a_ref, w_ref, x_ref, g_ref, b_ref, o_ref, *scratch):
    part = _dot(a_ref[...], w_ref[...])
    if nk == 1:
        o_ref[...] = _layer_norm(alpha * x_ref[...] + part, g_ref[...], b_ref[...])
        return
    acc_ref, = scratch
    k = pl.program_id(1)

    @pl.when(k == 0)
    def _():
        acc_ref[...] = jnp.zeros_like(acc_ref)

    acc_ref[...] += part

    @pl.when(k == nk - 1)
    def _():
        o_ref[...] = _layer_norm(alpha * x_ref[...] + acc_ref[...], g_ref[...], b_ref[...])


def _proj_ln(a, w, layer, x, g, b, alpha, tm, tk):
    m, k = a.shape
    n = w.shape[2]
    nk = k // tk
    return pl.pallas_call(
        functools.partial(_proj_ln_body, nk, alpha),
        grid=(m // tm, nk),
        in_specs=[pl.BlockSpec((tm, tk), lambda i, kk: (i, kk)),
                  pl.BlockSpec((None, tk, n), lambda i, kk: (layer, kk, 0)),
                  pl.BlockSpec((tm, n), lambda i, kk: (i, 0)),
                  pl.BlockSpec((None, 1, n), lambda i, kk: (layer, 0, 0)),
                  pl.BlockSpec((None, 1, n), lambda i, kk: (layer, 0, 0))],
        out_specs=pl.BlockSpec((tm, n), lambda i, kk: (i, 0)),
        out_shape=jax.ShapeDtypeStruct((m, n), F32),
        scratch_shapes=[] if nk == 1 else [pltpu.VMEM((tm, n), F32)],
        compiler_params=_params("arbitrary", "arbitrary"),
        name="proj_ln",
    )(a, w, x, g, b)


def _ffn_body(tm, tiles_per_seq, nc, alpha, x_ref, wg_ref, wv_ref, wd_ref, cwg_ref, cwv_ref, g_ref, b_ref,
              o_ref, tg_ref, tv_ref, xb_ref, acc_ref, ext_g, ext_v, carry_g, carry_v):
    i = pl.program_id(0)
    c = pl.program_id(1)

    @pl.when(c == 0)
    def _():
        xb_ref[...] = x_ref[...].astype(BF16)
        acc_ref[...] = jnp.zeros_like(acc_ref)

    xb = xb_ref[...]
    seq_start = (i % tiles_per_seq) == 0

    def conv_half(w_ref, cw_ref, ext, carry, tail_ref):
        up = _dot(xb, w_ref[...])

        @pl.when(seq_start)
        def _():
            ext[0:SUBLANES] = jnp.zeros((SUBLANES, ext.shape[1]), F32)

        @pl.when(jnp.logical_not(seq_start))
        def _():
            ext[0:SUBLANES] = carry[c]

        ext[SUBLANES:SUBLANES + tm] = up
        tail = up[tm - SUBLANES:tm]
        carry[c] = tail
        tail_ref[...] = tail
        cw = cw_ref[...]
        y = ext[SUBLANES - 2:SUBLANES - 2 + tm] * cw[0:1]
        y = y + ext[SUBLANES - 1:SUBLANES - 1 + tm] * cw[1:2]
        return y + up * cw[2:3]

    hg = conv_half(wg_ref, cwg_ref, ext_g, carry_g, tg_ref)
    hv = conv_half(wv_ref, cwv_ref, ext_v, carry_v, tv_ref)
    act = (jax.nn.silu(hg) * hv).astype(BF16)
    acc_ref[...] += _dot(act, wd_ref[...])

    @pl.when(c == nc - 1)
    def _():
        o_ref[...] = _layer_norm(alpha * x_ref[...] + acc_ref[...], g_ref[...], b_ref[...])


def _ffn_prompt(x, w_up, w_down, cw, g, b, layer, alpha, seq, tm, tf):
    m = x.shape[0]
    nc = D_FF // tf
    tiles_per_seq = seq // tm
    out, tail_g, tail_v = pl.pallas_call(
        functools.partial(_ffn_body, tm, tiles_per_seq, nc, alpha),
        grid=(m // tm, nc),
        in_specs=[pl.BlockSpec((tm, D_MODEL), lambda i, c: (i, 0)),
                  pl.BlockSpec((None, D_MODEL, tf), lambda i, c: (layer, 0, c)),
                  pl.BlockSpec((None, D_MODEL, tf), lambda i, c: (layer, 0, nc + c)),
                  pl.BlockSpec((None, tf, D_MODEL), lambda i, c: (layer, c, 0)),
                  pl.BlockSpec((None, CONV_K, tf), lambda i, c: (layer, 0, c)),
                  pl.BlockSpec((None, CONV_K, tf), lambda i, c: (layer, 0, nc + c)),
                  pl.BlockSpec((None, 1, D_MODEL), lambda i, c: (layer, 0, 0)),
                  pl.BlockSpec((None, 1, D_MODEL), lambda i, c: (layer, 0, 0))],
        out_specs=[pl.BlockSpec((tm, D_MODEL), lambda i, c: (i, 0)),
                   pl.BlockSpec((None, SUBLANES, tf), lambda i, c: (i, 0, c)),
                   pl.BlockSpec((None, SUBLANES, tf), lambda i, c: (i, 0, c))],
        out_shape=[jax.ShapeDtypeStruct((m, D_MODEL), F32),
                   jax.ShapeDtypeStruct((m // tm, SUBLANES, D_FF), F32),
                   jax.ShapeDtypeStruct((m // tm, SUBLANES, D_FF), F32)],
        scratch_shapes=[pltpu.VMEM((tm, D_MODEL), BF16),
                        pltpu.VMEM((tm, D_MODEL), F32),
                        pltpu.VMEM((SUBLANES + tm, tf), F32),
                        pltpu.VMEM((SUBLANES + tm, tf), F32),
                        pltpu.VMEM((nc, SUBLANES, tf), F32),
                        pltpu.VMEM((nc, SUBLANES, tf), F32)],
        compiler_params=_params("arbitrary", "arbitrary"),
        name="ffn_prompt",
    )(x, w_up, w_up, w_down, cw, cw, g, b)
    last = slice(tiles_per_seq - 1, None, tiles_per_seq)
    return out, jnp.concatenate([tail_g[last], tail_v[last]], -1)


def _gate_decode_body(ug_ref, uv_ref, sg_ref, sv_ref, cwg_ref, cwv_ref, o_ref):
    def conv(u_ref, s_ref, cw_ref):
        cw = cw_ref[...]
        y = s_ref[0] * cw[0:1]
        y = y + s_ref[1] * cw[1:2]
        return y + u_ref[...] * cw[2:3]

    hg = conv(ug_ref, sg_ref, cwg_ref)
    hv = conv(uv_ref, sv_ref, cwv_ref)
    o_ref[...] = (jax.nn.silu(hg) * hv).astype(BF16)


def _gate_decode(up, state, cw, layer, tf):
    b = up.shape[0]
    nc = D_FF // tf
    return pl.pallas_call(
        _gate_decode_body,
        grid=(nc,),
        in_specs=[pl.BlockSpec((b, tf), lambda c: (0, c)),
                  pl.BlockSpec((b, tf), lambda c: (0, nc + c)),
                  pl.BlockSpec((CONV_K - 1, b, tf), lambda c: (0, 0, c)),
                  pl.BlockSpec((CONV_K - 1, b, tf), lambda c: (0, 0, nc + c)),
                  pl.BlockSpec((None, CONV_K, tf), lambda c: (layer, 0, c)),
                  pl.BlockSpec((None, CONV_K, tf), lambda c: (layer, 0, nc + c))],
        out_specs=pl.BlockSpec((b, tf), lambda c: (0, c)),
        out_shape=jax.ShapeDtypeStruct((b, D_FF), BF16),
        compiler_params=_params("arbitrary"),
        name="gate_decode",
    )(up, up, state, state, cw, cw)


def kernel(x_prompt, x_sample, cache_k, cache_v, state_conv, state_pool, state_ffn, rel_table, w_in, conv_w,
           pool_w, pool_scale, sinks, w_o, ln1_g, ln1_b, w_up, ffn_conv_w, w_down, ln2_g, ln2_b):
    depth = w_in.shape[0]
    bp_, seq, _ = x_prompt.shape
    bs = x_sample.shape[0]
    assert x_sample.shape[1] == 1 and cache_k.shape[2] == WINDOW
    alpha = (2 * depth) ** 0.25

    w_in_b = _cast_bf16(w_in, 256)
    w_o_b = _cast_bf16(w_o, 256)
    w_up_b = _cast_bf16(w_up, 128)
    w_down_b = _cast_bf16(w_down, 512)
    pool_w_b = pool_w.astype(BF16)

    bias_by_dist = rel_table.astype(F32)[_t5_bucket(jnp.arange(WINDOW))]

    ln1_g, ln1_b, ln2_g, ln2_b = (t[:, None, :] for t in (ln1_g, ln1_b, ln2_g, ln2_b))

    xp = x_prompt.reshape(bp_ * seq, D_MODEL)
    xs = x_sample.reshape(bs, D_MODEL)
    pad_rows = lambda t, rows: jnp.pad(t[:, None, :], ((0, 0), (0, rows - 1), (0, 0)))

    outs_p, outs_s = [], []
    for l in range(depth):
        tab_p = _attn_tables(bias_by_dist, sinks[l], WINDOW)
        tab_s = _attn_tables(bias_by_dist, sinks[l], SUBLANES)
        ps = pool_scale[l][None, :]

        z = _matmul(xp, w_in_b, l, 1024, 512).reshape(bp_, seq, IN_W)
        cat, u_tail = _mixers_prompt(z, tab_p, conv_w[l], pool_w_b[l], ps)
        x1 = _proj_ln(cat.reshape(bp_ * seq, D_MODEL), w_o_b, l, xp, ln1_g, ln1_b, alpha, 512, D_MODEL)
        xp, up_tail = _ffn_prompt(x1, w_up_b, w_down_b, ffn_conv_w, ln2_g, ln2_b, l, alpha, seq, 512, 512)
        outs_p.append((
            z[:, seq - WINDOW:, ATTN_W:ATTN_W + KV_W].reshape(bp_, WINDOW, KV_HEADS, HEAD_DIM),
            z[:, seq - WINDOW:, ATTN_W + KV_W:ATTN_W + 2 * KV_W].reshape(bp_, WINDOW, KV_HEADS, HEAD_DIM),
            u_tail[:, HALO - (CONV_K - 1):],
            z[:, seq - POOL_PREV:, IN_W - POOL_W:],
            up_tail[:, SUBLANES - (CONV_K - 1):],
        ))

        zs = _matmul(xs, w_in_b, l, bs, 512)
        ck = cache_k[l].reshape(bs, WINDOW, KV_W)
        cv = cache_v[l].reshape(bs, WINDOW, KV_W)
        uh = jnp.pad(state_conv[l], ((0, 0), (HALO - (CONV_K - 1), 0), (0, 0)))
        ph = jnp.pad(state_pool[l], ((0, 0), (HALO - POOL_PREV, 0), (0, 0)))
        k_new = zs[:, ATTN_W:ATTN_W + KV_W]
        v_new = zs[:, ATTN_W + KV_W:ATTN_W + 2 * KV_W]
        cat_s, u_new = _mixers_decode(pad_rows(zs, SUBLANES), pad_rows(k_new, WINDOW), pad_rows(v_new, WINDOW),
                                      ck, cv, uh, ph, tab_s, conv_w[l], pool_w_b[l], ps)
        x1s = _proj_ln(cat_s[:, 0], w_o_b, l, xs, ln1_g, ln1_b, alpha, bs, D_MODEL)
        up_s = _matmul(x1s, w_up_b, l, bs, 1024)
        act_s = _gate_decode(up_s, jnp.swapaxes(state_ffn[l], 0, 1), ffn_conv_w, l, 512)
        xs = _proj_ln(act_s, w_down_b, l, x1s, ln2_g, ln2_b, alpha, bs, 512)
        outs_s.append((
            jnp.concatenate([ck[:, 1:], k_new[:, None]], 1).reshape(bs, WINDOW, KV_HEADS, HEAD_DIM),
            jnp.concatenate([cv[:, 1:], v_new[:, None]], 1).reshape(bs, WINDOW, KV_HEADS, HEAD_DIM),
            u_new[:, HALO - (CONV_K - 1):],
            jnp.concatenate([state_pool[l][:, 1:], zs[:, None, IN_W - POOL_W:]], 1),
            jnp.concatenate([state_ffn[l][:, 1:], up_s[:, None, :]], 1),
        ))

    st = lambda lst, i: jnp.stack([e[i] for e in lst], 0)
    return (xp.reshape(bp_, seq, D_MODEL), xs.reshape(bs, 1, D_MODEL),
            st(outs_p, 0), st(outs_p, 1), st(outs_p, 2), st(outs_p, 3), st(outs_p, 4),
            st(outs_s, 0), st(outs_s, 1), st(outs_s, 2), st(outs_s, 3), st(outs_s, 4))
```

```python
import functools
import math

import jax
import jax.numpy as jnp
from jax import lax
from jax.experimental import pallas as pl
from jax.experimental.pallas import tpu as pltpu

D_MODEL = 2048
N_HEADS = 16
KV_HEADS = 4
HEAD_DIM = 64
GROUP = N_HEADS // KV_HEADS
ATTN_W = N_HEADS * HEAD_DIM
KV_W = KV_HEADS * HEAD_DIM
WINDOW = 128
ATTN_SCALE = HEAD_DIM ** -0.5
NEG_INF = -1e30
N_BUCKETS = 32
MAX_DISTANCE = 128
CONV_W = D_MODEL // 4
CONV_K = 3
POOL_W = D_MODEL // 4
POOL_WINDOWS = (2, 4, 8, 16)
POOL_GROUP = POOL_W // len(POOL_WINDOWS)
POOL_PREV = 15
IN_W = ATTN_W + 2 * KV_W + 3 * CONV_W + POOL_W
D_FF = 5632
PAST_LEN = 16384
LN_EPS = 1e-5

HALO = 16
SUBLANES = 8
Q_BLK = 0
K_BLK = ATTN_W // KV_W
V_BLK = K_BLK + 1
GB_BLK = (ATTN_W + 2 * KV_W) // CONV_W
GC_BLK = GB_BLK + 1
H_BLK = GB_BLK + 2
PIN_BLK = GB_BLK + 3

FFN_TM = 512
FFN_TF = 512
FFN_NS = 2

VMEM_LIMIT = 56 * 1024 * 1024

BF16 = jnp.bfloat16
F32 = jnp.float32


def _params(*sem, flags=None):
    return pltpu.CompilerParams(dimension_semantics=sem, vmem_limit_bytes=VMEM_LIMIT, flags=flags)


def _dot(a, b):
    return jnp.dot(a, b, preferred_element_type=F32)


def _layer_norm(y, g, b):
    mu = jnp.mean(y, -1, keepdims=True)
    yc = y - mu
    var = jnp.mean(yc * yc, -1, keepdims=True)
    return yc * lax.rsqrt(var + LN_EPS) * g + b


def _cast_body(w_ref, o_ref):
    o_ref[...] = w_ref[...].astype(BF16)


def _cast_bf16(w, rows, cols=None, col_perm=None):
    depth, k, n = w.shape
    cols = n if cols is None else cols
    col_perm = (lambda j: j) if col_perm is None else col_perm
    return pl.pallas_call(
        _cast_body,
        grid=(depth, k // rows, n // cols),
        in_specs=[pl.BlockSpec((None, rows, cols), lambda l, i, j: (l, i, j))],
        out_specs=pl.BlockSpec((None, rows, cols), lambda l, i, j: (l, i, col_perm(j))),
        out_shape=jax.ShapeDtypeStruct(w.shape, BF16),
        compiler_params=_params("arbitrary", "arbitrary", "arbitrary"),
        name="cast_bf16",
    )(w)


def _matmul_body(x_ref, w_ref, o_ref):
    o_ref[...] = _dot(x_ref[...].astype(BF16), w_ref[...])


def _matmul(x, w, layer, tm, tn):
    m, k = x.shape
    n = w.shape[2]
    return pl.pallas_call(
        _matmul_body,
        grid=(m // tm, n // tn),
        in_specs=[pl.BlockSpec((tm, k), lambda i, j: (i, 0)),
                  pl.BlockSpec((None, k, tn), lambda i, j: (layer, 0, j))],
        out_specs=pl.BlockSpec((tm, tn), lambda i, j: (i, j)),
        out_shape=jax.ShapeDtypeStruct((m, n), F32),
        compiler_params=_params("arbitrary", "arbitrary"),
        name="matmul",
    )(x, w)


def _mixers_body(decode, tq, new_rows, pos0, *refs):
    if decode:
        (q_ref, kc_ref, vc_ref, kp_ref, vp_ref, gb_ref, gc_ref, h_ref, pin_ref, uh_ref, ph_ref,
         bp_ref, bc_ref, sink_ref, cw_ref, pw_ref, ps_ref,
         cat_ref, uo_ref, ext_u, ext_p) = refs
    else:
        (q_ref, kc_ref, vc_ref, kp_ref, vp_ref, gb_ref, gc_ref, h_ref, pin_ref,
         gcp_ref, hp_ref, pinp_ref,
         bp_ref, bc_ref, sink_ref, cw_ref, pw_ref, ps_ref,
         cat_ref, uo_ref, ext_u, ext_p) = refs
    n = pl.program_id(1)
    rows = GROUP * tq

    q = q_ref[...]
    kc = kc_ref[...].astype(BF16)
    vc = vc_ref[...].astype(BF16)
    kp = kp_ref[...].astype(BF16)
    vp = vp_ref[...].astype(BF16)
    qi = lax.broadcasted_iota(jnp.int32, (rows, WINDOW), 0) & (tq - 1)
    kj = lax.broadcasted_iota(jnp.int32, (rows, WINDOW), 1)
    mask_p = kj > qi
    if not decode:
        mask_p = jnp.logical_and(mask_p, n > 0)
    mask_c = kj <= qi
    contract_last = (((1,), (1,)), ((), ()))
    head_out = []
    for kvh in range(KV_HEADS):
        lo = kvh * HEAD_DIM
        qs = jnp.concatenate(
            [q[:, (kvh * GROUP + g) * HEAD_DIM:(kvh * GROUP + g + 1) * HEAD_DIM] for g in range(GROUP)],
            axis=0).astype(BF16)
        sp = lax.dot_general(qs, kp[:, lo:lo + HEAD_DIM], contract_last, preferred_element_type=F32)
        sc = lax.dot_general(qs, kc[:, lo:lo + HEAD_DIM], contract_last, preferred_element_type=F32)
        sp = jnp.where(mask_p, sp * ATTN_SCALE + bp_ref[kvh], NEG_INF)
        sc = jnp.where(mask_c, sc * ATTN_SCALE + bc_ref[kvh], NEG_INF)
        sink = sink_ref[kvh]
        m = jnp.maximum(jnp.maximum(jnp.max(sp, -1, keepdims=True), jnp.max(sc, -1, keepdims=True)), sink)
        pp = jnp.exp(sp - m)
        pc = jnp.exp(sc - m)
        den = jnp.sum(pp, -1, keepdims=True) + jnp.sum(pc, -1, keepdims=True) + jnp.exp(sink - m)
        inv = 1.0 / den
        o = (_dot((pp * inv).astype(BF16), vp[:, lo:lo + HEAD_DIM])
             + _dot((pc * inv).astype(BF16), vc[:, lo:lo + HEAD_DIM]))
        head_out.extend(o[g * tq:(g + 1) * tq] for g in range(GROUP))
    attn = jnp.concatenate(head_out, axis=1)

    u = gc_ref[...] * h_ref[...]
    if decode:
        ext_u[0:HALO] = uh_ref[...]
    else:
        ext_u[0:HALO] = jnp.where(n > 0, gcp_ref[...] * hp_ref[...], 0.0)
    ext_u[HALO:HALO + tq] = u
    cw = cw_ref[...]
    conv = ext_u[HALO - 2:HALO - 2 + tq] * cw[0:1]
    conv = conv + ext_u[HALO - 1:HALO - 1 + tq] * cw[1:2]
    conv = conv + u * cw[2:3]
    c = gb_ref[...] * conv
    uo_ref[...] = ext_u[new_rows:new_rows + HALO]

    pin = pin_ref[...]
    if decode:
        ext_p[0:HALO] = ph_ref[...]
    else:
        ext_p[0:HALO] = jnp.where(n > 0, pinp_ref[...], 0.0)
    ext_p[HALO:HALO + tq] = pin
    pos = pos0 + n * tq + lax.broadcasted_iota(jnp.int32, (tq, 1), 0)
    pooled = []
    for g, w in enumerate(POOL_WINDOWS):
        lo = g * POOL_GROUP
        cur = pin[:, lo:lo + POOL_GROUP]
        win = ext_p[HALO - (w - 1):HALO - (w - 1) + tq, lo:lo + POOL_GROUP]
        for k in range(w - 2, -1, -1):
            win = win + ext_p[HALO - k:HALO - k + tq, lo:lo + POOL_GROUP]
        cnt = jnp.minimum(pos + 1, w).astype(F32)
        d = (win / cnt - cur).astype(BF16)
        pooled.append(_dot(d, pw_ref[g]))
    pm = jnp.concatenate(pooled, axis=1) * ps_ref[...]

    cat_ref[...] = jnp.concatenate([attn, c, pm], axis=1).astype(BF16)


def _mixers_prompt(z, tables, cw, pw, ps):
    b, s, _ = z.shape
    tq = WINDOW
    nb = s // tq
    bp, bc, sink = tables
    cur = lambda blk: (lambda i, n: (i, n, blk))
    prev = lambda blk: (lambda i, n: (i, jnp.maximum(n - 1, 0), blk))
    halo = lambda blk: (lambda i, n: (i, jnp.maximum(n * (tq // HALO) - 1, 0), blk))
    const3 = lambda i, n: (0, 0, 0)
    const2 = lambda i, n: (0, 0)
    in_specs = [
        pl.BlockSpec((None, tq, ATTN_W), cur(Q_BLK)),
        pl.BlockSpec((None, tq, KV_W), cur(K_BLK)),
        pl.BlockSpec((None, tq, KV_W), cur(V_BLK)),
        pl.BlockSpec((None, tq, KV_W), prev(K_BLK)),
        pl.BlockSpec((None, tq, KV_W), prev(V_BLK)),
        pl.BlockSpec((None, tq, CONV_W), cur(GB_BLK)),
        pl.BlockSpec((None, tq, CONV_W), cur(GC_BLK)),
        pl.BlockSpec((None, tq, CONV_W), cur(H_BLK)),
        pl.BlockSpec((None, tq, POOL_W), cur(PIN_BLK)),
        pl.BlockSpec((None, HALO, CONV_W), halo(GC_BLK)),
        pl.BlockSpec((None, HALO, CONV_W), halo(H_BLK)),
        pl.BlockSpec((None, HALO, POOL_W), halo(PIN_BLK)),
        pl.BlockSpec(bp.shape, const3),
        pl.BlockSpec(bc.shape, const3),
        pl.BlockSpec(sink.shape, const3),
        pl.BlockSpec(cw.shape, const2),
        pl.BlockSpec(pw.shape, const3),
        pl.BlockSpec(ps.shape, const2),
    ]
    return pl.pallas_call(
        functools.partial(_mixers_body, False, tq, tq, 0),
        grid=(b, nb),
        in_specs=in_specs,
        out_specs=[pl.BlockSpec((None, tq, D_MODEL), lambda i, n: (i, n, 0)),
                   pl.BlockSpec((None, HALO, CONV_W), lambda i, n: (i, 0, 0))],
        out_shape=[jax.ShapeDtypeStruct((b, s, D_MODEL), BF16),
                   jax.ShapeDtypeStruct((b, HALO, CONV_W), F32)],
        scratch_shapes=[pltpu.VMEM((HALO + tq, CONV_W), F32), pltpu.VMEM((HALO + tq, POOL_W), F32)],
        compiler_params=_params("arbitrary", "arbitrary"),
        name="mixers_prompt",
    )(z, z, z, z, z, z, z, z, z, z, z, z, bp, bc, sink, cw, pw, ps)


def _mixers_decode(zs, kn, vn, ck, cv, uh, ph, tables, cw, pw, ps):
    b = zs.shape[0]
    tq = SUBLANES
    bp, bc, sink = tables
    cur = lambda blk: (lambda i, n: (i, 0, blk))
    whole = lambda i, n: (i, 0, 0)
    const3 = lambda i, n: (0, 0, 0)
    const2 = lambda i, n: (0, 0)
    in_specs = [
        pl.BlockSpec((None, tq, ATTN_W), cur(Q_BLK)),
        pl.BlockSpec((None, WINDOW, KV_W), whole),
        pl.BlockSpec((None, WINDOW, KV_W), whole),
        pl.BlockSpec((None, WINDOW, KV_W), whole),
        pl.BlockSpec((None, WINDOW, KV_W), whole),
        pl.BlockSpec((None, tq, CONV_W), cur(GB_BLK)),
        pl.BlockSpec((None, tq, CONV_W), cur(GC_BLK)),
        pl.BlockSpec((None, tq, CONV_W), cur(H_BLK)),
        pl.BlockSpec((None, tq, POOL_W), cur(PIN_BLK)),
        pl.BlockSpec((None, HALO, CONV_W), whole),
        pl.BlockSpec((None, HALO, POOL_W), whole),
        pl.BlockSpec(bp.shape, const3),
        pl.BlockSpec(bc.shape, const3),
        pl.BlockSpec(sink.shape, const3),
        pl.BlockSpec(cw.shape, const2),
        pl.BlockSpec(pw.shape, const3),
        pl.BlockSpec(ps.shape, const2),
    ]
    return pl.pallas_call(
        functools.partial(_mixers_body, True, tq, 1, PAST_LEN),
        grid=(b, 1),
        in_specs=in_specs,
        out_specs=[pl.BlockSpec((None, tq, D_MODEL), whole),
                   pl.BlockSpec((None, HALO, CONV_W), whole)],
        out_shape=[jax.ShapeDtypeStruct((b, tq, D_MODEL), BF16),
                   jax.ShapeDtypeStruct((b, HALO, CONV_W), F32)],
        scratch_shapes=[pltpu.VMEM((HALO + tq, CONV_W), F32), pltpu.VMEM((HALO + tq, POOL_W), F32)],
        compiler_params=_params("arbitrary", "arbitrary"),
        name="mixers_decode",
    )(zs, kn, vn, ck, cv, zs, zs, zs, zs, uh, ph, bp, bc, sink, cw, pw, ps)


def _attn_tables(bias_by_dist, sinks_l, tq):
    qi = jnp.arange(tq)[:, None]
    dist_p = jnp.clip(qi + WINDOW - jnp.arange(WINDOW)[None, :], 0, WINDOW - 1)
    dist_c = jnp.clip(qi - jnp.arange(WINDOW)[None, :], 0, WINDOW - 1)
    stack = lambda t: jnp.moveaxis(t, -1, 0).reshape(KV_HEADS, GROUP * tq, t.shape[1])
    bp = stack(bias_by_dist[dist_p])
    bc = stack(bias_by_dist[dist_c])
    sink = jnp.broadcast_to(sinks_l.astype(F32).reshape(KV_HEADS, GROUP, 1, 1),
                            (KV_HEADS, GROUP, tq, 1)).reshape(KV_HEADS, GROUP * tq, 1)
    return bp, bc, sink


def _t5_bucket(n):
    max_exact = N_BUCKETS // 2
    nf = jnp.maximum(n, 1).astype(F32)
    large = max_exact + (jnp.log(nf / max_exact) / math.log(MAX_DISTANCE / max_exact)
                         * (N_BUCKETS - max_exact)).astype(jnp.int32)
    large = jnp.minimum(large, N_BUCKETS - 1)
    return jnp.where(n < max_exact, n, large)


def _proj_ln_body(nk, alpha, a_ref, w_ref, x_ref, g_ref, b_ref, o_ref, *scratch):
    part = _dot(a_ref[...], w_ref[...])
    if nk == 1:
        o_ref[...] = _layer_norm(alpha * x_ref[...] + part, g_ref[...], b_ref[...])
        return
    acc_ref, = scratch
    k = pl.program_id(1)

    @pl.when(k == 0)
    def _():
        acc_ref[...] = jnp.zeros_like(acc_ref)

    acc_ref[...] += part

    @pl.when(k == nk - 1)
    def _():
        o_ref[...] = _layer_norm(alpha * x_ref[...] + acc_ref[...], g_ref[...], b_ref[...])


def _proj_ln(a, w, layer, x, g, b, alpha, tm, tk):
    m, k = a.shape
    n = w.shape[2]
    nk = k // tk
    return pl.pallas_call(
        functools.partial(_proj_ln_body, nk, alpha),
        grid=(m // tm, nk),
        in_specs=[pl.BlockSpec((tm, tk), lambda i, kk: (i, kk)),
                  pl.BlockSpec((None, tk, n), lambda i, kk: (layer, kk, 0)),
                  pl.BlockSpec((tm, n), lambda i, kk: (i, 0)),
                  pl.BlockSpec((None, 1, n), lambda i, kk: (layer, 0, 0)),
                  pl.BlockSpec((None, 1, n), lambda i, kk: (layer, 0, 0))],
        out_specs=pl.BlockSpec((tm, n), lambda i, kk: (i, 0)),
        out_shape=jax.ShapeDtypeStruct((m, n), F32),
        scratch_shapes=[] if nk == 1 else [pltpu.VMEM((tm, n), F32)],
        compiler_params=_params("arbitrary", "arbitrary"),
        name="proj_ln",
    )(a, w, x, g, b)


def _ffn_body(tm, tf, ns, tiles_per_seq, nc, alpha, x_ref, wu_ref, wd_ref, cw_ref, g_ref, b_ref,
              o_ref, tail_ref, xb_ref, acc_ref, ext, carry):
    i = pl.program_id(0)
    c = pl.program_id(1)
    ts = tm // ns

    @pl.when(c == 0)
    def _():
        xb_ref[...] = x_ref[...].astype(BF16)
        acc_ref[...] = jnp.zeros_like(acc_ref)

    seq_start = (i % tiles_per_seq) == 0

    def up_proj(h, prev_tail):
        up = _dot(xb_ref[pl.ds(h * ts, ts)], wu_ref[...])
        if h == 0:
            @pl.when(seq_start)
            def _():
                ext[0, 0:SUBLANES] = jnp.zeros((SUBLANES, 2 * tf), F32)

            @pl.when(jnp.logical_not(seq_start))
            def _():
                ext[0, 0:SUBLANES] = carry[c]
        else:
            ext[h, 0:SUBLANES] = prev_tail
        ext[h, SUBLANES:SUBLANES + ts] = up
        tail = up[ts - SUBLANES:ts]
        if h == ns - 1:
            carry[c] = tail
            tail_ref[...] = tail
        return tail

    def gate(h):
        cw = cw_ref[...]
        hc = ext[h, SUBLANES - 2:SUBLANES - 2 + ts] * cw[0:1]
        hc = hc + ext[h, SUBLANES - 1:SUBLANES - 1 + ts] * cw[1:2]
        hc = hc + ext[h, SUBLANES:SUBLANES + ts] * cw[2:3]
        return (jax.nn.silu(hc[:, :tf]) * hc[:, tf:]).astype(BF16)

    def down_proj(h, act):
        acc_ref[pl.ds(h * ts, ts)] += _dot(act, wd_ref[...])

    tail = up_proj(0, None)
    for h in range(ns):
        act = gate(h)
        if h + 1 < ns:
            tail = up_proj(h + 1, tail)
        down_proj(h, act)

    @pl.when(c == nc - 1)
    def _():
        o_ref[...] = _layer_norm(alpha * x_ref[...] + acc_ref[...], g_ref[...], b_ref[...])


def _interleave_chunks(t, tf):
    lead = t.shape[:-1]
    return jnp.swapaxes(t.reshape(*lead, 2, D_FF // tf, tf), -3, -2).reshape(*lead, 2 * D_FF)


def _deinterleave_chunks(t, tf):
    lead = t.shape[:-1]
    return jnp.swapaxes(t.reshape(*lead, D_FF // tf, 2, tf), -3, -2).reshape(*lead, 2 * D_FF)


def _ffn_prompt(x, w_up_il, w_down, cw_il, g, b, layer, alpha, seq, tm, tf, ns):
    m = x.shape[0]
    nc = D_FF // tf
    tiles_per_seq = seq // tm
    out, tail = pl.pallas_call(
        functools.partial(_ffn_body, tm, tf, ns, tiles_per_seq, nc, alpha),
        grid=(m // tm, nc),
        in_specs=[pl.BlockSpec((tm, D_MODEL), lambda i, c: (i, 0)),
                  pl.BlockSpec((None, D_MODEL, 2 * tf), lambda i, c: (layer, 0, c)),
                  pl.BlockSpec((None, tf, D_MODEL), lambda i, c: (layer, c, 0)),
                  pl.BlockSpec((None, CONV_K, 2 * tf), lambda i, c: (layer, 0, c)),
                  pl.BlockSpec((None, 1, D_MODEL), lambda i, c: (layer, 0, 0)),
                  pl.BlockSpec((None, 1, D_MODEL), lambda i, c: (layer, 0, 0))],
        out_specs=[pl.BlockSpec((tm, D_MODEL), lambda i, c: (i, 0)),
                   pl.BlockSpec((None, SUBLANES, 2 * tf), lambda i, c: (i, 0, c))],
        out_shape=[jax.ShapeDtypeStruct((m, D_MODEL), F32),
                   jax.ShapeDtypeStruct((m // tm, SUBLANES, 2 * D_FF), F32)],
        scratch_shapes=[pltpu.VMEM((tm, D_MODEL), BF16),
                        pltpu.VMEM((tm, D_MODEL), F32),
                        pltpu.VMEM((ns, SUBLANES + tm // ns, 2 * tf), F32),
                        pltpu.VMEM((nc, SUBLANES, 2 * tf), F32)],
        compiler_params=_params("arbitrary", "arbitrary"),
        name="ffn_prompt",
    )(x, w_up_il, w_down, cw_il, g, b)
    return out, _deinterleave_chunks(tail[tiles_per_seq - 1::tiles_per_seq], tf)


def _gate_decode_body(ug_ref, uv_ref, sg_ref, sv_ref, cwg_ref, cwv_ref, o_ref):
    def conv(u_ref, s_ref, cw_ref):
        cw = cw_ref[...]
        y = s_ref[0] * cw[0:1]
        y = y + s_ref[1] * cw[1:2]
        return y + u_ref[...] * cw[2:3]

    hg = conv(ug_ref, sg_ref, cwg_ref)
    hv = conv(uv_ref, sv_ref, cwv_ref)
    o_ref[...] = (jax.nn.silu(hg) * hv).astype(BF16)


def _gate_decode(up, state, cw, layer, tf):
    b = up.shape[0]
    nc = D_FF // tf
    gate = lambda c: (0, 2 * c)
    value = lambda c: (0, 2 * c + 1)
    return pl.pallas_call(
        _gate_decode_body,
        grid=(nc,),
        in_specs=[pl.BlockSpec((b, tf), gate),
                  pl.BlockSpec((b, tf), value),
                  pl.BlockSpec((CONV_K - 1, b, tf), lambda c: (0, 0, 2 * c)),
                  pl.BlockSpec((CONV_K - 1, b, tf), lambda c: (0, 0, 2 * c + 1)),
                  pl.BlockSpec((None, CONV_K, tf), lambda c: (layer, 0, 2 * c)),
                  pl.BlockSpec((None, CONV_K, tf), lambda c: (layer, 0, 2 * c + 1))],
        out_specs=pl.BlockSpec((b, tf), lambda c: (0, c)),
        out_shape=jax.ShapeDtypeStruct((b, D_FF), BF16),
        compiler_params=_params("arbitrary"),
        name="gate_decode",
    )(up, up, state, state, cw, cw)


def kernel(x_prompt, x_sample, cache_k, cache_v, state_conv, state_pool, state_ffn, rel_table, w_in, conv_w,
           pool_w, pool_scale, sinks, w_o, ln1_g, ln1_b, w_up, ffn_conv_w, w_down, ln2_g, ln2_b):
    depth = w_in.shape[0]
    bp_, seq, _ = x_prompt.shape
    bs = x_sample.shape[0]
    assert x_sample.shape[1] == 1 and cache_k.shape[2] == WINDOW
    alpha = (2 * depth) ** 0.25
    nc = D_FF // FFN_TF

    w_in_b = _cast_bf16(w_in, 256)
    w_o_b = _cast_bf16(w_o, 256)
    w_up_il = _cast_bf16(w_up, D_MODEL, FFN_TF, lambda j: jnp.where(j < nc, 2 * j, 2 * (j - nc) + 1))
    cw_il = _interleave_chunks(ffn_conv_w, FFN_TF)
    w_down_b = _cast_bf16(w_down, 512)
    pool_w_b = pool_w.astype(BF16)

    bias_by_dist = rel_table.astype(F32)[_t5_bucket(jnp.arange(WINDOW))]

    ln1_g, ln1_b, ln2_g, ln2_b = (t[:, None, :] for t in (ln1_g, ln1_b, ln2_g, ln2_b))

    xp = x_prompt.reshape(bp_ * seq, D_MODEL)
    xs = x_sample.reshape(bs, D_MODEL)
    pad_rows = lambda t, rows: jnp.pad(t[:, None, :], ((0, 0), (0, rows - 1), (0, 0)))

    outs_p, outs_s = [], []
    for l in range(depth):
        tab_p = _attn_tables(bias_by_dist, sinks[l], WINDOW)
        tab_s = _attn_tables(bias_by_dist, sinks[l], SUBLANES)
        ps = pool_scale[l][None, :]

        z = _matmul(xp, w_in_b, l, 1024, 512).reshape(bp_, seq, IN_W)
        cat, u_tail = _mixers_prompt(z, tab_p, conv_w[l], pool_w_b[l], ps)
        x1 = _proj_ln(cat.reshape(bp_ * seq, D_MODEL), w_o_b, l, xp, ln1_g, ln1_b, alpha, 512, D_MODEL)
        xp, up_tail = _ffn_prompt(x1, w_up_il, w_down_b, cw_il, ln2_g, ln2_b, l, alpha, seq, FFN_TM, FFN_TF, FFN_NS)
        outs_p.append((
            z[:, seq - WINDOW:, ATTN_W:ATTN_W + KV_W].reshape(bp_, WINDOW, KV_HEADS, HEAD_DIM),
            z[:, seq - WINDOW:, ATTN_W + KV_W:ATTN_W + 2 * KV_W].reshape(bp_, WINDOW, KV_HEADS, HEAD_DIM),
            u_tail[:, HALO - (CONV_K - 1):],
            z[:, seq - POOL_PREV:, IN_W - POOL_W:],
            up_tail[:, SUBLANES - (CONV_K - 1):],
        ))

        zs = _matmul(xs, w_in_b, l, bs, 512)
        ck = cache_k[l].reshape(bs, WINDOW, KV_W)
        cv = cache_v[l].reshape(bs, WINDOW, KV_W)
        uh = jnp.pad(state_conv[l], ((0, 0), (HALO - (CONV_K - 1), 0), (0, 0)))
        ph = jnp.pad(state_pool[l], ((0, 0), (HALO - POOL_PREV, 0), (0, 0)))
        k_new = zs[:, ATTN_W:ATTN_W + KV_W]
        v_new = zs[:, ATTN_W + KV_W:ATTN_W + 2 * KV_W]
        cat_s, u_new = _mixers_decode(pad_rows(zs, SUBLANES), pad_rows(k_new, WINDOW), pad_rows(v_new, WINDOW),
                                      ck, cv, uh, ph, tab_s, conv_w[l], pool_w_b[l], ps)
        x1s = _proj_ln(cat_s[:, 0], w_o_b, l, xs, ln1_g, ln1_b, alpha, bs, D_MODEL)
        up_s = _matmul(x1s, w_up_il, l, bs, 2 * FFN_TF)
        ffn_state = _interleave_chunks(jnp.swapaxes(state_ffn[l], 0, 1), FFN_TF)
        act_s = _gate_decode(up_s, ffn_state, cw_il, l, FFN_TF)
        xs = _proj_ln(act_s, w_down_b, l, x1s, ln2_g, ln2_b, alpha, bs, 512)
        outs_s.append((
            jnp.concatenate([ck[:, 1:], k_new[:, None]], 1).reshape(bs, WINDOW, KV_HEADS, HEAD_DIM),
            jnp.concatenate([cv[:, 1:], v_new[:, None]], 1).reshape(bs, WINDOW, KV_HEADS, HEAD_DIM),
            u_new[:, HALO - (CONV_K - 1):],
            jnp.concatenate([state_pool[l][:, 1:], zs[:, None, IN_W - POOL_W:]], 1),
            jnp.concatenate([state_ffn[l][:, 1:], _deinterleave_chunks(up_s, FFN_TF)[:, None, :]], 1),
        ))

    st = lambda lst, i: jnp.stack([e[i] for e in lst], 0)
    return (xp.reshape(bp_, seq, D_MODEL), xs.reshape(bs, 1, D_MODEL),
            st(outs_p, 0), st(outs_p, 1), st(outs_p, 2), st(outs_p, 3), st(outs_p, 4),
            st(outs_s, 0), st(outs_s, 1), st(outs_s, 2), st(outs_s, 3), st(outs_s, 4))
```

```python
import functools
import math

import jax
import jax.numpy as jnp
from jax import lax
from jax.experimental import pallas as pl
from jax.experimental.pallas import tpu as pltpu

D_MODEL = 2048
N_HEADS = 16
KV_HEADS = 4
HEAD_DIM = 64
GROUP = N_HEADS // KV_HEADS
ATTN_W = N_HEADS * HEAD_DIM
KV_W = KV_HEADS * HEAD_DIM
WINDOW = 128
ATTN_SCALE = HEAD_DIM ** -0.5
NEG_INF = -1e30
N_BUCKETS = 32
MAX_DISTANCE = 128
CONV_W = D_MODEL // 4
CONV_K = 3
POOL_W = D_MODEL // 4
POOL_WINDOWS = (2, 4, 8, 16)
POOL_GROUP = POOL_W // len(POOL_WINDOWS)
POOL_PREV = 15
IN_W = ATTN_W + 2 * KV_W + 3 * CONV_W + POOL_W
D_FF = 5632
PAST_LEN = 16384
LN_EPS = 1e-5

HALO = 16
SUBLANES = 8
DECODE_ROWS = 128 // GROUP
Q_BLK = 0
K_BLK = ATTN_W // KV_W
V_BLK = K_BLK + 1
GB_BLK = (ATTN_W + 2 * KV_W) // CONV_W
GC_BLK = GB_BLK + 1
H_BLK = GB_BLK + 2
PIN_BLK = GB_BLK + 3

FFN_TM = 512
FFN_TF = 512
FFN_NS = 2

VMEM_LIMIT = 56 * 1024 * 1024

BF16 = jnp.bfloat16
F32 = jnp.float32


def _params(*sem, flags=None):
    return pltpu.CompilerParams(dimension_semantics=sem, vmem_limit_bytes=VMEM_LIMIT, flags=flags)


def _dot(a, b):
    return jnp.dot(a, b, preferred_element_type=F32)


def _layer_norm(y, g, b):
    mu = jnp.mean(y, -1, keepdims=True)
    yc = y - mu
    var = jnp.mean(yc * yc, -1, keepdims=True)
    return yc * lax.rsqrt(var + LN_EPS) * g + b


def _cast_body(w_ref, o_ref):
    o_ref[...] = w_ref[...].astype(BF16)


def _cast_bf16(w, rows, cols=None, col_perm=None):
    depth, k, n = w.shape
    cols = n if cols is None else cols
    col_perm = (lambda j: j) if col_perm is None else col_perm
    return pl.pallas_call(
        _cast_body,
        grid=(depth, k // rows, n // cols),
        in_specs=[pl.BlockSpec((None, rows, cols), lambda l, i, j: (l, i, j))],
        out_specs=pl.BlockSpec((None, rows, cols), lambda l, i, j: (l, i, col_perm(j))),
        out_shape=jax.ShapeDtypeStruct(w.shape, BF16),
        compiler_params=_params("arbitrary", "arbitrary", "arbitrary"),
        name="cast_bf16",
    )(w)


def _head_block_perm(blk):
    return (blk % GROUP) * KV_HEADS + blk // GROUP


def _cast_w_in_body(w_ref, o_ref):
    w = w_ref[...]
    order = sorted(range(N_HEADS), key=_head_block_perm)
    q = jnp.concatenate([w[:, h * HEAD_DIM:(h + 1) * HEAD_DIM] for h in order], axis=1)
    o_ref[...] = jnp.concatenate([q, w[:, ATTN_W:]], axis=1).astype(BF16)


def _cast_w_in(w, rows):
    depth, k, n = w.shape
    return pl.pallas_call(
        _cast_w_in_body,
        grid=(depth, k // rows),
        in_specs=[pl.BlockSpec((None, rows, n), lambda l, i: (l, i, 0))],
        out_specs=pl.BlockSpec((None, rows, n), lambda l, i: (l, i, 0)),
        out_shape=jax.ShapeDtypeStruct(w.shape, BF16),
        compiler_params=_params("arbitrary", "arbitrary"),
        name="cast_w_in",
    )(w)


def _cast_w_o(w):
    depth, k, n = w.shape
    nh = ATTN_W // HEAD_DIM
    row_perm = lambda i: jnp.where(i < nh, _head_block_perm(i), i)
    return pl.pallas_call(
        _cast_body,
        grid=(depth, k // HEAD_DIM),
        in_specs=[pl.BlockSpec((None, HEAD_DIM, n), lambda l, i: (l, i, 0))],
        out_specs=pl.BlockSpec((None, HEAD_DIM, n), lambda l, i: (l, row_perm(i), 0)),
        out_shape=jax.ShapeDtypeStruct(w.shape, BF16),
        compiler_params=_params("arbitrary", "arbitrary"),
        name="cast_w_o",
    )(w)


def _matmul_body(x_ref, w_ref, o_ref):
    o_ref[...] = _dot(x_ref[...].astype(BF16), w_ref[...])


def _matmul(x, w, layer, tm, tn):
    m, k = x.shape
    n = w.shape[2]
    return pl.pallas_call(
        _matmul_body,
        grid=(m // tm, n // tn),
        in_specs=[pl.BlockSpec((tm, k), lambda i, j: (i, 0)),
                  pl.BlockSpec((None, k, tn), lambda i, j: (layer, 0, j))],
        out_specs=pl.BlockSpec((tm, tn), lambda i, j: (i, j)),
        out_shape=jax.ShapeDtypeStruct((m, n), F32),
        compiler_params=_params("arbitrary", "arbitrary"),
        name="matmul",
    )(x, w)


def _mixers_body(decode, tq, new_rows, pos0, *refs):
    if decode:
        (q_ref, kc_ref, vc_ref, kp_ref, vp_ref, gb_ref, gc_ref, h_ref, pin_ref, uh_ref, ph_ref,
         bias_ref, sink_ref, cw_ref, pw_ref, ps_ref,
         cat_ref, uo_ref, ext_u, ext_p) = refs
    else:
        (q_ref, kc_ref, vc_ref, kp_ref, vp_ref, gb_ref, gc_ref, h_ref, pin_ref,
         gcp_ref, hp_ref, pinp_ref,
         bias_ref, sink_ref, cw_ref, pw_ref, ps_ref,
         cat_ref, uo_ref, ext_u, ext_p) = refs
    n = pl.program_id(1)
    rows = GROUP * tq

    nk = 2 * WINDOW
    q = q_ref[...]
    qs = jnp.concatenate([q[:, g * KV_W:(g + 1) * KV_W] for g in range(GROUP)], axis=0)
    qs = (qs * ATTN_SCALE).astype(BF16)
    k_all = jnp.concatenate([kp_ref[...], kc_ref[...]], axis=0).astype(BF16)
    v_all = jnp.concatenate([vp_ref[...], vc_ref[...]], axis=0)
    lane_head = lax.broadcasted_iota(jnp.int32, (nk, KV_W), 1) // HEAD_DIM
    k_cat = jnp.concatenate([jnp.where(lane_head == kvh, k_all, jnp.zeros((), BF16)) for kvh in range(KV_HEADS)],
                            axis=0)
    st_all = lax.dot_general(k_cat, qs, (((1,), (1,)), ((), ())), preferred_element_type=F32)
    kj = lax.broadcasted_iota(jnp.int32, (nk, rows), 0)
    qi = lax.broadcasted_iota(jnp.int32, (nk, rows), 1) & (tq - 1)
    mask = jnp.logical_and(kj > qi, kj <= qi + WINDOW)
    if not decode:
        mask = jnp.logical_and(mask, jnp.logical_or(n > 0, kj >= WINDOW))
    probs = []
    for kvh in range(KV_HEADS):
        s = jnp.where(mask, st_all[kvh * nk:(kvh + 1) * nk] + bias_ref[kvh], NEG_INF)
        sink = sink_ref[kvh]
        m = jnp.maximum(jnp.max(s, 0, keepdims=True), sink)
        p = jnp.exp(s - m)
        den = jnp.sum(p, 0, keepdims=True) + jnp.exp(sink - m)
        probs.append((p * (1.0 / den)).astype(BF16))
    pt_all = jnp.concatenate(probs, axis=0)
    vt = v_all.T
    row_head = lax.broadcasted_iota(jnp.int32, (KV_W, KV_HEADS * nk), 0) // HEAD_DIM
    col_head = lax.broadcasted_iota(jnp.int32, (KV_W, KV_HEADS * nk), 1) // nk
    vt_cat = jnp.where(row_head == col_head, jnp.concatenate([vt] * KV_HEADS, axis=1), 0.0).astype(BF16)
    o = _dot(vt_cat, pt_all).T
    attn = jnp.concatenate([o[g * tq:(g + 1) * tq] for g in range(GROUP)], axis=1)

    u = gc_ref[...] * h_ref[...]
    if decode:
        ext_u[0:HALO] = uh_ref[...]
    else:
        ext_u[0:HALO] = jnp.where(n > 0, gcp_ref[...] * hp_ref[...], 0.0)
    ext_u[HALO:HALO + tq] = u
    cw = cw_ref[...]
    conv = ext_u[HALO - 2:HALO - 2 + tq] * cw[0:1]
    conv = conv + ext_u[HALO - 1:HALO - 1 + tq] * cw[1:2]
    conv = conv + u * cw[2:3]
    c = gb_ref[...] * conv
    uo_ref[...] = ext_u[new_rows:new_rows + HALO]

    pin = pin_ref[...]
    if decode:
        ext_p[0:HALO] = ph_ref[...]
    else:
        ext_p[0:HALO] = jnp.where(n > 0, pinp_ref[...], 0.0)
    ext_p[HALO:HALO + tq] = pin
    pos = pos0 + n * tq + lax.broadcasted_iota(jnp.int32, (tq, 1), 0)
    pooled = []
    for g, w in enumerate(POOL_WINDOWS):
        lo = g * POOL_GROUP
        cur = pin[:, lo:lo + POOL_GROUP]
        win = ext_p[HALO - (w - 1):HALO - (w - 1) + tq, lo:lo + POOL_GROUP]
        for k in range(w - 2, -1, -1):
            win = win + ext_p[HALO - k:HALO - k + tq, lo:lo + POOL_GROUP]
        cnt = jnp.minimum(pos + 1, w).astype(F32)
        d = (win / cnt - cur).astype(BF16)
        pooled.append(_dot(d, pw_ref[g]))
    pm = jnp.concatenate(pooled, axis=1) * ps_ref[...]

    cat_ref[...] = jnp.concatenate([attn, c, pm], axis=1).astype(BF16)


def _mixers_prompt(z, tables, cw, pw, ps):
    b, s, _ = z.shape
    tq = WINDOW
    nb = s // tq
    bias, sink = tables
    cur = lambda blk: (lambda i, n: (i, n, blk))
    prev = lambda blk: (lambda i, n: (i, jnp.maximum(n - 1, 0), blk))
    halo = lambda blk: (lambda i, n: (i, jnp.maximum(n * (tq // HALO) - 1, 0), blk))
    const3 = lambda i, n: (0, 0, 0)
    const2 = lambda i, n: (0, 0)
    in_specs = [
        pl.BlockSpec((None, tq, ATTN_W), cur(Q_BLK)),
        pl.BlockSpec((None, tq, KV_W), cur(K_BLK)),
        pl.BlockSpec((None, tq, KV_W), cur(V_BLK)),
        pl.BlockSpec((None, tq, KV_W), prev(K_BLK)),
        pl.BlockSpec((None, tq, KV_W), prev(V_BLK)),
        pl.BlockSpec((None, tq, CONV_W), cur(GB_BLK)),
        pl.BlockSpec((None, tq, CONV_W), cur(GC_BLK)),
        pl.BlockSpec((None, tq, CONV_W), cur(H_BLK)),
        pl.BlockSpec((None, tq, POOL_W), cur(PIN_BLK)),
        pl.BlockSpec((None, HALO, CONV_W), halo(GC_BLK)),
        pl.BlockSpec((None, HALO, CONV_W), halo(H_BLK)),
        pl.BlockSpec((None, HALO, POOL_W), halo(PIN_BLK)),
        pl.BlockSpec(bias.shape, const3),
        pl.BlockSpec(sink.shape, const3),
        pl.BlockSpec(cw.shape, const2),
        pl.BlockSpec(pw.shape, const3),
        pl.BlockSpec(ps.shape, const2),
    ]
    return pl.pallas_call(
        functools.partial(_mixers_body, False, tq, tq, 0),
        grid=(b, nb),
        in_specs=in_specs,
        out_specs=[pl.BlockSpec((None, tq, D_MODEL), lambda i, n: (i, n, 0)),
                   pl.BlockSpec((None, HALO, CONV_W), lambda i, n: (i, 0, 0))],
        out_shape=[jax.ShapeDtypeStruct((b, s, D_MODEL), BF16),
                   jax.ShapeDtypeStruct((b, HALO, CONV_W), F32)],
        scratch_shapes=[pltpu.VMEM((HALO + tq, CONV_W), F32), pltpu.VMEM((HALO + tq, POOL_W), F32)],
        compiler_params=_params("arbitrary", "arbitrary"),
        name="mixers_prompt",
    )(z, z, z, z, z, z, z, z, z, z, z, z, bias, sink, cw, pw, ps)


def _mixers_decode(zs, kn, vn, ck, cv, uh, ph, tables, cw, pw, ps):
    b = zs.shape[0]
    tq = DECODE_ROWS
    bias, sink = tables
    cur = lambda blk: (lambda i, n: (i, 0, blk))
    whole = lambda i, n: (i, 0, 0)
    const3 = lambda i, n: (0, 0, 0)
    const2 = lambda i, n: (0, 0)
    in_specs = [
        pl.BlockSpec((None, tq, ATTN_W), cur(Q_BLK)),
        pl.BlockSpec((None, WINDOW, KV_W), whole),
        pl.BlockSpec((None, WINDOW, KV_W), whole),
        pl.BlockSpec((None, WINDOW, KV_W), whole),
        pl.BlockSpec((None, WINDOW, KV_W), whole),
        pl.BlockSpec((None, tq, CONV_W), cur(GB_BLK)),
        pl.BlockSpec((None, tq, CONV_W), cur(GC_BLK)),
        pl.BlockSpec((None, tq, CONV_W), cur(H_BLK)),
        pl.BlockSpec((None, tq, POOL_W), cur(PIN_BLK)),
        pl.BlockSpec((None, HALO, CONV_W), whole),
        pl.BlockSpec((None, HALO, POOL_W), whole),
        pl.BlockSpec(bias.shape, const3),
        pl.BlockSpec(sink.shape, const3),
        pl.BlockSpec(cw.shape, const2),
        pl.BlockSpec(pw.shape, const3),
        pl.BlockSpec(ps.shape, const2),
    ]
    return pl.pallas_call(
        functools.partial(_mixers_body, True, tq, 1, PAST_LEN),
        grid=(b, 1),
        in_specs=in_specs,
        out_specs=[pl.BlockSpec((None, tq, D_MODEL), whole),
                   pl.BlockSpec((None, HALO, CONV_W), whole)],
        out_shape=[jax.ShapeDtypeStruct((b, tq, D_MODEL), BF16),
                   jax.ShapeDtypeStruct((b, HALO, CONV_W), F32)],
        scratch_shapes=[pltpu.VMEM((HALO + tq, CONV_W), F32), pltpu.VMEM((HALO + tq, POOL_W), F32)],
        compiler_params=_params("arbitrary", "arbitrary"),
        name="mixers_decode",
    )(zs, kn, vn, ck, cv, zs, zs, zs, zs, uh, ph, bias, sink, cw, pw, ps)


def _bias_table(bias_by_dist, tq):
    nk = 2 * WINDOW
    by_offset = jnp.zeros((N_HEADS, nk + 1), F32).at[:, 1:WINDOW + 1].set(bias_by_dist[::-1].T)
    rows = jnp.tile(by_offset, (1, tq))[:, :tq * nk].reshape(N_HEADS, tq, nk)
    return jnp.swapaxes(rows.reshape(KV_HEADS, GROUP * tq, nk), 1, 2)


def _sink_table(sinks_l, tq):
    return jnp.broadcast_to(sinks_l.astype(F32).reshape(KV_HEADS, GROUP, 1, 1),
                            (KV_HEADS, GROUP, 1, tq)).reshape(KV_HEADS, 1, GROUP * tq)


def _t5_bucket(n):
    max_exact = N_BUCKETS // 2
    nf = jnp.maximum(n, 1).astype(F32)
    large = max_exact + (jnp.log(nf / max_exact) / math.log(MAX_DISTANCE / max_exact)
                         * (N_BUCKETS - max_exact)).astype(jnp.int32)
    large = jnp.minimum(large, N_BUCKETS - 1)
    return jnp.where(n < max_exact, n, large)


def _proj_ln_body(nk, alpha, a_ref, w_ref, x_ref, g_ref, b_ref, o_ref, *scratch):
    part = _dot(a_ref[...], w_ref[...])
    if nk == 1:
        o_ref[...] = _layer_norm(alpha * x_ref[...] + part, g_ref[...], b_ref[...])
        return
    acc_ref, = scratch
    k = pl.program_id(1)

    @pl.when(k == 0)
    def _():
        acc_ref[...] = jnp.zeros_like(acc_ref)

    acc_ref[...] += part

    @pl.when(k == nk - 1)
    def _():
        o_ref[...] = _layer_norm(alpha * x_ref[...] + acc_ref[...], g_ref[...], b_ref[...])


def _proj_ln(a, w, layer, x, g, b, alpha, tm, tk):
    m, k = a.shape
    n = w.shape[2]
    nk = k // tk
    return pl.pallas_call(
        functools.partial(_proj_ln_body, nk, alpha),
        grid=(m // tm, nk),
        in_specs=[pl.BlockSpec((tm, tk), lambda i, kk: (i, kk)),
                  pl.BlockSpec((None, tk, n), lambda i, kk: (layer, kk, 0)),
                  pl.BlockSpec((tm, n), lambda i, kk: (i, 0)),
                  pl.BlockSpec((None, 1, n), lambda i, kk: (layer, 0, 0)),
                  pl.BlockSpec((None, 1, n), lambda i, kk: (layer, 0, 0))],
        out_specs=pl.BlockSpec((tm, n), lambda i, kk: (i, 0)),
        out_shape=jax.ShapeDtypeStruct((m, n), F32),
        scratch_shapes=[] if nk == 1 else [pltpu.VMEM((tm, n), F32)],
        compiler_params=_params("arbitrary", "arbitrary"),
        name="proj_ln",
    )(a, w, x, g, b)


def _ffn_body(tm, tf, ns, tiles_per_seq, nc, alpha, x_ref, wu_ref, wd_ref, cw_ref, g_ref, b_ref,
              o_ref, tail_ref, xb_ref, acc_ref, ext, carry):
    i = pl.program_id(0)
    c = pl.program_id(1)
    ts = tm // ns

    @pl.when(c == 0)
    def _():
        xb_ref[...] = x_ref[...].astype(BF16)
        acc_ref[...] = jnp.zeros_like(acc_ref)

    seq_start = (i % tiles_per_seq) == 0

    def up_proj(h, prev_tail):
        up = _dot(xb_ref[pl.ds(h * ts, ts)], wu_ref[...])
        if h == 0:
            @pl.when(seq_start)
            def _():
                ext[0, 0:SUBLANES] = jnp.zeros((SUBLANES, 2 * tf), F32)

            @pl.when(jnp.logical_not(seq_start))
            def _():
                ext[0, 0:SUBLANES] = carry[c]
        else:
            ext[h, 0:SUBLANES] = prev_tail
        ext[h, SUBLANES:SUBLANES + ts] = up
        tail = up[ts - SUBLANES:ts]
        if h == ns - 1:
            carry[c] = tail
            tail_ref[...] = tail
        return tail

    def gate(h):
        cw = cw_ref[...]
        hc = ext[h, SUBLANES - 2:SUBLANES - 2 + ts] * cw[0:1]
        hc = hc + ext[h, SUBLANES - 1:SUBLANES - 1 + ts] * cw[1:2]
        hc = hc + ext[h, SUBLANES:SUBLANES + ts] * cw[2:3]
        return (jax.nn.silu(hc[:, :tf]) * hc[:, tf:]).astype(BF16)

    def down_proj(h, act):
        acc_ref[pl.ds(h * ts, ts)] += _dot(act, wd_ref[...])

    tail = up_proj(0, None)
    for h in range(ns):
        act = gate(h)
        if h + 1 < ns:
            tail = up_proj(h + 1, tail)
        down_proj(h, act)

    @pl.when(c == nc - 1)
    def _():
        o_ref[...] = _layer_norm(alpha * x_ref[...] + acc_ref[...], g_ref[...], b_ref[...])


def _interleave_chunks(t, tf):
    lead = t.shape[:-1]
    return jnp.swapaxes(t.reshape(*lead, 2, D_FF // tf, tf), -3, -2).reshape(*lead, 2 * D_FF)


def _deinterleave_chunks(t, tf):
    lead = t.shape[:-1]
    return jnp.swapaxes(t.reshape(*lead, D_FF // tf, 2, tf), -3, -2).reshape(*lead, 2 * D_FF)


def _ffn_prompt(x, w_up_il, w_down, cw_il, g, b, layer, alpha, seq, tm, tf, ns):
    m = x.shape[0]
    nc = D_FF // tf
    tiles_per_seq = seq // tm
    out, tail = pl.pallas_call(
        functools.partial(_ffn_body, tm, tf, ns, tiles_per_seq, nc, alpha),
        grid=(m // tm, nc),
        in_specs=[pl.BlockSpec((tm, D_MODEL), lambda i, c: (i, 0)),
                  pl.BlockSpec((None, D_MODEL, 2 * tf), lambda i, c: (layer, 0, c)),
                  pl.BlockSpec((None, tf, D_MODEL), lambda i, c: (layer, c, 0)),
                  pl.BlockSpec((None, CONV_K, 2 * tf), lambda i, c: (layer, 0, c)),
                  pl.BlockSpec((None, 1, D_MODEL), lambda i, c: (layer, 0, 0)),
                  pl.BlockSpec((None, 1, D_MODEL), lambda i, c: (layer, 0, 0))],
        out_specs=[pl.BlockSpec((tm, D_MODEL), lambda i, c: (i, 0)),
                   pl.BlockSpec((None, SUBLANES, 2 * tf), lambda i, c: (i, 0, c))],
        out_shape=[jax.ShapeDtypeStruct((m, D_MODEL), F32),
                   jax.ShapeDtypeStruct((m // tm, SUBLANES, 2 * D_FF), F32)],
        scratch_shapes=[pltpu.VMEM((tm, D_MODEL), BF16),
                        pltpu.VMEM((tm, D_MODEL), F32),
                        pltpu.VMEM((ns, SUBLANES + tm // ns, 2 * tf), F32),
                        pltpu.VMEM((nc, SUBLANES, 2 * tf), F32)],
        compiler_params=_params("arbitrary", "arbitrary"),
        name="ffn_prompt",
    )(x, w_up_il, w_down, cw_il, g, b)
    return out, _deinterleave_chunks(tail[tiles_per_seq - 1::tiles_per_seq], tf)


def _gate_decode_body(ug_ref, uv_ref, sg_ref, sv_ref, cwg_ref, cwv_ref, o_ref):
    def conv(u_ref, s_ref, cw_ref):
        cw = cw_ref[...]
        y = s_ref[0] * cw[0:1]
        y = y + s_ref[1] * cw[1:2]
        return y + u_ref[...] * cw[2:3]

    hg = conv(ug_ref, sg_ref, cwg_ref)
    hv = conv(uv_ref, sv_ref, cwv_ref)
    o_ref[...] = (jax.nn.silu(hg) * hv).astype(BF16)


def _gate_decode(up, state, cw, layer, tf):
    b = up.shape[0]
    nc = D_FF // tf
    gate = lambda c: (0, 2 * c)
    value = lambda c: (0, 2 * c + 1)
    return pl.pallas_call(
        _gate_decode_body,
        grid=(nc,),
        in_specs=[pl.BlockSpec((b, tf), gate),
                  pl.BlockSpec((b, tf), value),
                  pl.BlockSpec((CONV_K - 1, b, tf), lambda c: (0, 0, 2 * c)),
                  pl.BlockSpec((CONV_K - 1, b, tf), lambda c: (0, 0, 2 * c + 1)),
                  pl.BlockSpec((None, CONV_K, tf), lambda c: (layer, 0, 2 * c)),
                  pl.BlockSpec((None, CONV_K, tf), lambda c: (layer, 0, 2 * c + 1))],
        out_specs=pl.BlockSpec((b, tf), lambda c: (0, c)),
        out_shape=jax.ShapeDtypeStruct((b, D_FF), BF16),
        compiler_params=_params("arbitrary"),
        name="gate_decode",
    )(up, up, state, state, cw, cw)


def kernel(x_prompt, x_sample, cache_k, cache_v, state_conv, state_pool, state_ffn, rel_table, w_in, conv_w,
           pool_w, pool_scale, sinks, w_o, ln1_g, ln1_b, w_up, ffn_conv_w, w_down, ln2_g, ln2_b):
    depth = w_in.shape[0]
    bp_, seq, _ = x_prompt.shape
    bs = x_sample.shape[0]
    assert x_sample.shape[1] == 1 and cache_k.shape[2] == WINDOW
    alpha = (2 * depth) ** 0.25
    nc = D_FF // FFN_TF

    w_in_b = _cast_w_in(w_in, 256)
    w_o_b = _cast_w_o(w_o)
    w_up_il = _cast_bf16(w_up, D_MODEL, FFN_TF, lambda j: jnp.where(j < nc, 2 * j, 2 * (j - nc) + 1))
    cw_il = _interleave_chunks(ffn_conv_w, FFN_TF)
    w_down_b = _cast_bf16(w_down, 512)
    pool_w_b = pool_w.astype(BF16)

    bias_by_dist = rel_table.astype(F32)[_t5_bucket(jnp.arange(WINDOW))]
    bias_p = _bias_table(bias_by_dist, WINDOW)
    bias_s = _bias_table(bias_by_dist, DECODE_ROWS)

    ln1_g, ln1_b, ln2_g, ln2_b = (t[:, None, :] for t in (ln1_g, ln1_b, ln2_g, ln2_b))

    xp = x_prompt.reshape(bp_ * seq, D_MODEL)
    xs = x_sample.reshape(bs, D_MODEL)
    pad_rows = lambda t, rows: jnp.pad(t[:, None, :], ((0, 0), (0, rows - 1), (0, 0)))

    outs_p, outs_s = [], []
    for l in range(depth):
        tab_p = (bias_p, _sink_table(sinks[l], WINDOW))
        tab_s = (bias_s, _sink_table(sinks[l], DECODE_ROWS))
        ps = pool_scale[l][None, :]

        z = _matmul(xp, w_in_b, l, 1024, 512).reshape(bp_, seq, IN_W)
        cat, u_tail = _mixers_prompt(z, tab_p, conv_w[l], pool_w_b[l], ps)
        x1 = _proj_ln(cat.reshape(bp_ * seq, D_MODEL), w_o_b, l, xp, ln1_g, ln1_b, alpha, 512, D_MODEL)
        xp, up_tail = _ffn_prompt(x1, w_up_il, w_down_b, cw_il, ln2_g, ln2_b, l, alpha, seq, FFN_TM, FFN_TF, FFN_NS)
        outs_p.append((
            z[:, seq - WINDOW:, ATTN_W:ATTN_W + KV_W].reshape(bp_, WINDOW, KV_HEADS, HEAD_DIM),
            z[:, seq - WINDOW:, ATTN_W + KV_W:ATTN_W + 2 * KV_W].reshape(bp_, WINDOW, KV_HEADS, HEAD_DIM),
            u_tail[:, HALO - (CONV_K - 1):],
            z[:, seq - POOL_PREV:, IN_W - POOL_W:],
            up_tail[:, SUBLANES - (CONV_K - 1):],
        ))

        zs = _matmul(xs, w_in_b, l, bs, 512)
        ck = cache_k[l].reshape(bs, WINDOW, KV_W)
        cv = cache_v[l].reshape(bs, WINDOW, KV_W)
        uh = jnp.pad(state_conv[l], ((0, 0), (HALO - (CONV_K - 1), 0), (0, 0)))
        ph = jnp.pad(state_pool[l], ((0, 0), (HALO - POOL_PREV, 0), (0, 0)))
        k_new = zs[:, ATTN_W:ATTN_W + KV_W]
        v_new = zs[:, ATTN_W + KV_W:ATTN_W + 2 * KV_W]
        cat_s, u_new = _mixers_decode(pad_rows(zs, DECODE_ROWS), pad_rows(k_new, WINDOW), pad_rows(v_new, WINDOW),
                                      ck, cv, uh, ph, tab_s, conv_w[l], pool_w_b[l], ps)
        x1s = _proj_ln(cat_s[:, 0], w_o_b, l, xs, ln1_g, ln1_b, alpha, bs, D_MODEL)
        up_s = _matmul(x1s, w_up_il, l, bs, 2 * FFN_TF)
        ffn_state = _interleave_chunks(jnp.swapaxes(state_ffn[l], 0, 1), FFN_TF)
        act_s = _gate_decode(up_s, ffn_state, cw_il, l, FFN_TF)
        xs = _proj_ln(act_s, w_down_b, l, x1s, ln2_g, ln2_b, alpha, bs, 512)
        outs_s.append((
            jnp.concatenate([ck[:, 1:], k_new[:, None]], 1).reshape(bs, WINDOW, KV_HEADS, HEAD_DIM),
            jnp.concatenate([cv[:, 1:], v_new[:, None]], 1).reshape(bs, WINDOW, KV_HEADS, HEAD_DIM),
            u_new[:, HALO - (CONV_K - 1):],
            jnp.concatenate([state_pool[l][:, 1:], zs[:, None, IN_W - POOL_W:]], 1),
            jnp.concatenate([state_ffn[l][:, 1:], _deinterleave_chunks(up_s, FFN_TF)[:, None, :]], 1),
        ))

    st = lambda lst, i: jnp.stack([e[i] for e in lst], 0)
    return (xp.reshape(bp_, seq, D_MODEL), xs.reshape(bs, 1, D_MODEL),
            st(outs_p, 0), st(outs_p, 1), st(outs_p, 2), st(outs_p, 3), st(outs_p, 4),
            st(outs_s, 0), st(outs_s, 1), st(outs_s, 2), st(outs_s, 3), st(outs_s, 4))
```

```python
import functools
import math

import jax
import jax.numpy as jnp
from jax import lax
from jax.experimental import pallas as pl
from jax.experimental.pallas import tpu as pltpu

D_MODEL = 2048
N_HEADS = 16
KV_HEADS = 4
HEAD_DIM = 64
GROUP = N_HEADS // KV_HEADS
ATTN_W = N_HEADS * HEAD_DIM
KV_W = KV_HEADS * HEAD_DIM
WINDOW = 128
ATTN_SCALE = HEAD_DIM ** -0.5
NEG_INF = -1e30
N_BUCKETS = 32
MAX_DISTANCE = 128
CONV_W = D_MODEL // 4
CONV_K = 3
POOL_W = D_MODEL // 4
POOL_WINDOWS = (2, 4, 8, 16)
POOL_GROUP = POOL_W // len(POOL_WINDOWS)
POOL_PREV = 15
IN_W = ATTN_W + 2 * KV_W + 3 * CONV_W + POOL_W
D_FF = 5632
PAST_LEN = 16384
LN_EPS = 1e-5

HALO = 16
SUBLANES = 8
DECODE_ROWS = 128 // GROUP
Q_BLK = 0
K_BLK = ATTN_W // KV_W
V_BLK = K_BLK + 1
GB_BLK = (ATTN_W + 2 * KV_W) // CONV_W
GC_BLK = GB_BLK + 1
H_BLK = GB_BLK + 2
PIN_BLK = GB_BLK + 3

FFN_TM = 1024
FFN_TF = 512
FFN_NS = 2

VMEM_LIMIT = 60 * 1024 * 1024

BF16 = jnp.bfloat16
F32 = jnp.float32


def _params(*sem, flags=None):
    return pltpu.CompilerParams(dimension_semantics=sem, vmem_limit_bytes=VMEM_LIMIT, flags=flags)


def _dot(a, b):
    return jnp.dot(a, b, preferred_element_type=F32)


def _layer_norm(y, g, b):
    mu = jnp.mean(y, -1, keepdims=True)
    yc = y - mu
    var = jnp.mean(yc * yc, -1, keepdims=True)
    return yc * lax.rsqrt(var + LN_EPS) * g + b


def _cast_body(w_ref, o_ref):
    o_ref[...] = w_ref[...].astype(BF16)


def _cast_bf16(w, rows, cols=None, col_perm=None):
    depth, k, n = w.shape
    cols = n if cols is None else cols
    col_perm = (lambda j: j) if col_perm is None else col_perm
    return pl.pallas_call(
        _cast_body,
        grid=(depth, k // rows, n // cols),
        in_specs=[pl.BlockSpec((None, rows, cols), lambda l, i, j: (l, i, j))],
        out_specs=pl.BlockSpec((None, rows, cols), lambda l, i, j: (l, i, col_perm(j))),
        out_shape=jax.ShapeDtypeStruct(w.shape, BF16),
        compiler_params=_params("arbitrary", "arbitrary", "arbitrary"),
        name="cast_bf16",
    )(w)


def _head_block_perm(blk):
    return (blk % GROUP) * KV_HEADS + blk // GROUP


def _cast_w_in_body(w_ref, o_ref):
    w = w_ref[...]
    order = sorted(range(N_HEADS), key=_head_block_perm)
    q = jnp.concatenate([w[:, h * HEAD_DIM:(h + 1) * HEAD_DIM] for h in order], axis=1)
    o_ref[...] = jnp.concatenate([q, w[:, ATTN_W:]], axis=1).astype(BF16)


def _cast_w_in(w, rows):
    depth, k, n = w.shape
    return pl.pallas_call(
        _cast_w_in_body,
        grid=(depth, k // rows),
        in_specs=[pl.BlockSpec((None, rows, n), lambda l, i: (l, i, 0))],
        out_specs=pl.BlockSpec((None, rows, n), lambda l, i: (l, i, 0)),
        out_shape=jax.ShapeDtypeStruct(w.shape, BF16),
        compiler_params=_params("arbitrary", "arbitrary"),
        name="cast_w_in",
    )(w)


def _cast_w_o(w):
    depth, k, n = w.shape
    nh = ATTN_W // HEAD_DIM
    row_perm = lambda i: jnp.where(i < nh, _head_block_perm(i), i)
    return pl.pallas_call(
        _cast_body,
        grid=(depth, k // HEAD_DIM),
        in_specs=[pl.BlockSpec((None, HEAD_DIM, n), lambda l, i: (l, i, 0))],
        out_specs=pl.BlockSpec((None, HEAD_DIM, n), lambda l, i: (l, row_perm(i), 0)),
        out_shape=jax.ShapeDtypeStruct(w.shape, BF16),
        compiler_params=_params("arbitrary", "arbitrary"),
        name="cast_w_o",
    )(w)


def _matmul_body(x_ref, w_ref, o_ref, xb_ref):
    @pl.when(pl.program_id(1) == 0)
    def _():
        xb_ref[...] = x_ref[...].astype(BF16)

    o_ref[...] = _dot(xb_ref[...], w_ref[...])


def _matmul(x, w, layer, tm, tn):
    m, k = x.shape
    n = w.shape[2]
    return pl.pallas_call(
        _matmul_body,
        grid=(m // tm, n // tn),
        in_specs=[pl.BlockSpec((tm, k), lambda i, j: (i, 0)),
                  pl.BlockSpec((None, k, tn), lambda i, j: (layer, 0, j))],
        out_specs=pl.BlockSpec((tm, tn), lambda i, j: (i, j)),
        out_shape=jax.ShapeDtypeStruct((m, n), F32),
        scratch_shapes=[pltpu.VMEM((tm, k), BF16)],
        compiler_params=_params("arbitrary", "arbitrary"),
        name="matmul",
    )(x, w)


def _mixers_body(decode, tq, new_rows, pos0, *refs):
    if decode:
        (q_ref, kc_ref, vc_ref, kp_ref, vp_ref, gb_ref, gc_ref, h_ref, pin_ref, uh_ref, ph_ref,
         bias_ref, sink_ref, cw_ref, pw_ref, ps_ref,
         cat_ref, uo_ref, ext_u, ext_p) = refs
    else:
        (q_ref, kc_ref, vc_ref, kp_ref, vp_ref, gb_ref, gc_ref, h_ref, pin_ref,
         gcp_ref, hp_ref, pinp_ref,
         bias_ref, sink_ref, cw_ref, pw_ref, ps_ref,
         cat_ref, uo_ref, ext_u, ext_p) = refs
    n = pl.program_id(1)
    rows = GROUP * tq

    nk = 2 * WINDOW
    q = q_ref[...]
    qs = jnp.concatenate([q[:, g * KV_W:(g + 1) * KV_W] for g in range(GROUP)], axis=0)
    qs = (qs * ATTN_SCALE).astype(BF16)
    k_all = jnp.concatenate([kp_ref[...], kc_ref[...]], axis=0).astype(BF16)
    v_all = jnp.concatenate([vp_ref[...], vc_ref[...]], axis=0)
    lane_head = lax.broadcasted_iota(jnp.int32, (nk, KV_W), 1) // HEAD_DIM
    k_cat = jnp.concatenate([jnp.where(lane_head == kvh, k_all, jnp.zeros((), BF16)) for kvh in range(KV_HEADS)],
                            axis=0)
    st_all = lax.dot_general(k_cat, qs, (((1,), (1,)), ((), ())), preferred_element_type=F32)
    kj = lax.broadcasted_iota(jnp.int32, (nk, rows), 0)
    qi = lax.broadcasted_iota(jnp.int32, (nk, rows), 1) & (tq - 1)
    mask = jnp.logical_and(kj > qi, kj <= qi + WINDOW)
    if not decode:
        mask = jnp.logical_and(mask, jnp.logical_or(n > 0, kj >= WINDOW))
    probs = []
    for kvh in range(KV_HEADS):
        s = jnp.where(mask, st_all[kvh * nk:(kvh + 1) * nk] + bias_ref[kvh], NEG_INF)
        sink = sink_ref[kvh]
        m = jnp.maximum(jnp.max(s, 0, keepdims=True), sink)
        p = jnp.exp(s - m)
        den = jnp.sum(p, 0, keepdims=True) + jnp.exp(sink - m)
        probs.append((p * (1.0 / den)).astype(BF16))
    pt_all = jnp.concatenate(probs, axis=0)
    vt = v_all.T
    row_head = lax.broadcasted_iota(jnp.int32, (KV_W, KV_HEADS * nk), 0) // HEAD_DIM
    col_head = lax.broadcasted_iota(jnp.int32, (KV_W, KV_HEADS * nk), 1) // nk
    vt_cat = jnp.where(row_head == col_head, jnp.concatenate([vt] * KV_HEADS, axis=1), 0.0).astype(BF16)
    o = _dot(vt_cat, pt_all).T
    attn = jnp.concatenate([o[g * tq:(g + 1) * tq] for g in range(GROUP)], axis=1)

    u = gc_ref[...] * h_ref[...]
    if decode:
        ext_u[0:HALO] = uh_ref[...]
    else:
        ext_u[0:HALO] = jnp.where(n > 0, gcp_ref[...] * hp_ref[...], 0.0)
    ext_u[HALO:HALO + tq] = u
    cw = cw_ref[...]
    conv = ext_u[HALO - 2:HALO - 2 + tq] * cw[0:1]
    conv = conv + ext_u[HALO - 1:HALO - 1 + tq] * cw[1:2]
    conv = conv + u * cw[2:3]
    c = gb_ref[...] * conv
    uo_ref[...] = ext_u[new_rows:new_rows + HALO]

    pin = pin_ref[...]
    if decode:
        ext_p[0:HALO] = ph_ref[...]
    else:
        ext_p[0:HALO] = jnp.where(n > 0, pinp_ref[...], 0.0)
    ext_p[HALO:HALO + tq] = pin
    pos = pos0 + n * tq + lax.broadcasted_iota(jnp.int32, (tq, 1), 0)
    pooled = []
    for g, w in enumerate(POOL_WINDOWS):
        lo = g * POOL_GROUP
        cur = pin[:, lo:lo + POOL_GROUP]
        win = ext_p[HALO - (w - 1):HALO - (w - 1) + tq, lo:lo + POOL_GROUP]
        for k in range(w - 2, -1, -1):
            win = win + ext_p[HALO - k:HALO - k + tq, lo:lo + POOL_GROUP]
        cnt = jnp.minimum(pos + 1, w).astype(F32)
        d = (win / cnt - cur).astype(BF16)
        pooled.append(_dot(d, pw_ref[g]))
    pm = jnp.concatenate(pooled, axis=1) * ps_ref[...]

    cat_ref[...] = jnp.concatenate([attn, c, pm], axis=1).astype(BF16)


def _mixers_prompt(z, tables, cw, pw, ps):
    b, s, _ = z.shape
    tq = WINDOW
    nb = s // tq
    bias, sink = tables
    cur = lambda blk: (lambda i, n: (i, n, blk))
    prev = lambda blk: (lambda i, n: (i, jnp.maximum(n - 1, 0), blk))
    halo = lambda blk: (lambda i, n: (i, jnp.maximum(n * (tq // HALO) - 1, 0), blk))
    const3 = lambda i, n: (0, 0, 0)
    const2 = lambda i, n: (0, 0)
    in_specs = [
        pl.BlockSpec((None, tq, ATTN_W), cur(Q_BLK)),
        pl.BlockSpec((None, tq, KV_W), cur(K_BLK)),
        pl.BlockSpec((None, tq, KV_W), cur(V_BLK)),
        pl.BlockSpec((None, tq, KV_W), prev(K_BLK)),
        pl.BlockSpec((None, tq, KV_W), prev(V_BLK)),
        pl.BlockSpec((None, tq, CONV_W), cur(GB_BLK)),
        pl.BlockSpec((None, tq, CONV_W), cur(GC_BLK)),
        pl.BlockSpec((None, tq, CONV_W), cur(H_BLK)),
        pl.BlockSpec((None, tq, POOL_W), cur(PIN_BLK)),
        pl.BlockSpec((None, HALO, CONV_W), halo(GC_BLK)),
        pl.BlockSpec((None, HALO, CONV_W), halo(H_BLK)),
        pl.BlockSpec((None, HALO, POOL_W), halo(PIN_BLK)),
        pl.BlockSpec(bias.shape, const3),
        pl.BlockSpec(sink.shape, const3),
        pl.BlockSpec(cw.shape, const2),
        pl.BlockSpec(pw.shape, const3),
        pl.BlockSpec(ps.shape, const2),
    ]
    return pl.pallas_call(
        functools.partial(_mixers_body, False, tq, tq, 0),
        grid=(b, nb),
        in_specs=in_specs,
        out_specs=[pl.BlockSpec((None, tq, D_MODEL), lambda i, n: (i, n, 0)),
                   pl.BlockSpec((None, HALO, CONV_W), lambda i, n: (i, 0, 0))],
        out_shape=[jax.ShapeDtypeStruct((b, s, D_MODEL), BF16),
                   jax.ShapeDtypeStruct((b, HALO, CONV_W), F32)],
        scratch_shapes=[pltpu.VMEM((HALO + tq, CONV_W), F32), pltpu.VMEM((HALO + tq, POOL_W), F32)],
        compiler_params=_params("arbitrary", "arbitrary"),
        name="mixers_prompt",
    )(z, z, z, z, z, z, z, z, z, z, z, z, bias, sink, cw, pw, ps)


def _mixers_decode(zs, kn, vn, ck, cv, uh, ph, tables, cw, pw, ps):
    b = zs.shape[0]
    tq = DECODE_ROWS
    bias, sink = tables
    cur = lambda blk: (lambda i, n: (i, 0, blk))
    whole = lambda i, n: (i, 0, 0)
    const3 = lambda i, n: (0, 0, 0)
    const2 = lambda i, n: (0, 0)
    in_specs = [
        pl.BlockSpec((None, tq, ATTN_W), cur(Q_BLK)),
        pl.BlockSpec((None, WINDOW, KV_W), whole),
        pl.BlockSpec((None, WINDOW, KV_W), whole),
        pl.BlockSpec((None, WINDOW, KV_W), whole),
        pl.BlockSpec((None, WINDOW, KV_W), whole),
        pl.BlockSpec((None, tq, CONV_W), cur(GB_BLK)),
        pl.BlockSpec((None, tq, CONV_W), cur(GC_BLK)),
        pl.BlockSpec((None, tq, CONV_W), cur(H_BLK)),
        pl.BlockSpec((None, tq, POOL_W), cur(PIN_BLK)),
        pl.BlockSpec((None, HALO, CONV_W), whole),
        pl.BlockSpec((None, HALO, POOL_W), whole),
        pl.BlockSpec(bias.shape, const3),
        pl.BlockSpec(sink.shape, const3),
        pl.BlockSpec(cw.shape, const2),
        pl.BlockSpec(pw.shape, const3),
        pl.BlockSpec(ps.shape, const2),
    ]
    return pl.pallas_call(
        functools.partial(_mixers_body, True, tq, 1, PAST_LEN),
        grid=(b, 1),
        in_specs=in_specs,
        out_specs=[pl.BlockSpec((None, tq, D_MODEL), whole),
                   pl.BlockSpec((None, HALO, CONV_W), whole)],
        out_shape=[jax.ShapeDtypeStruct((b, tq, D_MODEL), BF16),
                   jax.ShapeDtypeStruct((b, HALO, CONV_W), F32)],
        scratch_shapes=[pltpu.VMEM((HALO + tq, CONV_W), F32), pltpu.VMEM((HALO + tq, POOL_W), F32)],
        compiler_params=_params("arbitrary", "arbitrary"),
        name="mixers_decode",
    )(zs, kn, vn, ck, cv, zs, zs, zs, zs, uh, ph, bias, sink, cw, pw, ps)


def _bias_table(bias_by_dist, tq):
    nk = 2 * WINDOW
    by_offset = jnp.zeros((N_HEADS, nk + 1), F32).at[:, 1:WINDOW + 1].set(bias_by_dist[::-1].T)
    rows = jnp.tile(by_offset, (1, tq))[:, :tq * nk].reshape(N_HEADS, tq, nk)
    return jnp.swapaxes(rows.reshape(KV_HEADS, GROUP * tq, nk), 1, 2)


def _sink_table(sinks_l, tq):
    return jnp.broadcast_to(sinks_l.astype(F32).reshape(KV_HEADS, GROUP, 1, 1),
                            (KV_HEADS, GROUP, 1, tq)).reshape(KV_HEADS, 1, GROUP * tq)


def _t5_bucket(n):
    max_exact = N_BUCKETS // 2
    nf = jnp.maximum(n, 1).astype(F32)
    large = max_exact + (jnp.log(nf / max_exact) / math.log(MAX_DISTANCE / max_exact)
                         * (N_BUCKETS - max_exact)).astype(jnp.int32)
    large = jnp.minimum(large, N_BUCKETS - 1)
    return jnp.where(n < max_exact, n, large)


def _proj_ln_body(nk, alpha, a_ref, w_ref, x_ref, g_ref, b_ref, o_ref, *scratch):
    part = _dot(a_ref[...], w_ref[...])
    if nk == 1:
        o_ref[...] = _layer_norm(alpha * x_ref[...] + part, g_ref[...], b_ref[...])
        return
    acc_ref, = scratch
    k = pl.program_id(1)

    @pl.when(k == 0)
    def _():
        acc_ref[...] = jnp.zeros_like(acc_ref)

    acc_ref[...] += part

    @pl.when(k == nk - 1)
    def _():
        o_ref[...] = _layer_norm(alpha * x_ref[...] + acc_ref[...], g_ref[...], b_ref[...])


def _proj_ln(a, w, layer, x, g, b, alpha, tm, tk):
    m, k = a.shape
    n = w.shape[2]
    nk = k // tk
    return pl.pallas_call(
        functools.partial(_proj_ln_body, nk, alpha),
        grid=(m // tm, nk),
        in_specs=[pl.BlockSpec((tm, tk), lambda i, kk: (i, kk)),
                  pl.BlockSpec((None, tk, n), lambda i, kk: (layer, kk, 0)),
                  pl.BlockSpec((tm, n), lambda i, kk: (i, 0)),
                  pl.BlockSpec((None, 1, n), lambda i, kk: (layer, 0, 0)),
                  pl.BlockSpec((None, 1, n), lambda i, kk: (layer, 0, 0))],
        out_specs=pl.BlockSpec((tm, n), lambda i, kk: (i, 0)),
        out_shape=jax.ShapeDtypeStruct((m, n), F32),
        scratch_shapes=[] if nk == 1 else [pltpu.VMEM((tm, n), F32)],
        compiler_params=_params("arbitrary", "arbitrary"),
        name="proj_ln",
    )(a, w, x, g, b)


def _ffn_body(tm, tf, ns, tiles_per_seq, nc, alpha, x_ref, wu_ref, wd_ref, cw_ref, g_ref, b_ref,
              o_ref, tail_ref, xb_ref, ext, carry):
    i = pl.program_id(0)
    c = pl.program_id(1)
    ts = tm // ns

    @pl.when(c == 0)
    def _():
        xb_ref[...] = x_ref[...].astype(BF16)
        o_ref[...] = jnp.zeros_like(o_ref)

    seq_start = (i % tiles_per_seq) == 0

    def up_proj(h, prev_tail):
        up = _dot(xb_ref[pl.ds(h * ts, ts)], wu_ref[...])
        if h == 0:
            @pl.when(seq_start)
            def _():
                ext[0, 0:SUBLANES] = jnp.zeros((SUBLANES, 2 * tf), F32)

            @pl.when(jnp.logical_not(seq_start))
            def _():
                ext[0, 0:SUBLANES] = carry[c]
        else:
            ext[h, 0:SUBLANES] = prev_tail
        ext[h, SUBLANES:SUBLANES + ts] = up
        tail = up[ts - SUBLANES:ts]
        if h == ns - 1:
            carry[c] = tail
            tail_ref[...] = tail
        return tail

    def gate(h):
        cw = cw_ref[...]
        hc = ext[h, SUBLANES - 2:SUBLANES - 2 + ts] * cw[0:1]
        hc = hc + ext[h, SUBLANES - 1:SUBLANES - 1 + ts] * cw[1:2]
        hc = hc + ext[h, SUBLANES:SUBLANES + ts] * cw[2:3]
        return (jax.nn.silu(hc[:, :tf]) * hc[:, tf:]).astype(BF16)

    def down_proj(h, act):
        o_ref[pl.ds(h * ts, ts)] += _dot(act, wd_ref[...])

    tail = up_proj(0, None)
    for h in range(ns):
        act = gate(h)
        if h + 1 < ns:
            tail = up_proj(h + 1, tail)
        down_proj(h, act)

    @pl.when(c == nc - 1)
    def _():
        o_ref[...] = _layer_norm(alpha * x_ref[...] + o_ref[...], g_ref[...], b_ref[...])


def _interleave_chunks(t, tf):
    lead = t.shape[:-1]
    return jnp.swapaxes(t.reshape(*lead, 2, D_FF // tf, tf), -3, -2).reshape(*lead, 2 * D_FF)


def _deinterleave_chunks(t, tf):
    lead = t.shape[:-1]
    return jnp.swapaxes(t.reshape(*lead, D_FF // tf, 2, tf), -3, -2).reshape(*lead, 2 * D_FF)


def _ffn_prompt(x, w_up_il, w_down, cw_il, g, b, layer, alpha, seq, tm, tf, ns):
    m = x.shape[0]
    nc = D_FF // tf
    tiles_per_seq = seq // tm
    out, tail = pl.pallas_call(
        functools.partial(_ffn_body, tm, tf, ns, tiles_per_seq, nc, alpha),
        grid=(m // tm, nc),
        in_specs=[pl.BlockSpec((tm, D_MODEL), lambda i, c: (i, 0), pipeline_mode=pl.Buffered(1)),
                  pl.BlockSpec((None, D_MODEL, 2 * tf), lambda i, c: (layer, 0, c)),
                  pl.BlockSpec((None, tf, D_MODEL), lambda i, c: (layer, c, 0)),
                  pl.BlockSpec((None, CONV_K, 2 * tf), lambda i, c: (layer, 0, c)),
                  pl.BlockSpec((None, 1, D_MODEL), lambda i, c: (layer, 0, 0)),
                  pl.BlockSpec((None, 1, D_MODEL), lambda i, c: (layer, 0, 0))],
        out_specs=[pl.BlockSpec((tm, D_MODEL), lambda i, c: (i, 0)),
                   pl.BlockSpec((None, SUBLANES, 2 * tf), lambda i, c: (i, 0, c))],
        out_shape=[jax.ShapeDtypeStruct((m, D_MODEL), F32),
                   jax.ShapeDtypeStruct((m // tm, SUBLANES, 2 * D_FF), F32)],
        scratch_shapes=[pltpu.VMEM((tm, D_MODEL), BF16),
                        pltpu.VMEM((ns, SUBLANES + tm // ns, 2 * tf), F32),
                        pltpu.VMEM((nc, SUBLANES, 2 * tf), F32)],
        compiler_params=_params("arbitrary", "arbitrary"),
        name="ffn_prompt",
    )(x, w_up_il, w_down, cw_il, g, b)
    return out, _deinterleave_chunks(tail[tiles_per_seq - 1::tiles_per_seq], tf)


def _gate_decode_body(ug_ref, uv_ref, sg_ref, sv_ref, cwg_ref, cwv_ref, o_ref):
    def conv(u_ref, s_ref, cw_ref):
        cw = cw_ref[...]
        y = s_ref[0] * cw[0:1]
        y = y + s_ref[1] * cw[1:2]
        return y + u_ref[...] * cw[2:3]

    hg = conv(ug_ref, sg_ref, cwg_ref)
    hv = conv(uv_ref, sv_ref, cwv_ref)
    o_ref[...] = (jax.nn.silu(hg) * hv).astype(BF16)


def _gate_decode(up, state, cw, layer, tf):
    b = up.shape[0]
    nc = D_FF // tf
    gate = lambda c: (0, 2 * c)
    value = lambda c: (0, 2 * c + 1)
    return pl.pallas_call(
        _gate_decode_body,
        grid=(nc,),
        in_specs=[pl.BlockSpec((b, tf), gate),
                  pl.BlockSpec((b, tf), value),
                  pl.BlockSpec((CONV_K - 1, b, tf), lambda c: (0, 0, 2 * c)),
                  pl.BlockSpec((CONV_K - 1, b, tf), lambda c: (0, 0, 2 * c + 1)),
                  pl.BlockSpec((None, CONV_K, tf), lambda c: (layer, 0, 2 * c)),
                  pl.BlockSpec((None, CONV_K, tf), lambda c: (layer, 0, 2 * c + 1))],
        out_specs=pl.BlockSpec((b, tf), lambda c: (0, c)),
        out_shape=jax.ShapeDtypeStruct((b, D_FF), BF16),
        compiler_params=_params("arbitrary"),
        name="gate_decode",
    )(up, up, state, state, cw, cw)


def kernel(x_prompt, x_sample, cache_k, cache_v, state_conv, state_pool, state_ffn, rel_table, w_in, conv_w,
           pool_w, pool_scale, sinks, w_o, ln1_g, ln1_b, w_up, ffn_conv_w, w_down, ln2_g, ln2_b):
    depth = w_in.shape[0]
    bp_, seq, _ = x_prompt.shape
    bs = x_sample.shape[0]
    assert x_sample.shape[1] == 1 and cache_k.shape[2] == WINDOW
    alpha = (2 * depth) ** 0.25
    nc = D_FF // FFN_TF

    w_in_b = _cast_w_in(w_in, 256)
    w_o_b = _cast_w_o(w_o)
    w_up_il = _cast_bf16(w_up, D_MODEL, FFN_TF, lambda j: jnp.where(j < nc, 2 * j, 2 * (j - nc) + 1))
    cw_il = _interleave_chunks(ffn_conv_w, FFN_TF)
    w_down_b = _cast_bf16(w_down, 512)
    pool_w_b = pool_w.astype(BF16)

    bias_by_dist = rel_table.astype(F32)[_t5_bucket(jnp.arange(WINDOW))]
    bias_p = _bias_table(bias_by_dist, WINDOW)
    bias_s = _bias_table(bias_by_dist, DECODE_ROWS)

    ln1_g, ln1_b, ln2_g, ln2_b = (t[:, None, :] for t in (ln1_g, ln1_b, ln2_g, ln2_b))

    xp = x_prompt.reshape(bp_ * seq, D_MODEL)
    xs = x_sample.reshape(bs, D_MODEL)
    pad_rows = lambda t, rows: jnp.pad(t[:, None, :], ((0, 0), (0, rows - 1), (0, 0)))

    outs_p, outs_s = [], []
    for l in range(depth):
        tab_p = (bias_p, _sink_table(sinks[l], WINDOW))
        tab_s = (bias_s, _sink_table(sinks[l], DECODE_ROWS))
        ps = pool_scale[l][None, :]

        z = _matmul(xp, w_in_b, l, 1024, IN_W // 2).reshape(bp_, seq, IN_W)
        cat, u_tail = _mixers_prompt(z, tab_p, conv_w[l], pool_w_b[l], ps)
        x1 = _proj_ln(cat.reshape(bp_ * seq, D_MODEL), w_o_b, l, xp, ln1_g, ln1_b, alpha, 512, D_MODEL)
        xp, up_tail = _ffn_prompt(x1, w_up_il, w_down_b, cw_il, ln2_g, ln2_b, l, alpha, seq, FFN_TM, FFN_TF, FFN_NS)
        outs_p.append((
            z[:, seq - WINDOW:, ATTN_W:ATTN_W + KV_W].reshape(bp_, WINDOW, KV_HEADS, HEAD_DIM),
            z[:, seq - WINDOW:, ATTN_W + KV_W:ATTN_W + 2 * KV_W].reshape(bp_, WINDOW, KV_HEADS, HEAD_DIM),
            u_tail[:, HALO - (CONV_K - 1):],
            z[:, seq - POOL_PREV:, IN_W - POOL_W:],
            up_tail[:, SUBLANES - (CONV_K - 1):],
        ))

        zs = _matmul(xs, w_in_b, l, bs, 512)
        ck = cache_k[l].reshape(bs, WINDOW, KV_W)
        cv = cache_v[l].reshape(bs, WINDOW, KV_W)
        uh = jnp.pad(state_conv[l], ((0, 0), (HALO - (CONV_K - 1), 0), (0, 0)))
        ph = jnp.pad(state_pool[l], ((0, 0), (HALO - POOL_PREV, 0), (0, 0)))
        k_new = zs[:, ATTN_W:ATTN_W + KV_W]
        v_new = zs[:, ATTN_W + KV_W:ATTN_W + 2 * KV_W]
        cat_s, u_new = _mixers_decode(pad_rows(zs, DECODE_ROWS), pad_rows(k_new, WINDOW), pad_rows(v_new, WINDOW),
                                      ck, cv, uh, ph, tab_s, conv_w[l], pool_w_b[l], ps)
        x1s = _proj_ln(cat_s[:, 0], w_o_b, l, xs, ln1_g, ln1_b, alpha, bs, D_MODEL)
        up_s = _matmul(x1s, w_up_il, l, bs, 2 * FFN_TF)
        ffn_state = _interleave_chunks(jnp.swapaxes(state_ffn[l], 0, 1), FFN_TF)
        act_s = _gate_decode(up_s, ffn_state, cw_il, l, FFN_TF)
        xs = _proj_ln(act_s, w_down_b, l, x1s, ln2_g, ln2_b, alpha, bs, 512)
        outs_s.append((
            jnp.concatenate([ck[:, 1:], k_new[:, None]], 1).reshape(bs, WINDOW, KV_HEADS, HEAD_DIM),
            jnp.concatenate([cv[:, 1:], v_new[:, None]], 1).reshape(bs, WINDOW, KV_HEADS, HEAD_DIM),
            u_new[:, HALO - (CONV_K - 1):],
            jnp.concatenate([state_pool[l][:, 1:], zs[:, None, IN_W - POOL_W:]], 1),
            jnp.concatenate([state_ffn[l][:, 1:], _deinterleave_chunks(up_s, FFN_TF)[:, None, :]], 1),
        ))

    st = lambda lst, i: jnp.stack([e[i] for e in lst], 0)
    return (xp.reshape(bp_, seq, D_MODEL), xs.reshape(bs, 1, D_MODEL),
            st(outs_p, 0), st(outs_p, 1), st(outs_p, 2), st(outs_p, 3), st(outs_p, 4),
            st(outs_s, 0), st(outs_s, 1), st(outs_s, 2), st(outs_s, 3), st(outs_s, 4))
```

```python
import functools
import math

import jax
import jax.numpy as jnp
from jax import lax
from jax.experimental import pallas as pl
from jax.experimental.pallas import tpu as pltpu

D_MODEL = 2048
N_HEADS = 16
KV_HEADS = 4
HEAD_DIM = 64
GROUP = N_HEADS // KV_HEADS
ATTN_W = N_HEADS * HEAD_DIM
KV_W = KV_HEADS * HEAD_DIM
WINDOW = 128
ATTN_SCALE = HEAD_DIM ** -0.5
NEG_INF = -1e30
N_BUCKETS = 32
MAX_DISTANCE = 128
CONV_W = D_MODEL // 4
CONV_K = 3
POOL_W = D_MODEL // 4
POOL_WINDOWS = (2, 4, 8, 16)
POOL_GROUP = POOL_W // len(POOL_WINDOWS)
POOL_PREV = 15
IN_W = ATTN_W + 2 * KV_W + 3 * CONV_W + POOL_W
D_FF = 5632
PAST_LEN = 16384
LN_EPS = 1e-5

HALO = 16
SUBLANES = 8
DECODE_ROWS = 128 // GROUP
Q_BLK = 0
K_BLK = ATTN_W // KV_W
V_BLK = K_BLK + 1
GB_BLK = (ATTN_W + 2 * KV_W) // CONV_W
GC_BLK = GB_BLK + 1
H_BLK = GB_BLK + 2
PIN_BLK = GB_BLK + 3

FFN_TM = 1024
FFN_TF = 512
FFN_NS = 2

VMEM_LIMIT = 60 * 1024 * 1024

BF16 = jnp.bfloat16
F32 = jnp.float32


def _params(*sem, flags=None):
    return pltpu.CompilerParams(dimension_semantics=sem, vmem_limit_bytes=VMEM_LIMIT, flags=flags)


def _dot(a, b):
    return jnp.dot(a, b, preferred_element_type=F32)


def _layer_norm(y, g, b):
    mu = jnp.mean(y, -1, keepdims=True)
    yc = y - mu
    var = jnp.mean(yc * yc, -1, keepdims=True)
    return yc * lax.rsqrt(var + LN_EPS) * g + b


def _cast_body(w_ref, o_ref):
    o_ref[...] = w_ref[...].astype(BF16)


def _cast_bf16(w, rows, cols=None, col_perm=None):
    depth, k, n = w.shape
    cols = n if cols is None else cols
    col_perm = (lambda j: j) if col_perm is None else col_perm
    return pl.pallas_call(
        _cast_body,
        grid=(depth, k // rows, n // cols),
        in_specs=[pl.BlockSpec((None, rows, cols), lambda l, i, j: (l, i, j))],
        out_specs=pl.BlockSpec((None, rows, cols), lambda l, i, j: (l, i, col_perm(j))),
        out_shape=jax.ShapeDtypeStruct(w.shape, BF16),
        compiler_params=_params("arbitrary", "arbitrary", "arbitrary"),
        name="cast_bf16",
    )(w)


def _head_block_perm(blk):
    return (blk % GROUP) * KV_HEADS + blk // GROUP


def _cast_w_in_body(w_ref, o_ref):
    w = w_ref[...]
    order = sorted(range(N_HEADS), key=_head_block_perm)
    q = jnp.concatenate([w[:, h * HEAD_DIM:(h + 1) * HEAD_DIM] for h in order], axis=1)
    o_ref[...] = jnp.concatenate([q, w[:, ATTN_W:]], axis=1).astype(BF16)


def _cast_w_in(w, rows):
    depth, k, n = w.shape
    return pl.pallas_call(
        _cast_w_in_body,
        grid=(depth, k // rows),
        in_specs=[pl.BlockSpec((None, rows, n), lambda l, i: (l, i, 0))],
        out_specs=pl.BlockSpec((None, rows, n), lambda l, i: (l, i, 0)),
        out_shape=jax.ShapeDtypeStruct(w.shape, BF16),
        compiler_params=_params("arbitrary", "arbitrary"),
        name="cast_w_in",
    )(w)


def _cast_w_o_body(w_ref, o_ref):
    @pl.when(pl.program_id(1) == 0)
    def _():
        w = w_ref[...]
        order = sorted(range(N_HEADS), key=_head_block_perm)
        o_ref[...] = jnp.concatenate([w[h * HEAD_DIM:(h + 1) * HEAD_DIM] for h in order], axis=0).astype(BF16)

    @pl.when(pl.program_id(1) > 0)
    def _():
        o_ref[...] = w_ref[...].astype(BF16)


def _cast_w_o(w):
    depth, k, n = w.shape
    return pl.pallas_call(
        _cast_w_o_body,
        grid=(depth, k // ATTN_W),
        in_specs=[pl.BlockSpec((None, ATTN_W, n), lambda l, i: (l, i, 0))],
        out_specs=pl.BlockSpec((None, ATTN_W, n), lambda l, i: (l, i, 0)),
        out_shape=jax.ShapeDtypeStruct(w.shape, BF16),
        compiler_params=_params("arbitrary", "arbitrary"),
        name="cast_w_o",
    )(w)


def _matmul_body(x_ref, w_ref, o_ref, xb_ref):
    @pl.when(pl.program_id(1) == 0)
    def _():
        xb_ref[...] = x_ref[...].astype(BF16)

    o_ref[...] = _dot(xb_ref[...], w_ref[...])


def _matmul(x, w, layer, tm, tn, out_col_perm=None):
    m, k = x.shape
    n = w.shape[2]
    out_col_perm = (lambda j: j) if out_col_perm is None else out_col_perm
    return pl.pallas_call(
        _matmul_body,
        grid=(m // tm, n // tn),
        in_specs=[pl.BlockSpec((tm, k), lambda i, j: (i, 0)),
                  pl.BlockSpec((None, k, tn), lambda i, j: (layer, 0, j))],
        out_specs=pl.BlockSpec((tm, tn), lambda i, j: (i, out_col_perm(j))),
        out_shape=jax.ShapeDtypeStruct((m, n), F32),
        scratch_shapes=[pltpu.VMEM((tm, k), BF16)],
        compiler_params=_params("arbitrary", "arbitrary"),
        name="matmul",
    )(x, w)


def _mixers_body(decode, tq, new_rows, pos0, *refs):
    if decode:
        (q_ref, kc_ref, vc_ref, kp_ref, vp_ref, gb_ref, gc_ref, h_ref, pin_ref, us_ref, ps_state_ref,
         bias_ref, sink_ref, cw_ref, pw_ref, ps_ref,
         cat_ref, uo_ref, ko_ref, vo_ref, ext_u, ext_p) = refs
    else:
        (q_ref, kc_ref, vc_ref, kp_ref, vp_ref, gb_ref, gc_ref, h_ref, pin_ref,
         gcp_ref, hp_ref, pinp_ref,
         bias_ref, sink_ref, cw_ref, pw_ref, ps_ref,
         cat_ref, uo_ref, ext_u, ext_p) = refs
    n = pl.program_id(1)
    rows = GROUP * tq

    def new_rows_of(ref, nrows=tq):
        x = ref[...]
        if x.shape[0] == nrows:
            return x
        real = lax.broadcasted_iota(jnp.int32, (nrows, x.shape[1]), 0) < x.shape[0]
        return jnp.where(real, jnp.broadcast_to(x, (nrows, x.shape[1])), 0.0)

    nk = 2 * WINDOW
    q = new_rows_of(q_ref)
    qs = jnp.concatenate([q[:, g * KV_W:(g + 1) * KV_W] for g in range(GROUP)], axis=0)
    qs = (qs * ATTN_SCALE).astype(BF16)
    k_all = jnp.concatenate([kp_ref[...], new_rows_of(kc_ref, WINDOW)], axis=0).astype(BF16)
    v_all = jnp.concatenate([vp_ref[...], new_rows_of(vc_ref, WINDOW)], axis=0)
    if decode:
        for cache_ref, new_ref, out_ref in ((kp_ref, kc_ref, ko_ref), (vp_ref, vc_ref, vo_ref)):
            out_ref[0:WINDOW - new_rows] = cache_ref[new_rows:WINDOW]
            out_ref[WINDOW - new_rows:WINDOW] = new_ref[...]
    lane_head = lax.broadcasted_iota(jnp.int32, (nk, KV_W), 1) // HEAD_DIM
    k_cat = jnp.concatenate([jnp.where(lane_head == kvh, k_all, jnp.zeros((), BF16)) for kvh in range(KV_HEADS)],
                            axis=0)
    st_all = lax.dot_general(k_cat, qs, (((1,), (1,)), ((), ())), preferred_element_type=F32)
    kj = lax.broadcasted_iota(jnp.int32, (nk, rows), 0)
    qi = lax.broadcasted_iota(jnp.int32, (nk, rows), 1) & (tq - 1)
    mask = jnp.logical_and(kj > qi, kj <= qi + WINDOW)
    if not decode:
        mask = jnp.logical_and(mask, jnp.logical_or(n > 0, kj >= WINDOW))
    probs = []
    for kvh in range(KV_HEADS):
        s = jnp.where(mask, st_all[kvh * nk:(kvh + 1) * nk] + bias_ref[kvh], NEG_INF)
        sink = sink_ref[kvh]
        m = jnp.maximum(jnp.max(s, 0, keepdims=True), sink)
        p = jnp.exp(s - m)
        den = jnp.sum(p, 0, keepdims=True) + jnp.exp(sink - m)
        probs.append((p * (1.0 / den)).astype(BF16))
    pt_all = jnp.concatenate(probs, axis=0)
    vt = v_all.T
    row_head = lax.broadcasted_iota(jnp.int32, (KV_W, KV_HEADS * nk), 0) // HEAD_DIM
    col_head = lax.broadcasted_iota(jnp.int32, (KV_W, KV_HEADS * nk), 1) // nk
    vt_cat = jnp.where(row_head == col_head, jnp.concatenate([vt] * KV_HEADS, axis=1), 0.0).astype(BF16)
    o = _dot(vt_cat, pt_all).T
    attn = jnp.concatenate([o[g * tq:(g + 1) * tq] for g in range(GROUP)], axis=1)

    u = new_rows_of(gc_ref) * new_rows_of(h_ref)
    if decode:
        ext_u[HALO - (CONV_K - 1):HALO] = us_ref[...]
    else:
        ext_u[0:HALO] = jnp.where(n > 0, gcp_ref[...] * hp_ref[...], 0.0)
    ext_u[HALO:HALO + tq] = u
    cw = cw_ref[...]
    conv = ext_u[HALO - 2:HALO - 2 + tq] * cw[0:1]
    conv = conv + ext_u[HALO - 1:HALO - 1 + tq] * cw[1:2]
    conv = conv + u * cw[2:3]
    c = new_rows_of(gb_ref) * conv
    uo_ref[...] = ext_u[new_rows + HALO - (CONV_K - 1):new_rows + HALO]

    pin = new_rows_of(pin_ref)
    if decode:
        ext_p[HALO - POOL_PREV:HALO] = ps_state_ref[...]
    else:
        ext_p[0:HALO] = jnp.where(n > 0, pinp_ref[...], 0.0)
    ext_p[HALO:HALO + tq] = pin
    pos = pos0 + n * tq + lax.broadcasted_iota(jnp.int32, (tq, 1), 0)
    pooled = []
    for g, w in enumerate(POOL_WINDOWS):
        lo = g * POOL_GROUP
        cur = pin[:, lo:lo + POOL_GROUP]
        win = ext_p[HALO - (w - 1):HALO - (w - 1) + tq, lo:lo + POOL_GROUP]
        for k in range(w - 2, -1, -1):
            win = win + ext_p[HALO - k:HALO - k + tq, lo:lo + POOL_GROUP]
        cnt = jnp.minimum(pos + 1, w).astype(F32)
        d = (win / cnt - cur).astype(BF16)
        pooled.append(_dot(d, pw_ref[g]))
    pm = jnp.concatenate(pooled, axis=1) * ps_ref[...]

    cat_ref[...] = jnp.concatenate([attn, c, pm], axis=1)[0:cat_ref.shape[0]].astype(BF16)


def _mixers_prompt(z, tables, cw, pw, ps):
    b, s, _ = z.shape
    tq = WINDOW
    nb = s // tq
    bias, sink = tables
    cur = lambda blk: (lambda i, n: (i, n, blk))
    prev = lambda blk: (lambda i, n: (i, jnp.maximum(n - 1, 0), blk))
    halo = lambda blk: (lambda i, n: (i, jnp.maximum(n * (tq // HALO) - 1, 0), blk))
    const3 = lambda i, n: (0, 0, 0)
    const2 = lambda i, n: (0, 0)
    in_specs = [
        pl.BlockSpec((None, tq, ATTN_W), cur(Q_BLK)),
        pl.BlockSpec((None, tq, KV_W), cur(K_BLK)),
        pl.BlockSpec((None, tq, KV_W), cur(V_BLK)),
        pl.BlockSpec((None, tq, KV_W), prev(K_BLK)),
        pl.BlockSpec((None, tq, KV_W), prev(V_BLK)),
        pl.BlockSpec((None, tq, CONV_W), cur(GB_BLK)),
        pl.BlockSpec((None, tq, CONV_W), cur(GC_BLK)),
        pl.BlockSpec((None, tq, CONV_W), cur(H_BLK)),
        pl.BlockSpec((None, tq, POOL_W), cur(PIN_BLK)),
        pl.BlockSpec((None, HALO, CONV_W), halo(GC_BLK)),
        pl.BlockSpec((None, HALO, CONV_W), halo(H_BLK)),
        pl.BlockSpec((None, HALO, POOL_W), halo(PIN_BLK)),
        pl.BlockSpec(bias.shape, const3),
        pl.BlockSpec(sink.shape, const3),
        pl.BlockSpec(cw.shape, const2),
        pl.BlockSpec(pw.shape, const3),
        pl.BlockSpec(ps.shape, const2),
    ]
    return pl.pallas_call(
        functools.partial(_mixers_body, False, tq, tq, 0),
        grid=(b, nb),
        in_specs=in_specs,
        out_specs=[pl.BlockSpec((None, tq, D_MODEL), lambda i, n: (i, n, 0)),
                   pl.BlockSpec((None, CONV_K - 1, CONV_W), lambda i, n: (i, 0, 0))],
        out_shape=[jax.ShapeDtypeStruct((b, s, D_MODEL), BF16),
                   jax.ShapeDtypeStruct((b, CONV_K - 1, CONV_W), F32)],
        scratch_shapes=[pltpu.VMEM((HALO + tq, CONV_W), F32), pltpu.VMEM((HALO + tq, POOL_W), F32)],
        compiler_params=_params("arbitrary", "arbitrary"),
        name="mixers_prompt",
    )(z, z, z, z, z, z, z, z, z, z, z, z, bias, sink, cw, pw, ps)


def _mixers_decode(zs, ck, cv, conv_state, pool_state, tables, cw, pw, ps):
    b = zs.shape[0]
    tq = DECODE_ROWS
    bias, sink = tables
    new = lambda width, blk: pl.BlockSpec((None, 1, width), lambda i, n: (i, 0, blk))
    whole = lambda rows, width: pl.BlockSpec((None, rows, width), lambda i, n: (i, 0, 0))
    const3 = lambda i, n: (0, 0, 0)
    const2 = lambda i, n: (0, 0)
    in_specs = [
        new(ATTN_W, Q_BLK), new(KV_W, K_BLK), new(KV_W, V_BLK),
        whole(WINDOW, KV_W), whole(WINDOW, KV_W),
        new(CONV_W, GB_BLK), new(CONV_W, GC_BLK), new(CONV_W, H_BLK), new(POOL_W, PIN_BLK),
        whole(CONV_K - 1, CONV_W), whole(POOL_PREV, POOL_W),
        pl.BlockSpec(bias.shape, const3),
        pl.BlockSpec(sink.shape, const3),
        pl.BlockSpec(cw.shape, const2),
        pl.BlockSpec(pw.shape, const3),
        pl.BlockSpec(ps.shape, const2),
    ]
    return pl.pallas_call(
        functools.partial(_mixers_body, True, tq, 1, PAST_LEN),
        grid=(b, 1),
        in_specs=in_specs,
        out_specs=[whole(1, D_MODEL), whole(CONV_K - 1, CONV_W), whole(WINDOW, KV_W), whole(WINDOW, KV_W)],
        out_shape=[jax.ShapeDtypeStruct((b, 1, D_MODEL), BF16),
                   jax.ShapeDtypeStruct((b, CONV_K - 1, CONV_W), F32),
                   jax.ShapeDtypeStruct((b, WINDOW, KV_W), F32),
                   jax.ShapeDtypeStruct((b, WINDOW, KV_W), F32)],
        scratch_shapes=[pltpu.VMEM((HALO + tq, CONV_W), F32), pltpu.VMEM((HALO + tq, POOL_W), F32)],
        compiler_params=_params("arbitrary", "arbitrary"),
        name="mixers_decode",
    )(zs, zs, zs, ck, cv, zs, zs, zs, zs, conv_state, pool_state, bias, sink, cw, pw, ps)


def _bias_table(bias_by_dist, tq):
    nk = 2 * WINDOW
    by_offset = jnp.zeros((N_HEADS, nk + 1), F32).at[:, 1:WINDOW + 1].set(bias_by_dist[::-1].T)
    rows = jnp.tile(by_offset, (1, tq))[:, :tq * nk].reshape(N_HEADS, tq, nk)
    return jnp.swapaxes(rows.reshape(KV_HEADS, GROUP * tq, nk), 1, 2)


def _sink_table(sinks_l, tq):
    return jnp.broadcast_to(sinks_l.astype(F32).reshape(KV_HEADS, GROUP, 1, 1),
                            (KV_HEADS, GROUP, 1, tq)).reshape(KV_HEADS, 1, GROUP * tq)


def _t5_bucket(n):
    max_exact = N_BUCKETS // 2
    nf = jnp.maximum(n, 1).astype(F32)
    large = max_exact + (jnp.log(nf / max_exact) / math.log(MAX_DISTANCE / max_exact)
                         * (N_BUCKETS - max_exact)).astype(jnp.int32)
    large = jnp.minimum(large, N_BUCKETS - 1)
    return jnp.where(n < max_exact, n, large)


def _proj_ln_body(nk, alpha, a_ref, w_ref, x_ref, g_ref, b_ref, o_ref, *scratch):
    part = _dot(a_ref[...], w_ref[...])
    if nk == 1:
        o_ref[...] = _layer_norm(alpha * x_ref[...] + part, g_ref[...], b_ref[...])
        return
    acc_ref, = scratch
    k = pl.program_id(1)

    @pl.when(k == 0)
    def _():
        acc_ref[...] = jnp.zeros_like(acc_ref)

    acc_ref[...] += part

    @pl.when(k == nk - 1)
    def _():
        o_ref[...] = _layer_norm(alpha * x_ref[...] + acc_ref[...], g_ref[...], b_ref[...])


def _proj_ln(a, w, layer, x, g, b, alpha, tm, tk):
    m, k = a.shape
    n = w.shape[2]
    nk = k // tk
    return pl.pallas_call(
        functools.partial(_proj_ln_body, nk, alpha),
        grid=(m // tm, nk),
        in_specs=[pl.BlockSpec((tm, tk), lambda i, kk: (i, kk)),
                  pl.BlockSpec((None, tk, n), lambda i, kk: (layer, kk, 0)),
                  pl.BlockSpec((tm, n), lambda i, kk: (i, 0)),
                  pl.BlockSpec((None, 1, n), lambda i, kk: (layer, 0, 0)),
                  pl.BlockSpec((None, 1, n), lambda i, kk: (layer, 0, 0))],
        out_specs=pl.BlockSpec((tm, n), lambda i, kk: (i, 0)),
        out_shape=jax.ShapeDtypeStruct((m, n), F32),
        scratch_shapes=[] if nk == 1 else [pltpu.VMEM((tm, n), F32)],
        compiler_params=_params("arbitrary", "arbitrary"),
        name="proj_ln",
    )(a, w, x, g, b)


def _ffn_body(tm, tf, ns, tiles_per_seq, nc, alpha, x_ref, wu_ref, wd_ref, cwg_ref, cwv_ref, g_ref, b_ref,
              o_ref, tail_g_ref, tail_v_ref, xb_ref, ext, carry):
    i = pl.program_id(0)
    c = pl.program_id(1)
    ts = tm // ns

    @pl.when(c == 0)
    def _():
        xb_ref[...] = x_ref[...].astype(BF16)
        o_ref[...] = jnp.zeros_like(o_ref)

    seq_start = (i % tiles_per_seq) == 0

    def up_proj(h, prev_tail):
        up = _dot(xb_ref[pl.ds(h * ts, ts)], wu_ref[...])
        if h == 0:
            @pl.when(seq_start)
            def _():
                ext[0, 0:SUBLANES] = jnp.zeros((SUBLANES, 2 * tf), F32)

            @pl.when(jnp.logical_not(seq_start))
            def _():
                ext[0, 0:SUBLANES] = carry[c]
        else:
            ext[h, 0:SUBLANES] = prev_tail
        ext[h, SUBLANES:SUBLANES + ts] = up
        tail = up[ts - SUBLANES:ts]
        if h == ns - 1:
            carry[c] = tail
            tail_g_ref[...] = tail[:, :tf]
            tail_v_ref[...] = tail[:, tf:]
        return tail

    def gate(h):
        cw = jnp.concatenate([cwg_ref[...], cwv_ref[...]], axis=1)
        hc = ext[h, SUBLANES - 2:SUBLANES - 2 + ts] * cw[0:1]
        hc = hc + ext[h, SUBLANES - 1:SUBLANES - 1 + ts] * cw[1:2]
        hc = hc + ext[h, SUBLANES:SUBLANES + ts] * cw[2:3]
        return (jax.nn.silu(hc[:, :tf]) * hc[:, tf:]).astype(BF16)

    def down_proj(h, act):
        o_ref[pl.ds(h * ts, ts)] += _dot(act, wd_ref[...])

    tail = up_proj(0, None)
    for h in range(ns):
        act = gate(h)
        if h + 1 < ns:
            tail = up_proj(h + 1, tail)
        down_proj(h, act)

    @pl.when(c == nc - 1)
    def _():
        o_ref[...] = _layer_norm(alpha * x_ref[...] + o_ref[...], g_ref[...], b_ref[...])


def _ffn_prompt(x, w_up_il, w_down, cw, g, b, layer, alpha, seq, tm, tf, ns):
    m = x.shape[0]
    nc = D_FF // tf
    tiles_per_seq = seq // tm
    tail_spec = pl.BlockSpec((None, SUBLANES, tf), lambda i, c: (i, 0, c))
    tail_shape = jax.ShapeDtypeStruct((m // tm, SUBLANES, D_FF), F32)
    out, tail_g, tail_v = pl.pallas_call(
        functools.partial(_ffn_body, tm, tf, ns, tiles_per_seq, nc, alpha),
        grid=(m // tm, nc),
        in_specs=[pl.BlockSpec((tm, D_MODEL), lambda i, c: (i, 0), pipeline_mode=pl.Buffered(1)),
                  pl.BlockSpec((None, D_MODEL, 2 * tf), lambda i, c: (layer, 0, c)),
                  pl.BlockSpec((None, tf, D_MODEL), lambda i, c: (layer, c, 0)),
                  pl.BlockSpec((None, CONV_K, tf), lambda i, c: (layer, 0, c)),
                  pl.BlockSpec((None, CONV_K, tf), lambda i, c: (layer, 0, nc + c)),
                  pl.BlockSpec((None, 1, D_MODEL), lambda i, c: (layer, 0, 0)),
                  pl.BlockSpec((None, 1, D_MODEL), lambda i, c: (layer, 0, 0))],
        out_specs=[pl.BlockSpec((tm, D_MODEL), lambda i, c: (i, 0)), tail_spec, tail_spec],
        out_shape=[jax.ShapeDtypeStruct((m, D_MODEL), F32), tail_shape, tail_shape],
        scratch_shapes=[pltpu.VMEM((tm, D_MODEL), BF16),
                        pltpu.VMEM((ns, SUBLANES + tm // ns, 2 * tf), F32),
                        pltpu.VMEM((nc, SUBLANES, 2 * tf), F32)],
        compiler_params=_params("arbitrary", "arbitrary"),
        name="ffn_prompt",
    )(x, w_up_il, w_down, cw, cw, g, b)
    last = slice(tiles_per_seq - 1, None, tiles_per_seq)
    return out, jnp.concatenate([tail_g[last], tail_v[last]], -1)


def _gate_decode_body(ug_ref, uv_ref, sg_ref, sv_ref, cwg_ref, cwv_ref, o_ref):
    def conv(u_ref, s_ref, cw_ref):
        cw = cw_ref[...]
        y = s_ref[:, 0, :] * cw[0:1]
        y = y + s_ref[:, 1, :] * cw[1:2]
        return y + u_ref[...] * cw[2:3]

    hg = conv(ug_ref, sg_ref, cwg_ref)
    hv = conv(uv_ref, sv_ref, cwv_ref)
    o_ref[...] = (jax.nn.silu(hg) * hv).astype(BF16)


def _gate_decode(up, state, cw, layer, tf):
    b = up.shape[0]
    nc = D_FF // tf
    return pl.pallas_call(
        _gate_decode_body,
        grid=(nc,),
        in_specs=[pl.BlockSpec((b, tf), lambda c: (0, c)),
                  pl.BlockSpec((b, tf), lambda c: (0, nc + c)),
                  pl.BlockSpec((b, CONV_K - 1, tf), lambda c: (0, 0, c)),
                  pl.BlockSpec((b, CONV_K - 1, tf), lambda c: (0, 0, nc + c)),
                  pl.BlockSpec((None, CONV_K, tf), lambda c: (layer, 0, c)),
                  pl.BlockSpec((None, CONV_K, tf), lambda c: (layer, 0, nc + c))],
        out_specs=pl.BlockSpec((b, tf), lambda c: (0, c)),
        out_shape=jax.ShapeDtypeStruct((b, D_FF), BF16),
        compiler_params=_params("arbitrary"),
        name="gate_decode",
    )(up, up, state, state, cw, cw)


def kernel(x_prompt, x_sample, cache_k, cache_v, state_conv, state_pool, state_ffn, rel_table, w_in, conv_w,
           pool_w, pool_scale, sinks, w_o, ln1_g, ln1_b, w_up, ffn_conv_w, w_down, ln2_g, ln2_b):
    depth = w_in.shape[0]
    bp_, seq, _ = x_prompt.shape
    bs = x_sample.shape[0]
    assert x_sample.shape[1] == 1 and cache_k.shape[2] == WINDOW
    alpha = (2 * depth) ** 0.25
    nc = D_FF // FFN_TF

    w_in_b = _cast_w_in(w_in, 256)
    w_o_b = _cast_w_o(w_o)
    w_up_il = _cast_bf16(w_up, D_MODEL, FFN_TF, lambda j: jnp.where(j < nc, 2 * j, 2 * (j - nc) + 1))
    natural_col = lambda j: jnp.where(j % 2 == 0, j // 2, nc + j // 2)
    w_down_b = _cast_bf16(w_down, 512)
    pool_w_b = pool_w.astype(BF16)

    bias_by_dist = rel_table.astype(F32)[_t5_bucket(jnp.arange(WINDOW))]
    bias_p = _bias_table(bias_by_dist, WINDOW)
    bias_s = _bias_table(bias_by_dist, DECODE_ROWS)

    ln1_g, ln1_b, ln2_g, ln2_b = (t[:, None, :] for t in (ln1_g, ln1_b, ln2_g, ln2_b))

    xp = x_prompt.reshape(bp_ * seq, D_MODEL)
    xs = x_sample.reshape(bs, D_MODEL)

    outs_p, outs_s = [], []
    for l in range(depth):
        tab_p = (bias_p, _sink_table(sinks[l], WINDOW))
        tab_s = (bias_s, _sink_table(sinks[l], DECODE_ROWS))
        ps = pool_scale[l][None, :]

        z = _matmul(xp, w_in_b, l, 1024, IN_W // 2).reshape(bp_, seq, IN_W)
        cat, u_tail = _mixers_prompt(z, tab_p, conv_w[l], pool_w_b[l], ps)
        x1 = _proj_ln(cat.reshape(bp_ * seq, D_MODEL), w_o_b, l, xp, ln1_g, ln1_b, alpha, 512, D_MODEL)
        xp, up_tail = _ffn_prompt(x1, w_up_il, w_down_b, ffn_conv_w, ln2_g, ln2_b, l, alpha, seq,
                                  FFN_TM, FFN_TF, FFN_NS)
        outs_p.append((
            z[:, seq - WINDOW:, ATTN_W:ATTN_W + KV_W].reshape(bp_, WINDOW, KV_HEADS, HEAD_DIM),
            z[:, seq - WINDOW:, ATTN_W + KV_W:ATTN_W + 2 * KV_W].reshape(bp_, WINDOW, KV_HEADS, HEAD_DIM),
            u_tail,
            z[:, seq - POOL_PREV:, IN_W - POOL_W:],
            up_tail[:, SUBLANES - (CONV_K - 1):],
        ))

        zs = _matmul(xs, w_in_b, l, bs, 512)
        ck = cache_k[l].reshape(bs, WINDOW, KV_W)
        cv = cache_v[l].reshape(bs, WINDOW, KV_W)
        cat_s, u_new, k_s, v_s = _mixers_decode(zs[:, None, :], ck, cv, state_conv[l], state_pool[l], tab_s,
                                                conv_w[l], pool_w_b[l], ps)
        x1s = _proj_ln(cat_s.reshape(bs, D_MODEL), w_o_b, l, xs, ln1_g, ln1_b, alpha, bs, D_MODEL)
        up_s = _matmul(x1s, w_up_il, l, bs, FFN_TF, natural_col)
        act_s = _gate_decode(up_s, state_ffn[l], ffn_conv_w, l, FFN_TF)
        xs = _proj_ln(act_s, w_down_b, l, x1s, ln2_g, ln2_b, alpha, bs, 512)
        outs_s.append((
            k_s.reshape(bs, WINDOW, KV_HEADS, HEAD_DIM),
            v_s.reshape(bs, WINDOW, KV_HEADS, HEAD_DIM),
            u_new,
            jnp.concatenate([state_pool[l][:, 1:], zs[:, None, IN_W - POOL_W:]], 1),
            jnp.concatenate([state_ffn[l][:, 1:], up_s[:, None, :]], 1),
        ))

    st = lambda lst, i: jnp.stack([e[i] for e in lst], 0)
    return (xp.reshape(bp_, seq, D_MODEL), xs.reshape(bs, 1, D_MODEL),
            st(outs_p, 0), st(outs_p, 1), st(outs_p, 2), st(outs_p, 3), st(outs_p, 4),
            st(outs_s, 0), st(outs_s, 1), st(outs_s, 2), st(outs_s, 3), st(outs_s, 4))
```

```python
import functools
import math

import jax
import jax.numpy as jnp
from jax import lax
from jax.experimental import pallas as pl
from jax.experimental.pallas import tpu as pltpu

D_MODEL = 2048
N_HEADS = 16
KV_HEADS = 4
HEAD_DIM = 64
GROUP = N_HEADS // KV_HEADS
ATTN_W = N_HEADS * HEAD_DIM
KV_W = KV_HEADS * HEAD_DIM
WINDOW = 128
ATTN_SCALE = HEAD_DIM ** -0.5
NEG_INF = -1e30
N_BUCKETS = 32
MAX_DISTANCE = 128
CONV_W = D_MODEL // 4
CONV_K = 3
POOL_W = D_MODEL // 4
POOL_WINDOWS = (2, 4, 8, 16)
POOL_GROUP = POOL_W // len(POOL_WINDOWS)
POOL_PREV = 15
IN_W = ATTN_W + 2 * KV_W + 3 * CONV_W + POOL_W
D_FF = 5632
PAST_LEN = 16384
LN_EPS = 1e-5

HALO = 16
SUBLANES = 8
DECODE_ROWS = 128 // GROUP
Q_BLK = 0
K_BLK = ATTN_W // KV_W
V_BLK = K_BLK + 1
GB_BLK = (ATTN_W + 2 * KV_W) // CONV_W
GC_BLK = GB_BLK + 1
H_BLK = GB_BLK + 2
PIN_BLK = GB_BLK + 3

FFN_TM = 1024
FFN_TF = 512
FFN_NS = 4

VMEM_LIMIT = 60 * 1024 * 1024

BF16 = jnp.bfloat16
F32 = jnp.float32


def _params(*sem, flags=None):
    return pltpu.CompilerParams(dimension_semantics=sem, vmem_limit_bytes=VMEM_LIMIT, flags=flags)


def _dot(a, b):
    return jnp.dot(a, b, preferred_element_type=F32)


def _layer_norm(y, g, b):
    mu = jnp.mean(y, -1, keepdims=True)
    yc = y - mu
    var = jnp.mean(yc * yc, -1, keepdims=True)
    return yc * lax.rsqrt(var + LN_EPS) * g + b


def _cast_body(w_ref, o_ref):
    o_ref[...] = w_ref[...].astype(BF16)


def _cast_bf16(w, rows, cols=None, col_perm=None):
    depth, k, n = w.shape
    cols = n if cols is None else cols
    col_perm = (lambda j: j) if col_perm is None else col_perm
    return pl.pallas_call(
        _cast_body,
        grid=(depth, k // rows, n // cols),
        in_specs=[pl.BlockSpec((None, rows, cols), lambda l, i, j: (l, i, j))],
        out_specs=pl.BlockSpec((None, rows, cols), lambda l, i, j: (l, i, col_perm(j))),
        out_shape=jax.ShapeDtypeStruct(w.shape, BF16),
        compiler_params=_params("arbitrary", "arbitrary", "arbitrary"),
        name="cast_bf16",
    )(w)


def _head_block_perm(blk):
    return (blk % GROUP) * KV_HEADS + blk // GROUP


def _cast_w_in_body(w_ref, o_ref):
    w = w_ref[...]
    order = sorted(range(N_HEADS), key=_head_block_perm)
    q = jnp.concatenate([w[:, h * HEAD_DIM:(h + 1) * HEAD_DIM] for h in order], axis=1)
    o_ref[...] = jnp.concatenate([q, w[:, ATTN_W:]], axis=1).astype(BF16)


def _cast_w_in(w, rows):
    depth, k, n = w.shape
    return pl.pallas_call(
        _cast_w_in_body,
        grid=(depth, k // rows),
        in_specs=[pl.BlockSpec((None, rows, n), lambda l, i: (l, i, 0))],
        out_specs=pl.BlockSpec((None, rows, n), lambda l, i: (l, i, 0)),
        out_shape=jax.ShapeDtypeStruct(w.shape, BF16),
        compiler_params=_params("arbitrary", "arbitrary"),
        name="cast_w_in",
    )(w)


def _cast_w_o_body(w_ref, o_ref):
    @pl.when(pl.program_id(1) == 0)
    def _():
        w = w_ref[...]
        order = sorted(range(N_HEADS), key=_head_block_perm)
        o_ref[...] = jnp.concatenate([w[h * HEAD_DIM:(h + 1) * HEAD_DIM] for h in order], axis=0).astype(BF16)

    @pl.when(pl.program_id(1) > 0)
    def _():
        o_ref[...] = w_ref[...].astype(BF16)


def _cast_w_o(w):
    depth, k, n = w.shape
    return pl.pallas_call(
        _cast_w_o_body,
        grid=(depth, k // ATTN_W),
        in_specs=[pl.BlockSpec((None, ATTN_W, n), lambda l, i: (l, i, 0))],
        out_specs=pl.BlockSpec((None, ATTN_W, n), lambda l, i: (l, i, 0)),
        out_shape=jax.ShapeDtypeStruct(w.shape, BF16),
        compiler_params=_params("arbitrary", "arbitrary"),
        name="cast_w_o",
    )(w)


def _matmul_body(x_ref, w_ref, o_ref, xb_ref):
    @pl.when(pl.program_id(1) == 0)
    def _():
        xb_ref[...] = x_ref[...].astype(BF16)

    o_ref[...] = _dot(xb_ref[...], w_ref[...])


def _matmul(x, w, layer, tm, tn, out_col_perm=None):
    m, k = x.shape
    n = w.shape[2]
    out_col_perm = (lambda j: j) if out_col_perm is None else out_col_perm
    return pl.pallas_call(
        _matmul_body,
        grid=(m // tm, n // tn),
        in_specs=[pl.BlockSpec((tm, k), lambda i, j: (i, 0)),
                  pl.BlockSpec((None, k, tn), lambda i, j: (layer, 0, j))],
        out_specs=pl.BlockSpec((tm, tn), lambda i, j: (i, out_col_perm(j))),
        out_shape=jax.ShapeDtypeStruct((m, n), F32),
        scratch_shapes=[pltpu.VMEM((tm, k), BF16)],
        compiler_params=_params("arbitrary", "arbitrary"),
        name="matmul",
    )(x, w)


def _mixers_body(decode, tq, new_rows, pos0, alpha, *refs):
    if decode:
        (q_ref, kc_ref, vc_ref, kp_ref, vp_ref, gb_ref, gc_ref, h_ref, pin_ref, us_ref, ps_state_ref,
         bias_ref, sink_ref, cw_ref, pw_ref, ps_ref,
         cat_ref, uo_ref, ko_ref, vo_ref, ext_u, ext_p) = refs
    else:
        (q_ref, kc_ref, vc_ref, kp_ref, vp_ref, gb_ref, gc_ref, h_ref, pin_ref,
         gcp_ref, hp_ref, pinp_ref,
         bias_ref, sink_ref, cw_ref, pw_ref, ps_ref, wo_ref, x_ref, lng_ref, lnb_ref,
         x1_ref, uo_ref, ext_u, ext_p) = refs
    n = pl.program_id(1)
    rows = GROUP * tq

    def new_rows_of(ref, nrows=tq):
        x = ref[...]
        if x.shape[0] == nrows:
            return x
        real = lax.broadcasted_iota(jnp.int32, (nrows, x.shape[1]), 0) < x.shape[0]
        return jnp.where(real, jnp.broadcast_to(x, (nrows, x.shape[1])), 0.0)

    nk = 2 * WINDOW
    q = new_rows_of(q_ref)
    qs = jnp.concatenate([q[:, g * KV_W:(g + 1) * KV_W] for g in range(GROUP)], axis=0)
    qs = (qs * ATTN_SCALE).astype(BF16)
    k_all = jnp.concatenate([kp_ref[...], new_rows_of(kc_ref, WINDOW)], axis=0).astype(BF16)
    v_all = jnp.concatenate([vp_ref[...], new_rows_of(vc_ref, WINDOW)], axis=0)
    if decode:
        for cache_ref, new_ref, out_ref in ((kp_ref, kc_ref, ko_ref), (vp_ref, vc_ref, vo_ref)):
            out_ref[0:WINDOW - new_rows] = cache_ref[new_rows:WINDOW]
            out_ref[WINDOW - new_rows:WINDOW] = new_ref[...]
    lane_head = lax.broadcasted_iota(jnp.int32, (nk, KV_W), 1) // HEAD_DIM
    k_cat = jnp.concatenate([jnp.where(lane_head == kvh, k_all, jnp.zeros((), BF16)) for kvh in range(KV_HEADS)],
                            axis=0)
    st_all = lax.dot_general(k_cat, qs, (((1,), (1,)), ((), ())), preferred_element_type=F32)
    kj = lax.broadcasted_iota(jnp.int32, (nk, rows), 0)
    qi = lax.broadcasted_iota(jnp.int32, (nk, rows), 1) & (tq - 1)
    mask = jnp.logical_and(kj > qi, kj <= qi + WINDOW)
    if not decode:
        mask = jnp.logical_and(mask, jnp.logical_or(n > 0, kj >= WINDOW))
    probs = []
    for kvh in range(KV_HEADS):
        s = jnp.where(mask, st_all[kvh * nk:(kvh + 1) * nk] + bias_ref[kvh], NEG_INF)
        sink = sink_ref[kvh]
        m = jnp.maximum(jnp.max(s, 0, keepdims=True), sink)
        p = jnp.exp(s - m)
        den = jnp.sum(p, 0, keepdims=True) + jnp.exp(sink - m)
        probs.append((p * (1.0 / den)).astype(BF16))
    pt_all = jnp.concatenate(probs, axis=0)
    vt = v_all.T
    row_head = lax.broadcasted_iota(jnp.int32, (KV_W, KV_HEADS * nk), 0) // HEAD_DIM
    col_head = lax.broadcasted_iota(jnp.int32, (KV_W, KV_HEADS * nk), 1) // nk
    vt_cat = jnp.where(row_head == col_head, jnp.concatenate([vt] * KV_HEADS, axis=1), 0.0).astype(BF16)
    o = _dot(vt_cat, pt_all).T
    attn = jnp.concatenate([o[g * tq:(g + 1) * tq] for g in range(GROUP)], axis=1)

    u = new_rows_of(gc_ref) * new_rows_of(h_ref)
    if decode:
        ext_u[HALO - (CONV_K - 1):HALO] = us_ref[...]
    else:
        ext_u[0:HALO] = jnp.where(n > 0, gcp_ref[...] * hp_ref[...], 0.0)
    ext_u[HALO:HALO + tq] = u
    cw = cw_ref[...]
    conv = ext_u[HALO - 2:HALO - 2 + tq] * cw[0:1]
    conv = conv + ext_u[HALO - 1:HALO - 1 + tq] * cw[1:2]
    conv = conv + u * cw[2:3]
    c = new_rows_of(gb_ref) * conv
    uo_ref[...] = ext_u[new_rows + HALO - (CONV_K - 1):new_rows + HALO]

    pin = new_rows_of(pin_ref)
    if decode:
        ext_p[HALO - POOL_PREV:HALO] = ps_state_ref[...]
    else:
        ext_p[0:HALO] = jnp.where(n > 0, pinp_ref[...], 0.0)
    ext_p[HALO:HALO + tq] = pin
    pos = pos0 + n * tq + lax.broadcasted_iota(jnp.int32, (tq, 1), 0)
    pooled = []
    for g, w in enumerate(POOL_WINDOWS):
        lo = g * POOL_GROUP
        cur = pin[:, lo:lo + POOL_GROUP]
        win = ext_p[HALO - (w - 1):HALO - (w - 1) + tq, lo:lo + POOL_GROUP]
        for k in range(w - 2, -1, -1):
            win = win + ext_p[HALO - k:HALO - k + tq, lo:lo + POOL_GROUP]
        cnt = jnp.minimum(pos + 1, w).astype(F32)
        d = (win / cnt - cur).astype(BF16)
        pooled.append(_dot(d, pw_ref[g]))
    pm = jnp.concatenate(pooled, axis=1) * ps_ref[...]

    cat = jnp.concatenate([attn, c, pm], axis=1).astype(BF16)
    if decode:
        cat_ref[...] = cat[0:new_rows]
    else:
        x1_ref[...] = _layer_norm(alpha * x_ref[...] + _dot(cat, wo_ref[...]), lng_ref[...], lnb_ref[...])


def _mixers_prompt(z, tables, cw, pw, ps, w_o, layer, x, ln_g, ln_b, alpha):
    b, s, _ = z.shape
    tq = WINDOW
    nb = s // tq
    bias, sink = tables
    cur = lambda blk: (lambda i, n: (i, n, blk))
    prev = lambda blk: (lambda i, n: (i, jnp.maximum(n - 1, 0), blk))
    halo = lambda blk: (lambda i, n: (i, jnp.maximum(n * (tq // HALO) - 1, 0), blk))
    const3 = lambda i, n: (0, 0, 0)
    const2 = lambda i, n: (0, 0)
    in_specs = [
        pl.BlockSpec((None, tq, ATTN_W), cur(Q_BLK)),
        pl.BlockSpec((None, tq, KV_W), cur(K_BLK)),
        pl.BlockSpec((None, tq, KV_W), cur(V_BLK)),
        pl.BlockSpec((None, tq, KV_W), prev(K_BLK)),
        pl.BlockSpec((None, tq, KV_W), prev(V_BLK)),
        pl.BlockSpec((None, tq, CONV_W), cur(GB_BLK)),
        pl.BlockSpec((None, tq, CONV_W), cur(GC_BLK)),
        pl.BlockSpec((None, tq, CONV_W), cur(H_BLK)),
        pl.BlockSpec((None, tq, POOL_W), cur(PIN_BLK)),
        pl.BlockSpec((None, HALO, CONV_W), halo(GC_BLK)),
        pl.BlockSpec((None, HALO, CONV_W), halo(H_BLK)),
        pl.BlockSpec((None, HALO, POOL_W), halo(PIN_BLK)),
        pl.BlockSpec(bias.shape, const3),
        pl.BlockSpec(sink.shape, const3),
        pl.BlockSpec(cw.shape, const2),
        pl.BlockSpec(pw.shape, const3),
        pl.BlockSpec(ps.shape, const2),
        pl.BlockSpec((None, D_MODEL, D_MODEL), lambda i, n: (layer, 0, 0), pipeline_mode=pl.Buffered(1)),
        pl.BlockSpec((None, tq, D_MODEL), lambda i, n: (i, n, 0)),
        pl.BlockSpec((None, 1, D_MODEL), lambda i, n: (layer, 0, 0)),
        pl.BlockSpec((None, 1, D_MODEL), lambda i, n: (layer, 0, 0)),
    ]
    return pl.pallas_call(
        functools.partial(_mixers_body, False, tq, tq, 0, alpha),
        grid=(b, nb),
        in_specs=in_specs,
        out_specs=[pl.BlockSpec((None, tq, D_MODEL), lambda i, n: (i, n, 0)),
                   pl.BlockSpec((None, CONV_K - 1, CONV_W), lambda i, n: (i, 0, 0))],
        out_shape=[jax.ShapeDtypeStruct((b, s, D_MODEL), F32),
                   jax.ShapeDtypeStruct((b, CONV_K - 1, CONV_W), F32)],
        scratch_shapes=[pltpu.VMEM((HALO + tq, CONV_W), F32), pltpu.VMEM((HALO + tq, POOL_W), F32)],
        compiler_params=_params("arbitrary", "arbitrary"),
        name="mixers_prompt",
    )(z, z, z, z, z, z, z, z, z, z, z, z, bias, sink, cw, pw, ps, w_o, x, ln_g, ln_b)


def _mixers_decode(zs, ck, cv, conv_state, pool_state, tables, cw, pw, ps):
    b = zs.shape[0]
    tq = DECODE_ROWS
    bias, sink = tables
    new = lambda width, blk: pl.BlockSpec((None, 1, width), lambda i, n: (i, 0, blk))
    whole = lambda rows, width: pl.BlockSpec((None, rows, width), lambda i, n: (i, 0, 0))
    const3 = lambda i, n: (0, 0, 0)
    const2 = lambda i, n: (0, 0)
    in_specs = [
        new(ATTN_W, Q_BLK), new(KV_W, K_BLK), new(KV_W, V_BLK),
        whole(WINDOW, KV_W), whole(WINDOW, KV_W),
        new(CONV_W, GB_BLK), new(CONV_W, GC_BLK), new(CONV_W, H_BLK), new(POOL_W, PIN_BLK),
        whole(CONV_K - 1, CONV_W), whole(POOL_PREV, POOL_W),
        pl.BlockSpec(bias.shape, const3),
        pl.BlockSpec(sink.shape, const3),
        pl.BlockSpec(cw.shape, const2),
        pl.BlockSpec(pw.shape, const3),
        pl.BlockSpec(ps.shape, const2),
    ]
    return pl.pallas_call(
        functools.partial(_mixers_body, True, tq, 1, PAST_LEN, None),
        grid=(b, 1),
        in_specs=in_specs,
        out_specs=[whole(1, D_MODEL), whole(CONV_K - 1, CONV_W), whole(WINDOW, KV_W), whole(WINDOW, KV_W)],
        out_shape=[jax.ShapeDtypeStruct((b, 1, D_MODEL), BF16),
                   jax.ShapeDtypeStruct((b, CONV_K - 1, CONV_W), F32),
                   jax.ShapeDtypeStruct((b, WINDOW, KV_W), F32),
                   jax.ShapeDtypeStruct((b, WINDOW, KV_W), F32)],
        scratch_shapes=[pltpu.VMEM((HALO + tq, CONV_W), F32), pltpu.VMEM((HALO + tq, POOL_W), F32)],
        compiler_params=_params("arbitrary", "arbitrary"),
        name="mixers_decode",
    )(zs, zs, zs, ck, cv, zs, zs, zs, zs, conv_state, pool_state, bias, sink, cw, pw, ps)


def _bias_table(bias_by_dist, tq):
    nk = 2 * WINDOW
    by_offset = jnp.zeros((N_HEADS, nk + 1), F32).at[:, 1:WINDOW + 1].set(bias_by_dist[::-1].T)
    rows = jnp.tile(by_offset, (1, tq))[:, :tq * nk].reshape(N_HEADS, tq, nk)
    return jnp.swapaxes(rows.reshape(KV_HEADS, GROUP * tq, nk), 1, 2)


def _sink_table(sinks_l, tq):
    return jnp.broadcast_to(sinks_l.astype(F32).reshape(KV_HEADS, GROUP, 1, 1),
                            (KV_HEADS, GROUP, 1, tq)).reshape(KV_HEADS, 1, GROUP * tq)


def _t5_bucket(n):
    max_exact = N_BUCKETS // 2
    nf = jnp.maximum(n, 1).astype(F32)
    large = max_exact + (jnp.log(nf / max_exact) / math.log(MAX_DISTANCE / max_exact)
                         * (N_BUCKETS - max_exact)).astype(jnp.int32)
    large = jnp.minimum(large, N_BUCKETS - 1)
    return jnp.where(n < max_exact, n, large)


def _proj_ln_body(nk, alpha, a_ref, w_ref, x_ref, g_ref, b_ref, o_ref, *scratch):
    part = _dot(a_ref[...], w_ref[...])
    if nk == 1:
        o_ref[...] = _layer_norm(alpha * x_ref[...] + part, g_ref[...], b_ref[...])
        return
    acc_ref, = scratch
    k = pl.program_id(1)

    @pl.when(k == 0)
    def _():
        acc_ref[...] = jnp.zeros_like(acc_ref)

    acc_ref[...] += part

    @pl.when(k == nk - 1)
    def _():
        o_ref[...] = _layer_norm(alpha * x_ref[...] + acc_ref[...], g_ref[...], b_ref[...])


def _proj_ln(a, w, layer, x, g, b, alpha, tm, tk):
    m, k = a.shape
    n = w.shape[2]
    nk = k // tk
    return pl.pallas_call(
        functools.partial(_proj_ln_body, nk, alpha),
        grid=(m // tm, nk),
        in_specs=[pl.BlockSpec((tm, tk), lambda i, kk: (i, kk)),
                  pl.BlockSpec((None, tk, n), lambda i, kk: (layer, kk, 0)),
                  pl.BlockSpec((tm, n), lambda i, kk: (i, 0)),
                  pl.BlockSpec((None, 1, n), lambda i, kk: (layer, 0, 0)),
                  pl.BlockSpec((None, 1, n), lambda i, kk: (layer, 0, 0))],
        out_specs=pl.BlockSpec((tm, n), lambda i, kk: (i, 0)),
        out_shape=jax.ShapeDtypeStruct((m, n), F32),
        scratch_shapes=[] if nk == 1 else [pltpu.VMEM((tm, n), F32)],
        compiler_params=_params("arbitrary", "arbitrary"),
        name="proj_ln",
    )(a, w, x, g, b)


def _ffn_body(tm, tf, ns, tiles_per_seq, nc, alpha, x_ref, wu_ref, wd_ref, cwg_ref, cwv_ref, g_ref, b_ref,
              o_ref, tail_g_ref, tail_v_ref, xb_ref, ext, carry):
    i = pl.program_id(0)
    c = pl.program_id(1)
    ts = tm // ns

    @pl.when(c == 0)
    def _():
        xb_ref[...] = x_ref[...].astype(BF16)
        o_ref[...] = jnp.zeros_like(o_ref)

    seq_start = (i % tiles_per_seq) == 0

    def up_proj(h, prev_tail):
        up = _dot(xb_ref[pl.ds(h * ts, ts)], wu_ref[...])
        if h == 0:
            @pl.when(seq_start)
            def _():
                ext[0, 0:SUBLANES] = jnp.zeros((SUBLANES, 2 * tf), F32)

            @pl.when(jnp.logical_not(seq_start))
            def _():
                ext[0, 0:SUBLANES] = carry[c]
        else:
            ext[h, 0:SUBLANES] = prev_tail
        ext[h, SUBLANES:SUBLANES + ts] = up
        tail = up[ts - SUBLANES:ts]
        if h == ns - 1:
            carry[c] = tail
            tail_g_ref[...] = tail[:, :tf]
            tail_v_ref[...] = tail[:, tf:]
        return tail

    def gate(h):
        cw = jnp.concatenate([cwg_ref[...], cwv_ref[...]], axis=1)
        hc = ext[h, SUBLANES - 2:SUBLANES - 2 + ts] * cw[0:1]
        hc = hc + ext[h, SUBLANES - 1:SUBLANES - 1 + ts] * cw[1:2]
        hc = hc + ext[h, SUBLANES:SUBLANES + ts] * cw[2:3]
        return (jax.nn.silu(hc[:, :tf]) * hc[:, tf:]).astype(BF16)

    def down_proj(h, act):
        o_ref[pl.ds(h * ts, ts)] += _dot(act, wd_ref[...])

    tail = up_proj(0, None)
    for h in range(ns):
        act = gate(h)
        if h + 1 < ns:
            tail = up_proj(h + 1, tail)
        down_proj(h, act)

    @pl.when(c == nc - 1)
    def _():
        o_ref[...] = _layer_norm(alpha * x_ref[...] + o_ref[...], g_ref[...], b_ref[...])


def _ffn_prompt(x, w_up_il, w_down, cw, g, b, layer, alpha, seq, tm, tf, ns):
    m = x.shape[0]
    nc = D_FF // tf
    tiles_per_seq = seq // tm
    tail_spec = pl.BlockSpec((None, SUBLANES, tf), lambda i, c: (i, 0, c))
    tail_shape = jax.ShapeDtypeStruct((m // tm, SUBLANES, D_FF), F32)
    out, tail_g, tail_v = pl.pallas_call(
        functools.partial(_ffn_body, tm, tf, ns, tiles_per_seq, nc, alpha),
        grid=(m // tm, nc),
        in_specs=[pl.BlockSpec((tm, D_MODEL), lambda i, c: (i, 0), pipeline_mode=pl.Buffered(1)),
                  pl.BlockSpec((None, D_MODEL, 2 * tf), lambda i, c: (layer, 0, c)),
                  pl.BlockSpec((None, tf, D_MODEL), lambda i, c: (layer, c, 0)),
                  pl.BlockSpec((None, CONV_K, tf), lambda i, c: (layer, 0, c)),
                  pl.BlockSpec((None, CONV_K, tf), lambda i, c: (layer, 0, nc + c)),
                  pl.BlockSpec((None, 1, D_MODEL), lambda i, c: (layer, 0, 0)),
                  pl.BlockSpec((None, 1, D_MODEL), lambda i, c: (layer, 0, 0))],
        out_specs=[pl.BlockSpec((tm, D_MODEL), lambda i, c: (i, 0)), tail_spec, tail_spec],
        out_shape=[jax.ShapeDtypeStruct((m, D_MODEL), F32), tail_shape, tail_shape],
        scratch_shapes=[pltpu.VMEM((tm, D_MODEL), BF16),
                        pltpu.VMEM((ns, SUBLANES + tm // ns, 2 * tf), F32),
                        pltpu.VMEM((nc, SUBLANES, 2 * tf), F32)],
        compiler_params=_params("arbitrary", "arbitrary"),
        name="ffn_prompt",
    )(x, w_up_il, w_down, cw, cw, g, b)
    last = slice(tiles_per_seq - 1, None, tiles_per_seq)
    return out, jnp.concatenate([tail_g[last], tail_v[last]], -1)


def _gate_decode_body(ug_ref, uv_ref, sg_ref, sv_ref, cwg_ref, cwv_ref, o_ref):
    def conv(u_ref, s_ref, cw_ref):
        cw = cw_ref[...]
        y = s_ref[:, 0, :] * cw[0:1]
        y = y + s_ref[:, 1, :] * cw[1:2]
        return y + u_ref[...] * cw[2:3]

    hg = conv(ug_ref, sg_ref, cwg_ref)
    hv = conv(uv_ref, sv_ref, cwv_ref)
    o_ref[...] = (jax.nn.silu(hg) * hv).astype(BF16)


def _gate_decode(up, state, cw, layer, tf):
    b = up.shape[0]
    nc = D_FF // tf
    return pl.pallas_call(
        _gate_decode_body,
        grid=(nc,),
        in_specs=[pl.BlockSpec((b, tf), lambda c: (0, c)),
                  pl.BlockSpec((b, tf), lambda c: (0, nc + c)),
                  pl.BlockSpec((b, CONV_K - 1, tf), lambda c: (0, 0, c)),
                  pl.BlockSpec((b, CONV_K - 1, tf), lambda c: (0, 0, nc + c)),
                  pl.BlockSpec((None, CONV_K, tf), lambda c: (layer, 0, c)),
                  pl.BlockSpec((None, CONV_K, tf), lambda c: (layer, 0, nc + c))],
        out_specs=pl.BlockSpec((b, tf), lambda c: (0, c)),
        out_shape=jax.ShapeDtypeStruct((b, D_FF), BF16),
        compiler_params=_params("arbitrary"),
        name="gate_decode",
    )(up, up, state, state, cw, cw)


def kernel(x_prompt, x_sample, cache_k, cache_v, state_conv, state_pool, state_ffn, rel_table, w_in, conv_w,
           pool_w, pool_scale, sinks, w_o, ln1_g, ln1_b, w_up, ffn_conv_w, w_down, ln2_g, ln2_b):
    depth = w_in.shape[0]
    bp_, seq, _ = x_prompt.shape
    bs = x_sample.shape[0]
    assert x_sample.shape[1] == 1 and cache_k.shape[2] == WINDOW
    alpha = (2 * depth) ** 0.25
    nc = D_FF // FFN_TF

    w_in_b = _cast_w_in(w_in, 256)
    w_o_b = _cast_w_o(w_o)
    w_up_il = _cast_bf16(w_up, D_MODEL, FFN_TF, lambda j: jnp.where(j < nc, 2 * j, 2 * (j - nc) + 1))
    natural_col = lambda j: jnp.where(j % 2 == 0, j // 2, nc + j // 2)
    w_down_b = _cast_bf16(w_down, 512)
    pool_w_b = pool_w.astype(BF16)

    bias_by_dist = rel_table.astype(F32)[_t5_bucket(jnp.arange(WINDOW))]
    bias_p = _bias_table(bias_by_dist, WINDOW)
    bias_s = _bias_table(bias_by_dist, DECODE_ROWS)

    ln1_g, ln1_b, ln2_g, ln2_b = (t[:, None, :] for t in (ln1_g, ln1_b, ln2_g, ln2_b))

    xp = x_prompt.reshape(bp_ * seq, D_MODEL)
    xs = x_sample.reshape(bs, D_MODEL)

    outs_p, outs_s = [], []
    for l in range(depth):
        tab_p = (bias_p, _sink_table(sinks[l], WINDOW))
        tab_s = (bias_s, _sink_table(sinks[l], DECODE_ROWS))
        ps = pool_scale[l][None, :]

        z = _matmul(xp, w_in_b, l, 1024, IN_W // 2).reshape(bp_, seq, IN_W)
        x1, u_tail = _mixers_prompt(z, tab_p, conv_w[l], pool_w_b[l], ps, w_o_b, l,
                                    xp.reshape(bp_, seq, D_MODEL), ln1_g, ln1_b, alpha)
        x1 = x1.reshape(bp_ * seq, D_MODEL)
        xp, up_tail = _ffn_prompt(x1, w_up_il, w_down_b, ffn_conv_w, ln2_g, ln2_b, l, alpha, seq,
                                  FFN_TM, FFN_TF, FFN_NS)
        outs_p.append((
            z[:, seq - WINDOW:, ATTN_W:ATTN_W + KV_W].reshape(bp_, WINDOW, KV_HEADS, HEAD_DIM),
            z[:, seq - WINDOW:, ATTN_W + KV_W:ATTN_W + 2 * KV_W].reshape(bp_, WINDOW, KV_HEADS, HEAD_DIM),
            u_tail,
            z[:, seq - POOL_PREV:, IN_W - POOL_W:],
            up_tail[:, SUBLANES - (CONV_K - 1):],
        ))

        zs = _matmul(xs, w_in_b, l, bs, 512)
        ck = cache_k[l].reshape(bs, WINDOW, KV_W)
        cv = cache_v[l].reshape(bs, WINDOW, KV_W)
        cat_s, u_new, k_s, v_s = _mixers_decode(zs[:, None, :], ck, cv, state_conv[l], state_pool[l], tab_s,
                                                conv_w[l], pool_w_b[l], ps)
        x1s = _proj_ln(cat_s.reshape(bs, D_MODEL), w_o_b, l, xs, ln1_g, ln1_b, alpha, bs, D_MODEL)
        up_s = _matmul(x1s, w_up_il, l, bs, FFN_TF, natural_col)
        act_s = _gate_decode(up_s, state_ffn[l], ffn_conv_w, l, FFN_TF)
        xs = _proj_ln(act_s, w_down_b, l, x1s, ln2_g, ln2_b, alpha, bs, 512)
        outs_s.append((
            k_s.reshape(bs, WINDOW, KV_HEADS, HEAD_DIM),
            v_s.reshape(bs, WINDOW, KV_HEADS, HEAD_DIM),
            u_new,
            jnp.concatenate([state_pool[l][:, 1:], zs[:, None, IN_W - POOL_W:]], 1),
            jnp.concatenate([state_ffn[l][:, 1:], up_s[:, None, :]], 1),
        ))

    st = lambda lst, i: jnp.stack([e[i] for e in lst], 0)
    return (xp.reshape(bp_, seq, D_MODEL), xs.reshape(bs, 1, D_MODEL),
            st(outs_p, 0), st(outs_p, 1), st(outs_p, 2), st(outs_p, 3), st(outs_p, 4),
            st(outs_s, 0), st(outs_s, 1), st(outs_s, 2), st(outs_s, 3), st(outs_s, 4))
```

```python
import functools
import math

import jax
import jax.numpy as jnp
from jax import lax
from jax.experimental import pallas as pl
from jax.experimental.pallas import tpu as pltpu

D_MODEL = 2048
N_HEADS = 16
KV_HEADS = 4
HEAD_DIM = 64
GROUP = N_HEADS // KV_HEADS
ATTN_W = N_HEADS * HEAD_DIM
KV_W = KV_HEADS * HEAD_DIM
WINDOW = 128
ATTN_SCALE = HEAD_DIM ** -0.5
NEG_INF = -1e30
N_BUCKETS = 32
MAX_DISTANCE = 128
CONV_W = D_MODEL // 4
CONV_K = 3
POOL_W = D_MODEL // 4
POOL_WINDOWS = (2, 4, 8, 16)
POOL_GROUP = POOL_W // len(POOL_WINDOWS)
POOL_PREV = 15
IN_W = ATTN_W + 2 * KV_W + 3 * CONV_W + POOL_W
D_FF = 5632
PAST_LEN = 16384
LN_EPS = 1e-5

HALO = 16
SUBLANES = 8
DECODE_ROWS = 128 // GROUP
Q_BLK = 0
K_BLK = ATTN_W // KV_W
V_BLK = K_BLK + 1
GB_BLK = (ATTN_W + 2 * KV_W) // CONV_W
GC_BLK = GB_BLK + 1
H_BLK = GB_BLK + 2
PIN_BLK = GB_BLK + 3

FFN_TM = 1024
FFN_TF = 512
FFN_NS = 4

VMEM_LIMIT = 60 * 1024 * 1024

BF16 = jnp.bfloat16
F32 = jnp.float32


def _params(*sem, flags=None):
    return pltpu.CompilerParams(dimension_semantics=sem, vmem_limit_bytes=VMEM_LIMIT, flags=flags)


def _dot(a, b):
    return jnp.dot(a, b, preferred_element_type=F32)


def _layer_norm(y, g, b):
    mu = jnp.mean(y, -1, keepdims=True)
    yc = y - mu
    var = jnp.mean(yc * yc, -1, keepdims=True)
    return yc * lax.rsqrt(var + LN_EPS) * g + b


def _cast_body(w_ref, o_ref):
    o_ref[...] = w_ref[...].astype(BF16)


def _cast_bf16(w, rows, cols=None, col_perm=None, layers=None):
    depth, k, n = w.shape
    lo, hi = (0, depth) if layers is None else layers
    cols = n if cols is None else cols
    col_perm = (lambda j: j) if col_perm is None else col_perm
    return pl.pallas_call(
        _cast_body,
        grid=(hi - lo, k // rows, n // cols),
        in_specs=[pl.BlockSpec((None, rows, cols), lambda l, i, j: (l + lo, i, j))],
        out_specs=pl.BlockSpec((None, rows, cols), lambda l, i, j: (l, i, col_perm(j))),
        out_shape=jax.ShapeDtypeStruct((hi - lo, k, n), BF16),
        compiler_params=_params("arbitrary", "arbitrary", "arbitrary"),
        name="cast_bf16",
    )(w)


def _head_block_perm(blk):
    return (blk % GROUP) * KV_HEADS + blk // GROUP


def _cast_w_in_body(w_ref, o_ref):
    w = w_ref[...]
    order = sorted(range(N_HEADS), key=_head_block_perm)
    q = jnp.concatenate([w[:, h * HEAD_DIM:(h + 1) * HEAD_DIM] for h in order], axis=1)
    o_ref[...] = jnp.concatenate([q, w[:, ATTN_W:]], axis=1).astype(BF16)


def _cast_w_in(w, rows):
    depth, k, n = w.shape
    return pl.pallas_call(
        _cast_w_in_body,
        grid=(depth, k // rows),
        in_specs=[pl.BlockSpec((None, rows, n), lambda l, i: (l, i, 0))],
        out_specs=pl.BlockSpec((None, rows, n), lambda l, i: (l, i, 0)),
        out_shape=jax.ShapeDtypeStruct(w.shape, BF16),
        compiler_params=_params("arbitrary", "arbitrary"),
        name="cast_w_in",
    )(w)


def _cast_w_o_body(w_ref, o_ref):
    @pl.when(pl.program_id(1) == 0)
    def _():
        w = w_ref[...]
        order = sorted(range(N_HEADS), key=_head_block_perm)
        o_ref[...] = jnp.concatenate([w[h * HEAD_DIM:(h + 1) * HEAD_DIM] for h in order], axis=0).astype(BF16)

    @pl.when(pl.program_id(1) > 0)
    def _():
        o_ref[...] = w_ref[...].astype(BF16)


def _cast_w_o(w):
    depth, k, n = w.shape
    return pl.pallas_call(
        _cast_w_o_body,
        grid=(depth, k // ATTN_W),
        in_specs=[pl.BlockSpec((None, ATTN_W, n), lambda l, i: (l, i, 0))],
        out_specs=pl.BlockSpec((None, ATTN_W, n), lambda l, i: (l, i, 0)),
        out_shape=jax.ShapeDtypeStruct(w.shape, BF16),
        compiler_params=_params("arbitrary", "arbitrary"),
        name="cast_w_o",
    )(w)


def _matmul_body(x_ref, w_ref, o_ref, xb_ref):
    @pl.when(pl.program_id(1) == 0)
    def _():
        xb_ref[...] = x_ref[...].astype(BF16)

    o_ref[...] = _dot(xb_ref[...], w_ref[...])


def _matmul(x, w, layer, tm, tn, out_col_perm=None):
    m, k = x.shape
    n = w.shape[2]
    out_col_perm = (lambda j: j) if out_col_perm is None else out_col_perm
    return pl.pallas_call(
        _matmul_body,
        grid=(m // tm, n // tn),
        in_specs=[pl.BlockSpec((tm, k), lambda i, j: (i, 0)),
                  pl.BlockSpec((None, k, tn), lambda i, j: (layer, 0, j))],
        out_specs=pl.BlockSpec((tm, tn), lambda i, j: (i, out_col_perm(j))),
        out_shape=jax.ShapeDtypeStruct((m, n), F32),
        scratch_shapes=[pltpu.VMEM((tm, k), BF16)],
        compiler_params=_params("arbitrary", "arbitrary"),
        name="matmul",
    )(x, w)


def _mixers_body(decode, tq, new_rows, pos0, alpha, cast_tf, *refs):
    if decode:
        (q_ref, kc_ref, vc_ref, kp_ref, vp_ref, gb_ref, gc_ref, h_ref, pin_ref, us_ref, ps_state_ref,
         bias_ref, sink_ref, cw_ref, pw_ref, ps_ref,
         cat_ref, uo_ref, ko_ref, vo_ref, ext_u, ext_p) = refs
    else:
        (q_ref, kc_ref, vc_ref, kp_ref, vp_ref, gb_ref, gc_ref, h_ref, pin_ref,
         gcp_ref, hp_ref, pinp_ref,
         bias_ref, sink_ref, cw_ref, pw_ref, ps_ref, wo_ref, x_ref, lng_ref, lnb_ref) = refs[:21]
        x1_ref, uo_ref = refs[-4:-2] if cast_tf is None else refs[-6:-4]
        ext_u, ext_p = refs[-2:]
        if cast_tf is not None:
            wu_src, wd_src = refs[21:23]
            wu_dst, wd_dst = refs[-4:-2]
            nc = D_FF // cast_tf
            wu = wu_src[...]
            halves = [wu[:, (j % 2) * D_FF + (j // 2) * cast_tf:(j % 2) * D_FF + (j // 2 + 1) * cast_tf]
                      for j in range(2 * nc)]
            wu_dst[...] = jnp.concatenate(halves, axis=1).astype(BF16)
            wd_dst[...] = wd_src[...].astype(BF16)
    n = pl.program_id(1)
    rows = GROUP * tq

    def new_rows_of(ref, nrows=tq):
        x = ref[...]
        if x.shape[0] == nrows:
            return x
        real = lax.broadcasted_iota(jnp.int32, (nrows, x.shape[1]), 0) < x.shape[0]
        return jnp.where(real, jnp.broadcast_to(x, (nrows, x.shape[1])), 0.0)

    nk = 2 * WINDOW
    q = new_rows_of(q_ref)
    qs = jnp.concatenate([q[:, g * KV_W:(g + 1) * KV_W] for g in range(GROUP)], axis=0)
    qs = (qs * ATTN_SCALE).astype(BF16)
    k_all = jnp.concatenate([kp_ref[...], new_rows_of(kc_ref, WINDOW)], axis=0).astype(BF16)
    v_all = jnp.concatenate([vp_ref[...], new_rows_of(vc_ref, WINDOW)], axis=0)
    if decode:
        for cache_ref, new_ref, out_ref in ((kp_ref, kc_ref, ko_ref), (vp_ref, vc_ref, vo_ref)):
            out_ref[0:WINDOW - new_rows] = cache_ref[new_rows:WINDOW]
            out_ref[WINDOW - new_rows:WINDOW] = new_ref[...]
    lane_head = lax.broadcasted_iota(jnp.int32, (nk, KV_W), 1) // HEAD_DIM
    k_cat = jnp.concatenate([jnp.where(lane_head == kvh, k_all, jnp.zeros((), BF16)) for kvh in range(KV_HEADS)],
                            axis=0)
    st_all = lax.dot_general(k_cat, qs, (((1,), (1,)), ((), ())), preferred_element_type=F32)
    kj = lax.broadcasted_iota(jnp.int32, (nk, rows), 0)
    qi = lax.broadcasted_iota(jnp.int32, (nk, rows), 1) & (tq - 1)
    mask = jnp.logical_and(kj > qi, kj <= qi + WINDOW)
    if not decode:
        mask = jnp.logical_and(mask, jnp.logical_or(n > 0, kj >= WINDOW))
    probs = []
    for kvh in range(KV_HEADS):
        s = jnp.where(mask, st_all[kvh * nk:(kvh + 1) * nk] + bias_ref[kvh], NEG_INF)
        sink = sink_ref[kvh]
        m = jnp.maximum(jnp.max(s, 0, keepdims=True), sink)
        p = jnp.exp(s - m)
        den = jnp.sum(p, 0, keepdims=True) + jnp.exp(sink - m)
        probs.append((p * (1.0 / den)).astype(BF16))
    pt_all = jnp.concatenate(probs, axis=0)
    vt = v_all.T
    row_head = lax.broadcasted_iota(jnp.int32, (KV_W, KV_HEADS * nk), 0) // HEAD_DIM
    col_head = lax.broadcasted_iota(jnp.int32, (KV_W, KV_HEADS * nk), 1) // nk
    vt_cat = jnp.where(row_head == col_head, jnp.concatenate([vt] * KV_HEADS, axis=1), 0.0).astype(BF16)
    o = _dot(vt_cat, pt_all).T
    attn = jnp.concatenate([o[g * tq:(g + 1) * tq] for g in range(GROUP)], axis=1)

    u = new_rows_of(gc_ref) * new_rows_of(h_ref)
    if decode:
        ext_u[HALO - (CONV_K - 1):HALO] = us_ref[...]
    else:
        ext_u[0:HALO] = jnp.where(n > 0, gcp_ref[...] * hp_ref[...], 0.0)
    ext_u[HALO:HALO + tq] = u
    cw = cw_ref[...]
    conv = ext_u[HALO - 2:HALO - 2 + tq] * cw[0:1]
    conv = conv + ext_u[HALO - 1:HALO - 1 + tq] * cw[1:2]
    conv = conv + u * cw[2:3]
    c = new_rows_of(gb_ref) * conv
    uo_ref[...] = ext_u[new_rows + HALO - (CONV_K - 1):new_rows + HALO]

    pin = new_rows_of(pin_ref)
    if decode:
        ext_p[HALO - POOL_PREV:HALO] = ps_state_ref[...]
    else:
        ext_p[0:HALO] = jnp.where(n > 0, pinp_ref[...], 0.0)
    ext_p[HALO:HALO + tq] = pin
    pos = pos0 + n * tq + lax.broadcasted_iota(jnp.int32, (tq, 1), 0)
    pooled = []
    for g, w in enumerate(POOL_WINDOWS):
        lo = g * POOL_GROUP
        cur = pin[:, lo:lo + POOL_GROUP]
        win = ext_p[HALO - (w - 1):HALO - (w - 1) + tq, lo:lo + POOL_GROUP]
        for k in range(w - 2, -1, -1):
            win = win + ext_p[HALO - k:HALO - k + tq, lo:lo + POOL_GROUP]
        cnt = jnp.minimum(pos + 1, w).astype(F32)
        d = (win / cnt - cur).astype(BF16)
        pooled.append(_dot(d, pw_ref[g]))
    pm = jnp.concatenate(pooled, axis=1) * ps_ref[...]

    cat = jnp.concatenate([attn, c, pm], axis=1).astype(BF16)
    if decode:
        cat_ref[...] = cat[0:new_rows]
    else:
        x1_ref[...] = _layer_norm(alpha * x_ref[...] + _dot(cat, wo_ref[...]), lng_ref[...], lnb_ref[...])


def _mixers_prompt(z, tables, cw, pw, ps, w_o, layer, x, ln_g, ln_b, alpha, cast_next=None):
    b, s, _ = z.shape
    tq = WINDOW
    nb = s // tq
    bias, sink = tables
    cur = lambda blk: (lambda i, n: (i, n, blk))
    prev = lambda blk: (lambda i, n: (i, jnp.maximum(n - 1, 0), blk))
    halo = lambda blk: (lambda i, n: (i, jnp.maximum(n * (tq // HALO) - 1, 0), blk))
    const3 = lambda i, n: (0, 0, 0)
    const2 = lambda i, n: (0, 0)
    in_specs = [
        pl.BlockSpec((None, tq, ATTN_W), cur(Q_BLK)),
        pl.BlockSpec((None, tq, KV_W), cur(K_BLK)),
        pl.BlockSpec((None, tq, KV_W), cur(V_BLK)),
        pl.BlockSpec((None, tq, KV_W), prev(K_BLK)),
        pl.BlockSpec((None, tq, KV_W), prev(V_BLK)),
        pl.BlockSpec((None, tq, CONV_W), cur(GB_BLK)),
        pl.BlockSpec((None, tq, CONV_W), cur(GC_BLK)),
        pl.BlockSpec((None, tq, CONV_W), cur(H_BLK)),
        pl.BlockSpec((None, tq, POOL_W), cur(PIN_BLK)),
        pl.BlockSpec((None, HALO, CONV_W), halo(GC_BLK)),
        pl.BlockSpec((None, HALO, CONV_W), halo(H_BLK)),
        pl.BlockSpec((None, HALO, POOL_W), halo(PIN_BLK)),
        pl.BlockSpec(bias.shape, const3),
        pl.BlockSpec(sink.shape, const3),
        pl.BlockSpec(cw.shape, const2),
        pl.BlockSpec(pw.shape, const3),
        pl.BlockSpec(ps.shape, const2),
        pl.BlockSpec((None, D_MODEL, D_MODEL), lambda i, n: (layer, 0, 0), pipeline_mode=pl.Buffered(1)),
        pl.BlockSpec((None, tq, D_MODEL), lambda i, n: (i, n, 0)),
        pl.BlockSpec((None, 1, D_MODEL), lambda i, n: (layer, 0, 0)),
        pl.BlockSpec((None, 1, D_MODEL), lambda i, n: (layer, 0, 0)),
    ]
    operands = [z] * 12 + [bias, sink, cw, pw, ps, w_o, x, ln_g, ln_b]
    out_specs = [pl.BlockSpec((None, tq, D_MODEL), lambda i, n: (i, n, 0)),
                 pl.BlockSpec((None, CONV_K - 1, CONV_W), lambda i, n: (i, 0, 0))]
    out_shape = [jax.ShapeDtypeStruct((b, s, D_MODEL), F32),
                 jax.ShapeDtypeStruct((b, CONV_K - 1, CONV_W), F32)]
    cast_tf = None
    if cast_next is not None:
        w_up, w_down, cast_layer, cast_tf = cast_next
        for w in (w_up, w_down):
            _, k, width = w.shape
            rows = next(r for r in range(2 * SUBLANES, k + 1, 2 * SUBLANES) if k % r == 0 and k // r <= b * nb)
            nblocks = k // rows
            index = lambda i, n, nblocks=nblocks: (jnp.minimum(i * nb + n, nblocks - 1), 0)
            in_specs.append(pl.BlockSpec((None, rows, width), lambda i, n, index=index: (cast_layer,) + index(i, n)))
            operands.append(w)
            out_specs.append(pl.BlockSpec((None, rows, width), lambda i, n, index=index: (0,) + index(i, n)))
            out_shape.append(jax.ShapeDtypeStruct((1, k, width), BF16))
    return pl.pallas_call(
        functools.partial(_mixers_body, False, tq, tq, 0, alpha, cast_tf),
        grid=(b, nb),
        in_specs=in_specs,
        out_specs=out_specs,
        out_shape=out_shape,
        scratch_shapes=[pltpu.VMEM((HALO + tq, CONV_W), F32), pltpu.VMEM((HALO + tq, POOL_W), F32)],
        compiler_params=_params("arbitrary", "arbitrary"),
        name="mixers_prompt",
    )(*operands)


def _mixers_decode(zs, ck, cv, conv_state, pool_state, tables, cw, pw, ps):
    b = zs.shape[0]
    tq = DECODE_ROWS
    bias, sink = tables
    new = lambda width, blk: pl.BlockSpec((None, 1, width), lambda i, n: (i, 0, blk))
    whole = lambda rows, width: pl.BlockSpec((None, rows, width), lambda i, n: (i, 0, 0))
    const3 = lambda i, n: (0, 0, 0)
    const2 = lambda i, n: (0, 0)
    in_specs = [
        new(ATTN_W, Q_BLK), new(KV_W, K_BLK), new(KV_W, V_BLK),
        whole(WINDOW, KV_W), whole(WINDOW, KV_W),
        new(CONV_W, GB_BLK), new(CONV_W, GC_BLK), new(CONV_W, H_BLK), new(POOL_W, PIN_BLK),
        whole(CONV_K - 1, CONV_W), whole(POOL_PREV, POOL_W),
        pl.BlockSpec(bias.shape, const3),
        pl.BlockSpec(sink.shape, const3),
        pl.BlockSpec(cw.shape, const2),
        pl.BlockSpec(pw.shape, const3),
        pl.BlockSpec(ps.shape, const2),
    ]
    return pl.pallas_call(
        functools.partial(_mixers_body, True, tq, 1, PAST_LEN, None, None),
        grid=(b, 1),
        in_specs=in_specs,
        out_specs=[whole(1, D_MODEL), whole(CONV_K - 1, CONV_W), whole(WINDOW, KV_W), whole(WINDOW, KV_W)],
        out_shape=[jax.ShapeDtypeStruct((b, 1, D_MODEL), BF16),
                   jax.ShapeDtypeStruct((b, CONV_K - 1, CONV_W), F32),
                   jax.ShapeDtypeStruct((b, WINDOW, KV_W), F32),
                   jax.ShapeDtypeStruct((b, WINDOW, KV_W), F32)],
        scratch_shapes=[pltpu.VMEM((HALO + tq, CONV_W), F32), pltpu.VMEM((HALO + tq, POOL_W), F32)],
        compiler_params=_params("arbitrary", "arbitrary"),
        name="mixers_decode",
    )(zs, zs, zs, ck, cv, zs, zs, zs, zs, conv_state, pool_state, bias, sink, cw, pw, ps)


def _bias_table(bias_by_dist, tq):
    nk = 2 * WINDOW
    by_offset = jnp.zeros((N_HEADS, nk + 1), F32).at[:, 1:WINDOW + 1].set(bias_by_dist[::-1].T)
    rows = jnp.tile(by_offset, (1, tq))[:, :tq * nk].reshape(N_HEADS, tq, nk)
    return jnp.swapaxes(rows.reshape(KV_HEADS, GROUP * tq, nk), 1, 2)


def _sink_table(sinks_l, tq):
    return jnp.broadcast_to(sinks_l.astype(F32).reshape(KV_HEADS, GROUP, 1, 1),
                            (KV_HEADS, GROUP, 1, tq)).reshape(KV_HEADS, 1, GROUP * tq)


def _t5_bucket(n):
    max_exact = N_BUCKETS // 2
    nf = jnp.maximum(n, 1).astype(F32)
    large = max_exact + (jnp.log(nf / max_exact) / math.log(MAX_DISTANCE / max_exact)
                         * (N_BUCKETS - max_exact)).astype(jnp.int32)
    large = jnp.minimum(large, N_BUCKETS - 1)
    return jnp.where(n < max_exact, n, large)


def _proj_ln_body(nk, alpha, a_ref, w_ref, x_ref, g_ref, b_ref, o_ref, *scratch):
    part = _dot(a_ref[...], w_ref[...])
    if nk == 1:
        o_ref[...] = _layer_norm(alpha * x_ref[...] + part, g_ref[...], b_ref[...])
        return
    acc_ref, = scratch
    k = pl.program_id(1)

    @pl.when(k == 0)
    def _():
        acc_ref[...] = jnp.zeros_like(acc_ref)

    acc_ref[...] += part

    @pl.when(k == nk - 1)
    def _():
        o_ref[...] = _layer_norm(alpha * x_ref[...] + acc_ref[...], g_ref[...], b_ref[...])


def _proj_ln(a, w, w_layer, x, g, b, layer, alpha, tm, tk):
    m, k = a.shape
    n = w.shape[2]
    nk = k // tk
    return pl.pallas_call(
        functools.partial(_proj_ln_body, nk, alpha),
        grid=(m // tm, nk),
        in_specs=[pl.BlockSpec((tm, tk), lambda i, kk: (i, kk)),
                  pl.BlockSpec((None, tk, n), lambda i, kk: (w_layer, kk, 0)),
                  pl.BlockSpec((tm, n), lambda i, kk: (i, 0)),
                  pl.BlockSpec((None, 1, n), lambda i, kk: (layer, 0, 0)),
                  pl.BlockSpec((None, 1, n), lambda i, kk: (layer, 0, 0))],
        out_specs=pl.BlockSpec((tm, n), lambda i, kk: (i, 0)),
        out_shape=jax.ShapeDtypeStruct((m, n), F32),
        scratch_shapes=[] if nk == 1 else [pltpu.VMEM((tm, n), F32)],
        compiler_params=_params("arbitrary", "arbitrary"),
        name="proj_ln",
    )(a, w, x, g, b)


def _ffn_body(tm, tf, ns, tiles_per_seq, nc, alpha, x_ref, wu_ref, wd_ref, cwg_ref, cwv_ref, g_ref, b_ref,
              o_ref, tail_g_ref, tail_v_ref, xb_ref, ext, carry):
    i = pl.program_id(0)
    c = pl.program_id(1)
    ts = tm // ns

    @pl.when(c == 0)
    def _():
        xb_ref[...] = x_ref[...].astype(BF16)
        o_ref[...] = jnp.zeros_like(o_ref)

    seq_start = (i % tiles_per_seq) == 0

    def up_proj(h, prev_tail):
        up = _dot(xb_ref[pl.ds(h * ts, ts)], wu_ref[...])
        if h == 0:
            @pl.when(seq_start)
            def _():
                ext[0, 0:SUBLANES] = jnp.zeros((SUBLANES, 2 * tf), F32)

            @pl.when(jnp.logical_not(seq_start))
            def _():
                ext[0, 0:SUBLANES] = carry[c]
        else:
            ext[h, 0:SUBLANES] = prev_tail
        ext[h, SUBLANES:SUBLANES + ts] = up
        tail = up[ts - SUBLANES:ts]
        if h == ns - 1:
            carry[c] = tail
            tail_g_ref[...] = tail[:, :tf]
            tail_v_ref[...] = tail[:, tf:]
        return tail

    def gate(h):
        cw = jnp.concatenate([cwg_ref[...], cwv_ref[...]], axis=1)
        hc = ext[h, SUBLANES - 2:SUBLANES - 2 + ts] * cw[0:1]
        hc = hc + ext[h, SUBLANES - 1:SUBLANES - 1 + ts] * cw[1:2]
        hc = hc + ext[h, SUBLANES:SUBLANES + ts] * cw[2:3]
        return (jax.nn.silu(hc[:, :tf]) * hc[:, tf:]).astype(BF16)

    def down_proj(h, act):
        o_ref[pl.ds(h * ts, ts)] += _dot(act, wd_ref[...])

    tail = up_proj(0, None)
    for h in range(ns):
        act = gate(h)
        if h + 1 < ns:
            tail = up_proj(h + 1, tail)
        down_proj(h, act)

    @pl.when(c == nc - 1)
    def _():
        o_ref[...] = _layer_norm(alpha * x_ref[...] + o_ref[...], g_ref[...], b_ref[...])


def _ffn_prompt(x, w_up_il, w_down, w_layer, cw, g, b, layer, alpha, seq, tm, tf, ns):
    m = x.shape[0]
    nc = D_FF // tf
    tiles_per_seq = seq // tm
    tail_spec = pl.BlockSpec((None, SUBLANES, tf), lambda i, c: (i, 0, c))
    tail_shape = jax.ShapeDtypeStruct((m // tm, SUBLANES, D_FF), F32)
    out, tail_g, tail_v = pl.pallas_call(
        functools.partial(_ffn_body, tm, tf, ns, tiles_per_seq, nc, alpha),
        grid=(m // tm, nc),
        in_specs=[pl.BlockSpec((tm, D_MODEL), lambda i, c: (i, 0), pipeline_mode=pl.Buffered(1)),
                  pl.BlockSpec((None, D_MODEL, 2 * tf), lambda i, c: (w_layer, 0, c)),
                  pl.BlockSpec((None, tf, D_MODEL), lambda i, c: (w_layer, c, 0)),
                  pl.BlockSpec((None, CONV_K, tf), lambda i, c: (layer, 0, c)),
                  pl.BlockSpec((None, CONV_K, tf), lambda i, c: (layer, 0, nc + c)),
                  pl.BlockSpec((None, 1, D_MODEL), lambda i, c: (layer, 0, 0)),
                  pl.BlockSpec((None, 1, D_MODEL), lambda i, c: (layer, 0, 0))],
        out_specs=[pl.BlockSpec((tm, D_MODEL), lambda i, c: (i, 0)), tail_spec, tail_spec],
        out_shape=[jax.ShapeDtypeStruct((m, D_MODEL), F32), tail_shape, tail_shape],
        scratch_shapes=[pltpu.VMEM((tm, D_MODEL), BF16),
                        pltpu.VMEM((ns, SUBLANES + tm // ns, 2 * tf), F32),
                        pltpu.VMEM((nc, SUBLANES, 2 * tf), F32)],
        compiler_params=_params("arbitrary", "arbitrary"),
        name="ffn_prompt",
    )(x, w_up_il, w_down, cw, cw, g, b)
    last = slice(tiles_per_seq - 1, None, tiles_per_seq)
    return out, jnp.concatenate([tail_g[last], tail_v[last]], -1)


def _gate_decode_body(ug_ref, uv_ref, sg_ref, sv_ref, cwg_ref, cwv_ref, o_ref):
    def conv(u_ref, s_ref, cw_ref):
        cw = cw_ref[...]
        y = s_ref[:, 0, :] * cw[0:1]
        y = y + s_ref[:, 1, :] * cw[1:2]
        return y + u_ref[...] * cw[2:3]

    hg = conv(ug_ref, sg_ref, cwg_ref)
    hv = conv(uv_ref, sv_ref, cwv_ref)
    o_ref[...] = (jax.nn.silu(hg) * hv).astype(BF16)


def _gate_decode(up, state, cw, layer, tf):
    b = up.shape[0]
    nc = D_FF // tf
    return pl.pallas_call(
        _gate_decode_body,
        grid=(nc,),
        in_specs=[pl.BlockSpec((b, tf), lambda c: (0, c)),
                  pl.BlockSpec((b, tf), lambda c: (0, nc + c)),
                  pl.BlockSpec((b, CONV_K - 1, tf), lambda c: (0, 0, c)),
                  pl.BlockSpec((b, CONV_K - 1, tf), lambda c: (0, 0, nc + c)),
                  pl.BlockSpec((None, CONV_K, tf), lambda c: (layer, 0, c)),
                  pl.BlockSpec((None, CONV_K, tf), lambda c: (layer, 0, nc + c))],
        out_specs=pl.BlockSpec((b, tf), lambda c: (0, c)),
        out_shape=jax.ShapeDtypeStruct((b, D_FF), BF16),
        compiler_params=_params("arbitrary"),
        name="gate_decode",
    )(up, up, state, state, cw, cw)


def kernel(x_prompt, x_sample, cache_k, cache_v, state_conv, state_pool, state_ffn, rel_table, w_in, conv_w,
           pool_w, pool_scale, sinks, w_o, ln1_g, ln1_b, w_up, ffn_conv_w, w_down, ln2_g, ln2_b):
    depth = w_in.shape[0]
    bp_, seq, _ = x_prompt.shape
    bs = x_sample.shape[0]
    assert x_sample.shape[1] == 1 and cache_k.shape[2] == WINDOW
    alpha = (2 * depth) ** 0.25
    nc = D_FF // FFN_TF

    w_in_b = _cast_w_in(w_in, 256)
    w_o_b = _cast_w_o(w_o)
    w_up_il = [_cast_bf16(w_up, D_MODEL, FFN_TF, lambda j: jnp.where(j < nc, 2 * j, 2 * (j - nc) + 1), (0, 1))]
    w_down_b = [_cast_bf16(w_down, 512, layers=(0, 1))]
    natural_col = lambda j: jnp.where(j % 2 == 0, j // 2, nc + j // 2)
    pool_w_b = pool_w.astype(BF16)

    bias_by_dist = rel_table.astype(F32)[_t5_bucket(jnp.arange(WINDOW))]
    bias_p = _bias_table(bias_by_dist, WINDOW)
    bias_s = _bias_table(bias_by_dist, DECODE_ROWS)

    ln1_g, ln1_b, ln2_g, ln2_b = (t[:, None, :] for t in (ln1_g, ln1_b, ln2_g, ln2_b))

    xp = x_prompt.reshape(bp_ * seq, D_MODEL)
    xs = x_sample.reshape(bs, D_MODEL)

    outs_p, outs_s = [], []
    for l in range(depth):
        tab_p = (bias_p, _sink_table(sinks[l], WINDOW))
        tab_s = (bias_s, _sink_table(sinks[l], DECODE_ROWS))
        ps = pool_scale[l][None, :]

        z = _matmul(xp, w_in_b, l, 1024, IN_W // 2).reshape(bp_, seq, IN_W)
        cast_next = (w_up, w_down, l + 1, FFN_TF) if l + 1 < depth else None
        x1, u_tail, *next_w = _mixers_prompt(z, tab_p, conv_w[l], pool_w_b[l], ps, w_o_b, l,
                                             xp.reshape(bp_, seq, D_MODEL), ln1_g, ln1_b, alpha, cast_next)
        if next_w:
            w_up_il.append(next_w[0])
            w_down_b.append(next_w[1])
        x1 = x1.reshape(bp_ * seq, D_MODEL)
        xp, up_tail = _ffn_prompt(x1, w_up_il[l], w_down_b[l], 0, ffn_conv_w, ln2_g, ln2_b, l, alpha, seq,
                                  FFN_TM, FFN_TF, FFN_NS)
        outs_p.append((
            z[:, seq - WINDOW:, ATTN_W:ATTN_W + KV_W].reshape(bp_, WINDOW, KV_HEADS, HEAD_DIM),
            z[:, seq - WINDOW:, ATTN_W + KV_W:ATTN_W + 2 * KV_W].reshape(bp_, WINDOW, KV_HEADS, HEAD_DIM),
            u_tail,
            z[:, seq - POOL_PREV:, IN_W - POOL_W:],
            up_tail[:, SUBLANES - (CONV_K - 1):],
        ))

        zs = _matmul(xs, w_in_b, l, bs, 512)
        ck = cache_k[l].reshape(bs, WINDOW, KV_W)
        cv = cache_v[l].reshape(bs, WINDOW, KV_W)
        cat_s, u_new, k_s, v_s = _mixers_decode(zs[:, None, :], ck, cv, state_conv[l], state_pool[l], tab_s,
                                                conv_w[l], pool_w_b[l], ps)
        x1s = _proj_ln(cat_s.reshape(bs, D_MODEL), w_o_b, l, xs, ln1_g, ln1_b, l, alpha, bs, D_MODEL)
        up_s = _matmul(x1s, w_up_il[l], 0, bs, FFN_TF, natural_col)
        act_s = _gate_decode(up_s, state_ffn[l], ffn_conv_w, l, FFN_TF)
        xs = _proj_ln(act_s, w_down_b[l], 0, x1s, ln2_g, ln2_b, l, alpha, bs, 512)
        outs_s.append((
            k_s.reshape(bs, WINDOW, KV_HEADS, HEAD_DIM),
            v_s.reshape(bs, WINDOW, KV_HEADS, HEAD_DIM),
            u_new,
            jnp.concatenate([state_pool[l][:, 1:], zs[:, None, IN_W - POOL_W:]], 1),
            jnp.concatenate([state_ffn[l][:, 1:], up_s[:, None, :]], 1),
        ))

    st = lambda lst, i: jnp.stack([e[i] for e in lst], 0)
    return (xp.reshape(bp_, seq, D_MODEL), xs.reshape(bs, 1, D_MODEL),
            st(outs_p, 0), st(outs_p, 1), st(outs_p, 2), st(outs_p, 3), st(outs_p, 4),
            st(outs_s, 0), st(outs_s, 1), st(outs_s, 2), st(outs_s, 3), st(outs_s, 4))
```

```python
import functools
import math

import jax
import jax.numpy as jnp
from jax import lax
from jax.experimental import pallas as pl
from jax.experimental.pallas import tpu as pltpu

D_MODEL = 2048
N_HEADS = 16
KV_HEADS = 4
HEAD_DIM = 64
GROUP = N_HEADS // KV_HEADS
ATTN_W = N_HEADS * HEAD_DIM
KV_W = KV_HEADS * HEAD_DIM
WINDOW = 128
ATTN_SCALE = HEAD_DIM ** -0.5
NEG_INF = -1e30
N_BUCKETS = 32
MAX_DISTANCE = 128
CONV_W = D_MODEL // 4
CONV_K = 3
POOL_W = D_MODEL // 4
POOL_WINDOWS = (2, 4, 8, 16)
POOL_GROUP = POOL_W // len(POOL_WINDOWS)
POOL_PREV = 15
IN_W = ATTN_W + 2 * KV_W + 3 * CONV_W + POOL_W
D_FF = 5632
PAST_LEN = 16384
LN_EPS = 1e-5

HALO = 16
SUBLANES = 8
DECODE_ROWS = 128 // GROUP
Q_BLK = 0
K_BLK = ATTN_W // KV_W
V_BLK = K_BLK + 1
GB_BLK = (ATTN_W + 2 * KV_W) // CONV_W
GC_BLK = GB_BLK + 1
H_BLK = GB_BLK + 2
PIN_BLK = GB_BLK + 3

FFN_TM = 1024
FFN_TF = 512
FFN_NS = 4

VMEM_LIMIT = 60 * 1024 * 1024

BF16 = jnp.bfloat16
F32 = jnp.float32


def _params(*sem, flags=None):
    return pltpu.CompilerParams(dimension_semantics=sem, vmem_limit_bytes=VMEM_LIMIT, flags=flags)


def _dot(a, b):
    return jnp.dot(a, b, preferred_element_type=F32)


def _layer_norm(y, g, b):
    mu = jnp.mean(y, -1, keepdims=True)
    yc = y - mu
    var = jnp.mean(yc * yc, -1, keepdims=True)
    return yc * lax.rsqrt(var + LN_EPS) * g + b


def _head_block_perm(blk):
    return (blk % GROUP) * KV_HEADS + blk // GROUP


def _cast_w_in_body(w_ref, o_ref):
    w = w_ref[...]
    order = sorted(range(N_HEADS), key=_head_block_perm)
    q = jnp.concatenate([w[:, h * HEAD_DIM:(h + 1) * HEAD_DIM] for h in order], axis=1)
    o_ref[...] = jnp.concatenate([q, w[:, ATTN_W:]], axis=1).astype(BF16)


def _cast_w_in(w, rows):
    depth, k, n = w.shape
    return pl.pallas_call(
        _cast_w_in_body,
        grid=(depth, k // rows),
        in_specs=[pl.BlockSpec((None, rows, n), lambda l, i: (l, i, 0))],
        out_specs=pl.BlockSpec((None, rows, n), lambda l, i: (l, i, 0)),
        out_shape=jax.ShapeDtypeStruct(w.shape, BF16),
        compiler_params=_params("arbitrary", "arbitrary"),
        name="cast_w_in",
    )(w)


def _cast_w_o_body(w_ref, o_ref):
    @pl.when(pl.program_id(1) == 0)
    def _():
        w = w_ref[...]
        order = sorted(range(N_HEADS), key=_head_block_perm)
        o_ref[...] = jnp.concatenate([w[h * HEAD_DIM:(h + 1) * HEAD_DIM] for h in order], axis=0).astype(BF16)

    @pl.when(pl.program_id(1) > 0)
    def _():
        o_ref[...] = w_ref[...].astype(BF16)


def _cast_w_o(w):
    depth, k, n = w.shape
    return pl.pallas_call(
        _cast_w_o_body,
        grid=(depth, k // ATTN_W),
        in_specs=[pl.BlockSpec((None, ATTN_W, n), lambda l, i: (l, i, 0))],
        out_specs=pl.BlockSpec((None, ATTN_W, n), lambda l, i: (l, i, 0)),
        out_shape=jax.ShapeDtypeStruct(w.shape, BF16),
        compiler_params=_params("arbitrary", "arbitrary"),
        name="cast_w_o",
    )(w)


def _matmul_body(x_ref, w_ref, o_ref, xb_ref):
    @pl.when(pl.program_id(1) == 0)
    def _():
        xb_ref[...] = x_ref[...].astype(BF16)

    o_ref[...] = _dot(xb_ref[...], w_ref[...])


def _matmul(x, w, layer, tm, tn, out_col_perm=None):
    m, k = x.shape
    n = w.shape[2]
    out_col_perm = (lambda j: j) if out_col_perm is None else out_col_perm
    return pl.pallas_call(
        _matmul_body,
        grid=(m // tm, n // tn),
        in_specs=[pl.BlockSpec((tm, k), lambda i, j: (i, 0)),
                  pl.BlockSpec((None, k, tn), lambda i, j: (layer, 0, j))],
        out_specs=pl.BlockSpec((tm, tn), lambda i, j: (i, out_col_perm(j))),
        out_shape=jax.ShapeDtypeStruct((m, n), F32),
        scratch_shapes=[pltpu.VMEM((tm, k), BF16)],
        compiler_params=_params("arbitrary", "arbitrary"),
        name="matmul",
    )(x, w)


def _mixers_body(decode, tq, new_rows, pos0, alpha, cast_tf, *refs):
    if decode:
        (q_ref, kc_ref, vc_ref, kp_ref, vp_ref, gb_ref, gc_ref, h_ref, pin_ref, us_ref, ps_state_ref,
         bias_ref, sink_ref, cw_ref, pw_ref, ps_ref,
         cat_ref, uo_ref, ko_ref, vo_ref, ext_u, ext_p) = refs
    else:
        (q_ref, kc_ref, vc_ref, kp_ref, vp_ref, gb_ref, gc_ref, h_ref, pin_ref,
         gcp_ref, hp_ref, pinp_ref,
         bias_ref, sink_ref, cw_ref, pw_ref, ps_ref, wo_ref, x_ref, lng_ref, lnb_ref) = refs[:21]
        ext_u, ext_p = refs[-2:]
        cast_src = refs[21:-4 - (len(refs) - 25) // 2]
        x1_ref, uo_ref = refs[21 + len(cast_src):23 + len(cast_src)]
        cast_dst = refs[23 + len(cast_src):-2]
        for k, (src, dst) in enumerate(zip(cast_src, cast_dst)):
            w = src[...]
            if k % 2 == 0:
                w = jnp.concatenate(
                    [w[:, (j % 2) * D_FF + (j // 2) * cast_tf:(j % 2) * D_FF + (j // 2 + 1) * cast_tf]
                     for j in range(2 * (D_FF // cast_tf))], axis=1)
            dst[...] = w.astype(BF16)
    n = pl.program_id(1)
    rows = GROUP * tq

    def new_rows_of(ref, nrows=tq):
        x = ref[...]
        if x.shape[0] == nrows:
            return x
        real = lax.broadcasted_iota(jnp.int32, (nrows, x.shape[1]), 0) < x.shape[0]
        return jnp.where(real, jnp.broadcast_to(x, (nrows, x.shape[1])), 0.0)

    nk = 2 * WINDOW
    q = new_rows_of(q_ref)
    qs = jnp.concatenate([q[:, g * KV_W:(g + 1) * KV_W] for g in range(GROUP)], axis=0)
    qs = (qs * ATTN_SCALE).astype(BF16)
    k_all = jnp.concatenate([kp_ref[...], new_rows_of(kc_ref, WINDOW)], axis=0).astype(BF16)
    v_all = jnp.concatenate([vp_ref[...], new_rows_of(vc_ref, WINDOW)], axis=0)
    if decode:
        for cache_ref, new_ref, out_ref in ((kp_ref, kc_ref, ko_ref), (vp_ref, vc_ref, vo_ref)):
            out_ref[0:WINDOW - new_rows] = cache_ref[new_rows:WINDOW]
            out_ref[WINDOW - new_rows:WINDOW] = new_ref[...]
    lane_head = lax.broadcasted_iota(jnp.int32, (nk, KV_W), 1) // HEAD_DIM
    k_cat = jnp.concatenate([jnp.where(lane_head == kvh, k_all, jnp.zeros((), BF16)) for kvh in range(KV_HEADS)],
                            axis=0)
    st_all = lax.dot_general(k_cat, qs, (((1,), (1,)), ((), ())), preferred_element_type=F32)
    kj = lax.broadcasted_iota(jnp.int32, (nk, rows), 0)
    qi = lax.broadcasted_iota(jnp.int32, (nk, rows), 1) & (tq - 1)
    mask = jnp.logical_and(kj > qi, kj <= qi + WINDOW)
    if not decode:
        mask = jnp.logical_and(mask, jnp.logical_or(n > 0, kj >= WINDOW))
    probs = []
    for kvh in range(KV_HEADS):
        s = jnp.where(mask, st_all[kvh * nk:(kvh + 1) * nk] + bias_ref[kvh], NEG_INF)
        sink = sink_ref[kvh]
        m = jnp.maximum(jnp.max(s, 0, keepdims=True), sink)
        p = jnp.exp(s - m)
        den = jnp.sum(p, 0, keepdims=True) + jnp.exp(sink - m)
        probs.append((p * (1.0 / den)).astype(BF16))
    pt_all = jnp.concatenate(probs, axis=0)
    vt = v_all.T
    row_head = lax.broadcasted_iota(jnp.int32, (KV_W, KV_HEADS * nk), 0) // HEAD_DIM
    col_head = lax.broadcasted_iota(jnp.int32, (KV_W, KV_HEADS * nk), 1) // nk
    vt_cat = jnp.where(row_head == col_head, jnp.concatenate([vt] * KV_HEADS, axis=1), 0.0).astype(BF16)
    o = _dot(vt_cat, pt_all).T
    attn = jnp.concatenate([o[g * tq:(g + 1) * tq] for g in range(GROUP)], axis=1)

    u = new_rows_of(gc_ref) * new_rows_of(h_ref)
    if decode:
        ext_u[HALO - (CONV_K - 1):HALO] = us_ref[...]
    else:
        ext_u[0:HALO] = jnp.where(n > 0, gcp_ref[...] * hp_ref[...], 0.0)
    ext_u[HALO:HALO + tq] = u
    cw = cw_ref[...]
    conv = ext_u[HALO - 2:HALO - 2 + tq] * cw[0:1]
    conv = conv + ext_u[HALO - 1:HALO - 1 + tq] * cw[1:2]
    conv = conv + u * cw[2:3]
    c = new_rows_of(gb_ref) * conv
    uo_ref[...] = ext_u[new_rows + HALO - (CONV_K - 1):new_rows + HALO]

    pin = new_rows_of(pin_ref)
    if decode:
        ext_p[HALO - POOL_PREV:HALO] = ps_state_ref[...]
    else:
        ext_p[0:HALO] = jnp.where(n > 0, pinp_ref[...], 0.0)
    ext_p[HALO:HALO + tq] = pin
    pos = pos0 + n * tq + lax.broadcasted_iota(jnp.int32, (tq, 1), 0)
    pooled = []
    for g, w in enumerate(POOL_WINDOWS):
        lo = g * POOL_GROUP
        cur = pin[:, lo:lo + POOL_GROUP]
        win = ext_p[HALO - (w - 1):HALO - (w - 1) + tq, lo:lo + POOL_GROUP]
        for k in range(w - 2, -1, -1):
            win = win + ext_p[HALO - k:HALO - k + tq, lo:lo + POOL_GROUP]
        cnt = jnp.minimum(pos + 1, w).astype(F32)
        d = (win / cnt - cur).astype(BF16)
        pooled.append(_dot(d, pw_ref[g]))
    pm = jnp.concatenate(pooled, axis=1) * ps_ref[...]

    cat = jnp.concatenate([attn, c, pm], axis=1).astype(BF16)
    if decode:
        cat_ref[...] = cat[0:new_rows]
    else:
        x1_ref[...] = _layer_norm(alpha * x_ref[...] + _dot(cat, wo_ref[...]), lng_ref[...], lnb_ref[...])


def _mixers_prompt(z, tables, cw, pw, ps, w_o, layer, x, ln_g, ln_b, alpha, cast_next=None):
    b, s, _ = z.shape
    tq = WINDOW
    nb = s // tq
    bias, sink = tables
    cur = lambda blk: (lambda i, n: (i, n, blk))
    prev = lambda blk: (lambda i, n: (i, jnp.maximum(n - 1, 0), blk))
    halo = lambda blk: (lambda i, n: (i, jnp.maximum(n * (tq // HALO) - 1, 0), blk))
    const3 = lambda i, n: (0, 0, 0)
    const2 = lambda i, n: (0, 0)
    in_specs = [
        pl.BlockSpec((None, tq, ATTN_W), cur(Q_BLK)),
        pl.BlockSpec((None, tq, KV_W), cur(K_BLK)),
        pl.BlockSpec((None, tq, KV_W), cur(V_BLK)),
        pl.BlockSpec((None, tq, KV_W), prev(K_BLK)),
        pl.BlockSpec((None, tq, KV_W), prev(V_BLK)),
        pl.BlockSpec((None, tq, CONV_W), cur(GB_BLK)),
        pl.BlockSpec((None, tq, CONV_W), cur(GC_BLK)),
        pl.BlockSpec((None, tq, CONV_W), cur(H_BLK)),
        pl.BlockSpec((None, tq, POOL_W), cur(PIN_BLK)),
        pl.BlockSpec((None, HALO, CONV_W), halo(GC_BLK)),
        pl.BlockSpec((None, HALO, CONV_W), halo(H_BLK)),
        pl.BlockSpec((None, HALO, POOL_W), halo(PIN_BLK)),
        pl.BlockSpec(bias.shape, const3),
        pl.BlockSpec(sink.shape, const3),
        pl.BlockSpec(cw.shape, const2),
        pl.BlockSpec(pw.shape, const3),
        pl.BlockSpec(ps.shape, const2),
        pl.BlockSpec((None, D_MODEL, D_MODEL), lambda i, n: (layer, 0, 0), pipeline_mode=pl.Buffered(1)),
        pl.BlockSpec((None, tq, D_MODEL), lambda i, n: (i, n, 0)),
        pl.BlockSpec((None, 1, D_MODEL), lambda i, n: (layer, 0, 0)),
        pl.BlockSpec((None, 1, D_MODEL), lambda i, n: (layer, 0, 0)),
    ]
    operands = [z] * 12 + [bias, sink, cw, pw, ps, w_o, x, ln_g, ln_b]
    out_specs = [pl.BlockSpec((None, tq, D_MODEL), lambda i, n: (i, n, 0)),
                 pl.BlockSpec((None, CONV_K - 1, CONV_W), lambda i, n: (i, 0, 0))]
    out_shape = [jax.ShapeDtypeStruct((b, s, D_MODEL), F32),
                 jax.ShapeDtypeStruct((b, CONV_K - 1, CONV_W), F32)]
    cast_tf = None
    if cast_next is not None:
        w_up, w_down, cast_layers, cast_tf = cast_next
        for cast_layer, w in [(cl, w) for cl in cast_layers for w in (w_up, w_down)]:
            _, k, width = w.shape
            rows = next(r for r in range(2 * SUBLANES, k + 1, 2 * SUBLANES) if k % r == 0 and k // r <= b * nb)
            nblocks = k // rows
            index = lambda i, n, nblocks=nblocks: (jnp.minimum(i * nb + n, nblocks - 1), 0)
            in_specs.append(pl.BlockSpec((None, rows, width),
                                         lambda i, n, index=index, cl=cast_layer: (cl,) + index(i, n)))
            operands.append(w)
            out_specs.append(pl.BlockSpec((None, rows, width), lambda i, n, index=index: (0,) + index(i, n)))
            out_shape.append(jax.ShapeDtypeStruct((1, k, width), BF16))
    return pl.pallas_call(
        functools.partial(_mixers_body, False, tq, tq, 0, alpha, cast_tf),
        grid=(b, nb),
        in_specs=in_specs,
        out_specs=out_specs,
        out_shape=out_shape,
        scratch_shapes=[pltpu.VMEM((HALO + tq, CONV_W), F32), pltpu.VMEM((HALO + tq, POOL_W), F32)],
        compiler_params=_params("arbitrary", "arbitrary"),
        name="mixers_prompt",
    )(*operands)


def _mixers_decode(zs, ck, cv, conv_state, pool_state, tables, cw, pw, ps):
    b = zs.shape[0]
    tq = DECODE_ROWS
    bias, sink = tables
    new = lambda width, blk: pl.BlockSpec((None, 1, width), lambda i, n: (i, 0, blk))
    whole = lambda rows, width: pl.BlockSpec((None, rows, width), lambda i, n: (i, 0, 0))
    const3 = lambda i, n: (0, 0, 0)
    const2 = lambda i, n: (0, 0)
    in_specs = [
        new(ATTN_W, Q_BLK), new(KV_W, K_BLK), new(KV_W, V_BLK),
        whole(WINDOW, KV_W), whole(WINDOW, KV_W),
        new(CONV_W, GB_BLK), new(CONV_W, GC_BLK), new(CONV_W, H_BLK), new(POOL_W, PIN_BLK),
        whole(CONV_K - 1, CONV_W), whole(POOL_PREV, POOL_W),
        pl.BlockSpec(bias.shape, const3),
        pl.BlockSpec(sink.shape, const3),
        pl.BlockSpec(cw.shape, const2),
        pl.BlockSpec(pw.shape, const3),
        pl.BlockSpec(ps.shape, const2),
    ]
    return pl.pallas_call(
        functools.partial(_mixers_body, True, tq, 1, PAST_LEN, None, None),
        grid=(b, 1),
        in_specs=in_specs,
        out_specs=[whole(1, D_MODEL), whole(CONV_K - 1, CONV_W), whole(WINDOW, KV_W), whole(WINDOW, KV_W)],
        out_shape=[jax.ShapeDtypeStruct((b, 1, D_MODEL), BF16),
                   jax.ShapeDtypeStruct((b, CONV_K - 1, CONV_W), F32),
                   jax.ShapeDtypeStruct((b, WINDOW, KV_W), F32),
                   jax.ShapeDtypeStruct((b, WINDOW, KV_W), F32)],
        scratch_shapes=[pltpu.VMEM((HALO + tq, CONV_W), F32), pltpu.VMEM((HALO + tq, POOL_W), F32)],
        compiler_params=_params("arbitrary", "arbitrary"),
        name="mixers_decode",
    )(zs, zs, zs, ck, cv, zs, zs, zs, zs, conv_state, pool_state, bias, sink, cw, pw, ps)


def _bias_table(bias_by_dist, tq):
    nk = 2 * WINDOW
    by_offset = jnp.zeros((N_HEADS, nk + 1), F32).at[:, 1:WINDOW + 1].set(bias_by_dist[::-1].T)
    rows = jnp.tile(by_offset, (1, tq))[:, :tq * nk].reshape(N_HEADS, tq, nk)
    return jnp.swapaxes(rows.reshape(KV_HEADS, GROUP * tq, nk), 1, 2)


def _sink_table(sinks_l, tq):
    return jnp.broadcast_to(sinks_l.astype(F32).reshape(KV_HEADS, GROUP, 1, 1),
                            (KV_HEADS, GROUP, 1, tq)).reshape(KV_HEADS, 1, GROUP * tq)


def _t5_bucket(n):
    max_exact = N_BUCKETS // 2
    nf = jnp.maximum(n, 1).astype(F32)
    large = max_exact + (jnp.log(nf / max_exact) / math.log(MAX_DISTANCE / max_exact)
                         * (N_BUCKETS - max_exact)).astype(jnp.int32)
    large = jnp.minimum(large, N_BUCKETS - 1)
    return jnp.where(n < max_exact, n, large)


def _proj_ln_body(nk, alpha, a_ref, w_ref, x_ref, g_ref, b_ref, o_ref, *scratch):
    part = _dot(a_ref[...], w_ref[...])
    if nk == 1:
        o_ref[...] = _layer_norm(alpha * x_ref[...] + part, g_ref[...], b_ref[...])
        return
    acc_ref, = scratch
    k = pl.program_id(1)

    @pl.when(k == 0)
    def _():
        acc_ref[...] = jnp.zeros_like(acc_ref)

    acc_ref[...] += part

    @pl.when(k == nk - 1)
    def _():
        o_ref[...] = _layer_norm(alpha * x_ref[...] + acc_ref[...], g_ref[...], b_ref[...])


def _proj_ln(a, w, w_layer, x, g, b, layer, alpha, tm, tk):
    m, k = a.shape
    n = w.shape[2]
    nk = k // tk
    return pl.pallas_call(
        functools.partial(_proj_ln_body, nk, alpha),
        grid=(m // tm, nk),
        in_specs=[pl.BlockSpec((tm, tk), lambda i, kk: (i, kk)),
                  pl.BlockSpec((None, tk, n), lambda i, kk: (w_layer, kk, 0)),
                  pl.BlockSpec((tm, n), lambda i, kk: (i, 0)),
                  pl.BlockSpec((None, 1, n), lambda i, kk: (layer, 0, 0)),
                  pl.BlockSpec((None, 1, n), lambda i, kk: (layer, 0, 0))],
        out_specs=pl.BlockSpec((tm, n), lambda i, kk: (i, 0)),
        out_shape=jax.ShapeDtypeStruct((m, n), F32),
        scratch_shapes=[] if nk == 1 else [pltpu.VMEM((tm, n), F32)],
        compiler_params=_params("arbitrary", "arbitrary"),
        name="proj_ln",
    )(a, w, x, g, b)


def _ffn_body(tm, tf, ns, tiles_per_seq, nc, alpha, x_ref, wu_ref, wd_ref, cwg_ref, cwv_ref, g_ref, b_ref,
              o_ref, tail_g_ref, tail_v_ref, xb_ref, ext, carry):
    i = pl.program_id(0)
    c = pl.program_id(1)
    ts = tm // ns

    @pl.when(c == 0)
    def _():
        xb_ref[...] = x_ref[...].astype(BF16)
        o_ref[...] = jnp.zeros_like(o_ref)

    seq_start = (i % tiles_per_seq) == 0

    def up_proj(h, prev_tail):
        up = _dot(xb_ref[pl.ds(h * ts, ts)], wu_ref[...])
        if h == 0:
            @pl.when(seq_start)
            def _():
                ext[0, 0:SUBLANES] = jnp.zeros((SUBLANES, 2 * tf), F32)

            @pl.when(jnp.logical_not(seq_start))
            def _():
                ext[0, 0:SUBLANES] = carry[c]
        else:
            ext[h, 0:SUBLANES] = prev_tail
        ext[h, SUBLANES:SUBLANES + ts] = up
        tail = up[ts - SUBLANES:ts]
        if h == ns - 1:
            carry[c] = tail
            tail_g_ref[...] = tail[:, :tf]
            tail_v_ref[...] = tail[:, tf:]
        return tail

    def gate(h):
        cw = jnp.concatenate([cwg_ref[...], cwv_ref[...]], axis=1)
        hc = ext[h, SUBLANES - 2:SUBLANES - 2 + ts] * cw[0:1]
        hc = hc + ext[h, SUBLANES - 1:SUBLANES - 1 + ts] * cw[1:2]
        hc = hc + ext[h, SUBLANES:SUBLANES + ts] * cw[2:3]
        return (jax.nn.silu(hc[:, :tf]) * hc[:, tf:]).astype(BF16)

    def down_proj(h, act):
        o_ref[pl.ds(h * ts, ts)] += _dot(act, wd_ref[...])

    tail = up_proj(0, None)
    for h in range(ns):
        act = gate(h)
        if h + 1 < ns:
            tail = up_proj(h + 1, tail)
        down_proj(h, act)

    @pl.when(c == nc - 1)
    def _():
        o_ref[...] = _layer_norm(alpha * x_ref[...] + o_ref[...], g_ref[...], b_ref[...])


def _ffn_prompt(x, w_up_il, w_down, w_layer, cw, g, b, layer, alpha, seq, tm, tf, ns):
    m = x.shape[0]
    nc = D_FF // tf
    tiles_per_seq = seq // tm
    tail_spec = pl.BlockSpec((None, SUBLANES, tf), lambda i, c: (i, 0, c))
    tail_shape = jax.ShapeDtypeStruct((m // tm, SUBLANES, D_FF), F32)
    out, tail_g, tail_v = pl.pallas_call(
        functools.partial(_ffn_body, tm, tf, ns, tiles_per_seq, nc, alpha),
        grid=(m // tm, nc),
        in_specs=[pl.BlockSpec((tm, D_MODEL), lambda i, c: (i, 0), pipeline_mode=pl.Buffered(1)),
                  pl.BlockSpec((None, D_MODEL, 2 * tf), lambda i, c: (w_layer, 0, c)),
                  pl.BlockSpec((None, tf, D_MODEL), lambda i, c: (w_layer, c, 0)),
                  pl.BlockSpec((None, CONV_K, tf), lambda i, c: (layer, 0, c)),
                  pl.BlockSpec((None, CONV_K, tf), lambda i, c: (layer, 0, nc + c)),
                  pl.BlockSpec((None, 1, D_MODEL), lambda i, c: (layer, 0, 0)),
                  pl.BlockSpec((None, 1, D_MODEL), lambda i, c: (layer, 0, 0))],
        out_specs=[pl.BlockSpec((tm, D_MODEL), lambda i, c: (i, 0)), tail_spec, tail_spec],
        out_shape=[jax.ShapeDtypeStruct((m, D_MODEL), F32), tail_shape, tail_shape],
        scratch_shapes=[pltpu.VMEM((tm, D_MODEL), BF16),
                        pltpu.VMEM((ns, SUBLANES + tm // ns, 2 * tf), F32),
                        pltpu.VMEM((nc, SUBLANES, 2 * tf), F32)],
        compiler_params=_params("arbitrary", "arbitrary"),
        name="ffn_prompt",
    )(x, w_up_il, w_down, cw, cw, g, b)
    last = slice(tiles_per_seq - 1, None, tiles_per_seq)
    return out, jnp.concatenate([tail_g[last], tail_v[last]], -1)


def _gate_decode_body(ug_ref, uv_ref, sg_ref, sv_ref, cwg_ref, cwv_ref, o_ref):
    def conv(u_ref, s_ref, cw_ref):
        cw = cw_ref[...]
        y = s_ref[:, 0, :] * cw[0:1]
        y = y + s_ref[:, 1, :] * cw[1:2]
        return y + u_ref[...] * cw[2:3]

    hg = conv(ug_ref, sg_ref, cwg_ref)
    hv = conv(uv_ref, sv_ref, cwv_ref)
    o_ref[...] = (jax.nn.silu(hg) * hv).astype(BF16)


def _gate_decode(up, state, cw, layer, tf):
    b = up.shape[0]
    nc = D_FF // tf
    return pl.pallas_call(
        _gate_decode_body,
        grid=(nc,),
        in_specs=[pl.BlockSpec((b, tf), lambda c: (0, c)),
                  pl.BlockSpec((b, tf), lambda c: (0, nc + c)),
                  pl.BlockSpec((b, CONV_K - 1, tf), lambda c: (0, 0, c)),
                  pl.BlockSpec((b, CONV_K - 1, tf), lambda c: (0, 0, nc + c)),
                  pl.BlockSpec((None, CONV_K, tf), lambda c: (layer, 0, c)),
                  pl.BlockSpec((None, CONV_K, tf), lambda c: (layer, 0, nc + c))],
        out_specs=pl.BlockSpec((b, tf), lambda c: (0, c)),
        out_shape=jax.ShapeDtypeStruct((b, D_FF), BF16),
        compiler_params=_params("arbitrary"),
        name="gate_decode",
    )(up, up, state, state, cw, cw)


def kernel(x_prompt, x_sample, cache_k, cache_v, state_conv, state_pool, state_ffn, rel_table, w_in, conv_w,
           pool_w, pool_scale, sinks, w_o, ln1_g, ln1_b, w_up, ffn_conv_w, w_down, ln2_g, ln2_b):
    depth = w_in.shape[0]
    bp_, seq, _ = x_prompt.shape
    bs = x_sample.shape[0]
    assert x_sample.shape[1] == 1 and cache_k.shape[2] == WINDOW
    alpha = (2 * depth) ** 0.25
    nc = D_FF // FFN_TF

    w_in_b = _cast_w_in(w_in, 256)
    w_o_b = _cast_w_o(w_o)
    w_up_il, w_down_b = [], []
    natural_col = lambda j: jnp.where(j % 2 == 0, j // 2, nc + j // 2)
    pool_w_b = pool_w.astype(BF16)

    bias_by_dist = rel_table.astype(F32)[_t5_bucket(jnp.arange(WINDOW))]
    bias_p = _bias_table(bias_by_dist, WINDOW)
    bias_s = _bias_table(bias_by_dist, DECODE_ROWS)

    ln1_g, ln1_b, ln2_g, ln2_b = (t[:, None, :] for t in (ln1_g, ln1_b, ln2_g, ln2_b))

    xp = x_prompt.reshape(bp_ * seq, D_MODEL)
    xs = x_sample.reshape(bs, D_MODEL)

    outs_p, outs_s = [], []
    for l in range(depth):
        tab_p = (bias_p, _sink_table(sinks[l], WINDOW))
        tab_s = (bias_s, _sink_table(sinks[l], DECODE_ROWS))
        ps = pool_scale[l][None, :]

        z = _matmul(xp, w_in_b, l, 1024, IN_W // 2).reshape(bp_, seq, IN_W)
        cast_layers = [cl for cl in ((0, 1) if l == 0 else (l + 1,)) if cl < depth]
        cast_next = (w_up, w_down, cast_layers, FFN_TF) if cast_layers else None
        x1, u_tail, *next_w = _mixers_prompt(z, tab_p, conv_w[l], pool_w_b[l], ps, w_o_b, l,
                                             xp.reshape(bp_, seq, D_MODEL), ln1_g, ln1_b, alpha, cast_next)
        w_up_il.extend(next_w[0::2])
        w_down_b.extend(next_w[1::2])
        x1 = x1.reshape(bp_ * seq, D_MODEL)
        xp, up_tail = _ffn_prompt(x1, w_up_il[l], w_down_b[l], 0, ffn_conv_w, ln2_g, ln2_b, l, alpha, seq,
                                  FFN_TM, FFN_TF, FFN_NS)
        outs_p.append((
            z[:, seq - WINDOW:, ATTN_W:ATTN_W + KV_W].reshape(bp_, WINDOW, KV_HEADS, HEAD_DIM),
            z[:, seq - WINDOW:, ATTN_W + KV_W:ATTN_W + 2 * KV_W].reshape(bp_, WINDOW, KV_HEADS, HEAD_DIM),
            u_tail,
            z[:, seq - POOL_PREV:, IN_W - POOL_W:],
            up_tail[:, SUBLANES - (CONV_K - 1):],
        ))

        zs = _matmul(xs, w_in_b, l, bs, 512)
        ck = cache_k[l].reshape(bs, WINDOW, KV_W)
        cv = cache_v[l].reshape(bs, WINDOW, KV_W)
        cat_s, u_new, k_s, v_s = _mixers_decode(zs[:, None, :], ck, cv, state_conv[l], state_pool[l], tab_s,
                                                conv_w[l], pool_w_b[l], ps)
        x1s = _proj_ln(cat_s.reshape(bs, D_MODEL), w_o_b, l, xs, ln1_g, ln1_b, l, alpha, bs, D_MODEL)
        up_s = _matmul(x1s, w_up_il[l], 0, bs, FFN_TF, natural_col)
        act_s = _gate_decode(up_s, state_ffn[l], ffn_conv_w, l, FFN_TF)
        xs = _proj_ln(act_s, w_down_b[l], 0, x1s, ln2_g, ln2_b, l, alpha, bs, 512)
        outs_s.append((
            k_s.reshape(bs, WINDOW, KV_HEADS, HEAD_DIM),
            v_s.reshape(bs, WINDOW, KV_HEADS, HEAD_DIM),
            u_new,
            jnp.concatenate([state_pool[l][:, 1:], zs[:, None, IN_W - POOL_W:]], 1),
            jnp.concatenate([state_ffn[l][:, 1:], up_s[:, None, :]], 1),
        ))

    st = lambda lst, i: jnp.stack([e[i] for e in lst], 0)
    return (xp.reshape(bp_, seq, D_MODEL), xs.reshape(bs, 1, D_MODEL),
            st(outs_p, 0), st(outs_p, 1), st(outs_p, 2), st(outs_p, 3), st(outs_p, 4),
            st(outs_s, 0), st(outs_s, 1), st(outs_s, 2), st(outs_s, 3), st(outs_s, 4))
```

```python
import functools
import math

import jax
import jax.numpy as jnp
from jax import lax
from jax.experimental import pallas as pl
from jax.experimental.pallas import tpu as pltpu

D_MODEL = 2048
N_HEADS = 16
KV_HEADS = 4
HEAD_DIM = 64
GROUP = N_HEADS // KV_HEADS
ATTN_W = N_HEADS * HEAD_DIM
KV_W = KV_HEADS * HEAD_DIM
WINDOW = 128
ATTN_SCALE = HEAD_DIM ** -0.5
NEG_INF = -1e30
N_BUCKETS = 32
MAX_DISTANCE = 128
CONV_W = D_MODEL // 4
CONV_K = 3
POOL_W = D_MODEL // 4
POOL_WINDOWS = (2, 4, 8, 16)
POOL_GROUP = POOL_W // len(POOL_WINDOWS)
POOL_PREV = 15
IN_W = ATTN_W + 2 * KV_W + 3 * CONV_W + POOL_W
D_FF = 5632
PAST_LEN = 16384
LN_EPS = 1e-5

HALO = 16
SUBLANES = 8
DECODE_ROWS = 128 // GROUP
DECODE_SEQS_PER_STEP = 4
Q_BLK = 0
K_BLK = ATTN_W // KV_W
V_BLK = K_BLK + 1
GB_BLK = (ATTN_W + 2 * KV_W) // CONV_W
GC_BLK = GB_BLK + 1
H_BLK = GB_BLK + 2
PIN_BLK = GB_BLK + 3

FFN_TM = 1024
FFN_TF = 512
FFN_NS = 4

VMEM_LIMIT = 60 * 1024 * 1024

BF16 = jnp.bfloat16
F32 = jnp.float32


def _params(*sem, flags=None):
    return pltpu.CompilerParams(dimension_semantics=sem, vmem_limit_bytes=VMEM_LIMIT, flags=flags)


def _dot(a, b):
    return jnp.dot(a, b, preferred_element_type=F32)


def _layer_norm(y, g, b):
    mu = jnp.mean(y, -1, keepdims=True)
    yc = y - mu
    var = jnp.mean(yc * yc, -1, keepdims=True)
    return yc * lax.rsqrt(var + LN_EPS) * g + b


def _head_block_perm(blk):
    return (blk % GROUP) * KV_HEADS + blk // GROUP


def _cast_w_in_body(w_ref, o_ref):
    w = w_ref[...]
    order = sorted(range(N_HEADS), key=_head_block_perm)
    q = jnp.concatenate([w[:, h * HEAD_DIM:(h + 1) * HEAD_DIM] for h in order], axis=1)
    o_ref[...] = jnp.concatenate([q, w[:, ATTN_W:]], axis=1).astype(BF16)


def _cast_w_in(w, rows):
    depth, k, n = w.shape
    return pl.pallas_call(
        _cast_w_in_body,
        grid=(depth, k // rows),
        in_specs=[pl.BlockSpec((None, rows, n), lambda l, i: (l, i, 0))],
        out_specs=pl.BlockSpec((None, rows, n), lambda l, i: (l, i, 0)),
        out_shape=jax.ShapeDtypeStruct(w.shape, BF16),
        compiler_params=_params("arbitrary", "arbitrary"),
        name="cast_w_in",
    )(w)


def _cast_w_o_body(w_ref, o_ref):
    @pl.when(pl.program_id(1) == 0)
    def _():
        w = w_ref[...]
        order = sorted(range(N_HEADS), key=_head_block_perm)
        o_ref[...] = jnp.concatenate([w[h * HEAD_DIM:(h + 1) * HEAD_DIM] for h in order], axis=0).astype(BF16)

    @pl.when(pl.program_id(1) > 0)
    def _():
        o_ref[...] = w_ref[...].astype(BF16)


def _cast_w_o(w):
    depth, k, n = w.shape
    return pl.pallas_call(
        _cast_w_o_body,
        grid=(depth, k // ATTN_W),
        in_specs=[pl.BlockSpec((None, ATTN_W, n), lambda l, i: (l, i, 0))],
        out_specs=pl.BlockSpec((None, ATTN_W, n), lambda l, i: (l, i, 0)),
        out_shape=jax.ShapeDtypeStruct(w.shape, BF16),
        compiler_params=_params("arbitrary", "arbitrary"),
        name="cast_w_o",
    )(w)


def _matmul_body(x_ref, w_ref, o_ref, xb_ref):
    @pl.when(pl.program_id(1) == 0)
    def _():
        xb_ref[...] = x_ref[...].astype(BF16)

    o_ref[...] = _dot(xb_ref[...], w_ref[...])


def _matmul(x, w, layer, tm, tn, out_col_perm=None):
    m, k = x.shape
    n = w.shape[2]
    out_col_perm = (lambda j: j) if out_col_perm is None else out_col_perm
    return pl.pallas_call(
        _matmul_body,
        grid=(m // tm, n // tn),
        in_specs=[pl.BlockSpec((tm, k), lambda i, j: (i, 0)),
                  pl.BlockSpec((None, k, tn), lambda i, j: (layer, 0, j))],
        out_specs=pl.BlockSpec((tm, tn), lambda i, j: (i, out_col_perm(j))),
        out_shape=jax.ShapeDtypeStruct((m, n), F32),
        scratch_shapes=[pltpu.VMEM((tm, k), BF16)],
        compiler_params=_params("arbitrary", "arbitrary"),
        name="matmul",
    )(x, w)


def _mixers_body(decode, tq, new_rows, pos0, alpha, cast_tf, *refs):
    if decode:
        (q_ref, kc_ref, vc_ref, kp_ref, vp_ref, gb_ref, gc_ref, h_ref, pin_ref, us_ref, ps_state_ref,
         bias_ref, sink_ref, cw_ref, pw_ref, ps_ref,
         cat_ref, uo_ref, ko_ref, vo_ref, ext_u, ext_p) = refs
    else:
        (q_ref, kc_ref, vc_ref, kp_ref, vp_ref, gb_ref, gc_ref, h_ref, pin_ref,
         gcp_ref, hp_ref, pinp_ref,
         bias_ref, sink_ref, cw_ref, pw_ref, ps_ref, wo_ref, x_ref, lng_ref, lnb_ref) = refs[:21]
        ext_u, ext_p = refs[-2:]
        cast_src = refs[21:-4 - (len(refs) - 25) // 2]
        x1_ref, uo_ref = refs[21 + len(cast_src):23 + len(cast_src)]
        cast_dst = refs[23 + len(cast_src):-2]
        for k, (src, dst) in enumerate(zip(cast_src, cast_dst)):
            w = src[...]
            if k % 2 == 0:
                w = jnp.concatenate(
                    [w[:, (j % 2) * D_FF + (j // 2) * cast_tf:(j % 2) * D_FF + (j // 2 + 1) * cast_tf]
                     for j in range(2 * (D_FF // cast_tf))], axis=1)
            dst[...] = w.astype(BF16)
    n = pl.program_id(1)
    rows = GROUP * tq

    def new_rows_of(ref, nrows=tq):
        x = ref[...]
        if x.shape[0] == nrows:
            return x
        real = lax.broadcasted_iota(jnp.int32, (nrows, x.shape[1]), 0) < x.shape[0]
        return jnp.where(real, jnp.broadcast_to(x, (nrows, x.shape[1])), 0.0)

    nk = 2 * WINDOW
    q = new_rows_of(q_ref)
    qs = jnp.concatenate([q[:, g * KV_W:(g + 1) * KV_W] for g in range(GROUP)], axis=0)
    qs = (qs * ATTN_SCALE).astype(BF16)
    k_all = jnp.concatenate([kp_ref[...], new_rows_of(kc_ref, WINDOW)], axis=0).astype(BF16)
    v_all = jnp.concatenate([vp_ref[...], new_rows_of(vc_ref, WINDOW)], axis=0)
    if decode:
        for cache_ref, new_ref, out_ref in ((kp_ref, kc_ref, ko_ref), (vp_ref, vc_ref, vo_ref)):
            out_ref[0:WINDOW - new_rows] = cache_ref[new_rows:WINDOW]
            out_ref[WINDOW - new_rows:WINDOW] = new_ref[...]
    lane_head = lax.broadcasted_iota(jnp.int32, (nk, KV_W), 1) // HEAD_DIM
    k_cat = jnp.concatenate([jnp.where(lane_head == kvh, k_all, jnp.zeros((), BF16)) for kvh in range(KV_HEADS)],
                            axis=0)
    st_all = lax.dot_general(k_cat, qs, (((1,), (1,)), ((), ())), preferred_element_type=F32)
    kj = lax.broadcasted_iota(jnp.int32, (nk, rows), 0)
    qi = lax.broadcasted_iota(jnp.int32, (nk, rows), 1) & (tq - 1)
    mask = jnp.logical_and(kj > qi, kj <= qi + WINDOW)
    if not decode:
        mask = jnp.logical_and(mask, jnp.logical_or(n > 0, kj >= WINDOW))
    probs = []
    for kvh in range(KV_HEADS):
        s = jnp.where(mask, st_all[kvh * nk:(kvh + 1) * nk] + bias_ref[kvh], NEG_INF)
        sink = sink_ref[kvh]
        m = jnp.maximum(jnp.max(s, 0, keepdims=True), sink)
        p = jnp.exp(s - m)
        den = jnp.sum(p, 0, keepdims=True) + jnp.exp(sink - m)
        probs.append((p * (1.0 / den)).astype(BF16))
    pt_all = jnp.concatenate(probs, axis=0)
    vt = v_all.T
    row_head = lax.broadcasted_iota(jnp.int32, (KV_W, KV_HEADS * nk), 0) // HEAD_DIM
    col_head = lax.broadcasted_iota(jnp.int32, (KV_W, KV_HEADS * nk), 1) // nk
    vt_cat = jnp.where(row_head == col_head, jnp.concatenate([vt] * KV_HEADS, axis=1), 0.0).astype(BF16)
    o = _dot(vt_cat, pt_all).T
    attn = jnp.concatenate([o[g * tq:(g + 1) * tq] for g in range(GROUP)], axis=1)

    u = new_rows_of(gc_ref) * new_rows_of(h_ref)
    if decode:
        ext_u[HALO - (CONV_K - 1):HALO] = us_ref[...]
    else:
        ext_u[0:HALO] = jnp.where(n > 0, gcp_ref[...] * hp_ref[...], 0.0)
    ext_u[HALO:HALO + tq] = u
    cw = cw_ref[...]
    conv = ext_u[HALO - 2:HALO - 2 + tq] * cw[0:1]
    conv = conv + ext_u[HALO - 1:HALO - 1 + tq] * cw[1:2]
    conv = conv + u * cw[2:3]
    c = new_rows_of(gb_ref) * conv
    uo_ref[...] = ext_u[new_rows + HALO - (CONV_K - 1):new_rows + HALO]

    pin = new_rows_of(pin_ref)
    if decode:
        ext_p[0:HALO - POOL_PREV] = jnp.zeros((HALO - POOL_PREV, POOL_W), F32)
        ext_p[HALO - POOL_PREV:HALO] = ps_state_ref[...]
    else:
        ext_p[0:HALO] = jnp.where(n > 0, pinp_ref[...], 0.0)
    ext_p[HALO:HALO + tq] = pin
    pos = pos0 + n * tq + lax.broadcasted_iota(jnp.int32, (tq, 1), 0)
    assert POOL_WINDOWS == tuple(2 << g for g in range(len(POOL_WINDOWS)))
    sums = ext_p[...]
    trailing = []
    for w in POOL_WINDOWS:
        sums = sums + pltpu.roll(sums, w // 2, 0)
        trailing.append(sums[HALO:HALO + tq, 0:POOL_GROUP])
        sums = sums[:, POOL_GROUP:]
    pooled = []
    for g, w in enumerate(POOL_WINDOWS):
        lo = g * POOL_GROUP
        cur = pin[:, lo:lo + POOL_GROUP]
        win = trailing[g]
        cnt = jnp.minimum(pos + 1, w).astype(F32)
        d = (win / cnt - cur).astype(BF16)
        pooled.append(_dot(d, pw_ref[g]))
    pm = jnp.concatenate(pooled, axis=1) * ps_ref[...]

    cat = jnp.concatenate([attn, c, pm], axis=1).astype(BF16)
    if decode:
        cat_ref[...] = cat[0:new_rows]
    else:
        x1_ref[...] = _layer_norm(alpha * x_ref[...] + _dot(cat, wo_ref[...]), lng_ref[...], lnb_ref[...])


def _mixers_decode_body(nseq, tq, *refs):
    n_in, n_shared, n_out = 11, 5, 4
    for s in range(nseq):
        own = lambda group: [r.at[s] for r in group]
        _mixers_body(True, tq, 1, PAST_LEN, None, None, *own(refs[:n_in]), *refs[n_in:n_in + n_shared],
                     *own(refs[n_in + n_shared:n_in + n_shared + n_out]), *own(refs[n_in + n_shared + n_out:]))


def _mixers_prompt(z, tables, cw, pw, ps, w_o, layer, x, ln_g, ln_b, alpha, cast_next=None):
    b, s, _ = z.shape
    tq = WINDOW
    nb = s // tq
    bias, sink = tables
    cur = lambda blk: (lambda i, n: (i, n, blk))
    prev = lambda blk: (lambda i, n: (i, jnp.maximum(n - 1, 0), blk))
    halo = lambda blk: (lambda i, n: (i, jnp.maximum(n * (tq // HALO) - 1, 0), blk))
    const3 = lambda i, n: (0, 0, 0)
    const2 = lambda i, n: (0, 0)
    in_specs = [
        pl.BlockSpec((None, tq, ATTN_W), cur(Q_BLK)),
        pl.BlockSpec((None, tq, KV_W), cur(K_BLK)),
        pl.BlockSpec((None, tq, KV_W), cur(V_BLK)),
        pl.BlockSpec((None, tq, KV_W), prev(K_BLK)),
        pl.BlockSpec((None, tq, KV_W), prev(V_BLK)),
        pl.BlockSpec((None, tq, CONV_W), cur(GB_BLK)),
        pl.BlockSpec((None, tq, CONV_W), cur(GC_BLK)),
        pl.BlockSpec((None, tq, CONV_W), cur(H_BLK)),
        pl.BlockSpec((None, tq, POOL_W), cur(PIN_BLK)),
        pl.BlockSpec((None, HALO, CONV_W), halo(GC_BLK)),
        pl.BlockSpec((None, HALO, CONV_W), halo(H_BLK)),
        pl.BlockSpec((None, HALO, POOL_W), halo(PIN_BLK)),
        pl.BlockSpec(bias.shape, const3),
        pl.BlockSpec(sink.shape, const3),
        pl.BlockSpec(cw.shape, const2),
        pl.BlockSpec(pw.shape, const3),
        pl.BlockSpec(ps.shape, const2),
        pl.BlockSpec((None, D_MODEL, D_MODEL), lambda i, n: (layer, 0, 0), pipeline_mode=pl.Buffered(1)),
        pl.BlockSpec((None, tq, D_MODEL), lambda i, n: (i, n, 0)),
        pl.BlockSpec((None, 1, D_MODEL), lambda i, n: (layer, 0, 0)),
        pl.BlockSpec((None, 1, D_MODEL), lambda i, n: (layer, 0, 0)),
    ]
    operands = [z] * 12 + [bias, sink, cw, pw, ps, w_o, x, ln_g, ln_b]
    out_specs = [pl.BlockSpec((None, tq, D_MODEL), lambda i, n: (i, n, 0)),
                 pl.BlockSpec((None, CONV_K - 1, CONV_W), lambda i, n: (i, 0, 0))]
    out_shape = [jax.ShapeDtypeStruct((b, s, D_MODEL), F32),
                 jax.ShapeDtypeStruct((b, CONV_K - 1, CONV_W), F32)]
    cast_tf = None
    if cast_next is not None:
        w_up, w_down, cast_layers, cast_tf = cast_next
        for cast_layer, w in [(cl, w) for cl in cast_layers for w in (w_up, w_down)]:
            _, k, width = w.shape
            rows = next(r for r in range(2 * SUBLANES, k + 1, 2 * SUBLANES) if k % r == 0 and k // r <= b * nb)
            nblocks = k // rows
            index = lambda i, n, nblocks=nblocks: (jnp.minimum(i * nb + n, nblocks - 1), 0)
            in_specs.append(pl.BlockSpec((None, rows, width),
                                         lambda i, n, index=index, cl=cast_layer: (cl,) + index(i, n)))
            operands.append(w)
            out_specs.append(pl.BlockSpec((None, rows, width), lambda i, n, index=index: (0,) + index(i, n)))
            out_shape.append(jax.ShapeDtypeStruct((1, k, width), BF16))
    return pl.pallas_call(
        functools.partial(_mixers_body, False, tq, tq, 0, alpha, cast_tf),
        grid=(b, nb),
        in_specs=in_specs,
        out_specs=out_specs,
        out_shape=out_shape,
        scratch_shapes=[pltpu.VMEM((HALO + tq, CONV_W), F32), pltpu.VMEM((HALO + tq, POOL_W), F32)],
        compiler_params=_params("arbitrary", "arbitrary"),
        name="mixers_prompt",
    )(*operands)


def _mixers_decode(zs, ck, cv, conv_state, pool_state, tables, cw, pw, ps):
    b = zs.shape[0]
    tq = DECODE_ROWS
    nseq = DECODE_SEQS_PER_STEP
    bias, sink = tables
    new = lambda width, blk: pl.BlockSpec((nseq, 1, width), lambda i, n: (i, 0, blk))
    whole = lambda rows, width: pl.BlockSpec((nseq, rows, width), lambda i, n: (i, 0, 0))
    const3 = lambda i, n: (0, 0, 0)
    const2 = lambda i, n: (0, 0)
    in_specs = [
        new(ATTN_W, Q_BLK), new(KV_W, K_BLK), new(KV_W, V_BLK),
        whole(WINDOW, KV_W), whole(WINDOW, KV_W),
        new(CONV_W, GB_BLK), new(CONV_W, GC_BLK), new(CONV_W, H_BLK), new(POOL_W, PIN_BLK),
        whole(CONV_K - 1, CONV_W), whole(POOL_PREV, POOL_W),
        pl.BlockSpec(bias.shape, const3),
        pl.BlockSpec(sink.shape, const3),
        pl.BlockSpec(cw.shape, const2),
        pl.BlockSpec(pw.shape, const3),
        pl.BlockSpec(ps.shape, const2),
    ]
    return pl.pallas_call(
        functools.partial(_mixers_decode_body, nseq, tq),
        grid=(b // nseq, 1),
        in_specs=in_specs,
        out_specs=[whole(1, D_MODEL), whole(CONV_K - 1, CONV_W), whole(WINDOW, KV_W), whole(WINDOW, KV_W)],
        out_shape=[jax.ShapeDtypeStruct((b, 1, D_MODEL), BF16),
                   jax.ShapeDtypeStruct((b, CONV_K - 1, CONV_W), F32),
                   jax.ShapeDtypeStruct((b, WINDOW, KV_W), F32),
                   jax.ShapeDtypeStruct((b, WINDOW, KV_W), F32)],
        scratch_shapes=[pltpu.VMEM((nseq, HALO + tq, CONV_W), F32), pltpu.VMEM((nseq, HALO + tq, POOL_W), F32)],
        compiler_params=_params("arbitrary", "arbitrary"),
        name="mixers_decode",
    )(zs, zs, zs, ck, cv, zs, zs, zs, zs, conv_state, pool_state, bias, sink, cw, pw, ps)


def _bias_table(bias_by_dist, tq):
    nk = 2 * WINDOW
    by_offset = jnp.zeros((N_HEADS, nk + 1), F32).at[:, 1:WINDOW + 1].set(bias_by_dist[::-1].T)
    rows = jnp.tile(by_offset, (1, tq))[:, :tq * nk].reshape(N_HEADS, tq, nk)
    return jnp.swapaxes(rows.reshape(KV_HEADS, GROUP * tq, nk), 1, 2)


def _sink_table(sinks_l, tq):
    return jnp.broadcast_to(sinks_l.astype(F32).reshape(KV_HEADS, GROUP, 1, 1),
                            (KV_HEADS, GROUP, 1, tq)).reshape(KV_HEADS, 1, GROUP * tq)


def _t5_bucket(n):
    max_exact = N_BUCKETS // 2
    nf = jnp.maximum(n, 1).astype(F32)
    large = max_exact + (jnp.log(nf / max_exact) / math.log(MAX_DISTANCE / max_exact)
                         * (N_BUCKETS - max_exact)).astype(jnp.int32)
    large = jnp.minimum(large, N_BUCKETS - 1)
    return jnp.where(n < max_exact, n, large)


def _proj_ln_body(nk, alpha, a_ref, w_ref, x_ref, g_ref, b_ref, o_ref, *scratch):
    part = _dot(a_ref[...], w_ref[...])
    if nk == 1:
        o_ref[...] = _layer_norm(alpha * x_ref[...] + part, g_ref[...], b_ref[...])
        return
    acc_ref, = scratch
    k = pl.program_id(1)

    @pl.when(k == 0)
    def _():
        acc_ref[...] = jnp.zeros_like(acc_ref)

    acc_ref[...] += part

    @pl.when(k == nk - 1)
    def _():
        o_ref[...] = _layer_norm(alpha * x_ref[...] + acc_ref[...], g_ref[...], b_ref[...])


def _proj_ln(a, w, w_layer, x, g, b, layer, alpha, tm, tk):
    m, k = a.shape
    n = w.shape[2]
    nk = k // tk
    return pl.pallas_call(
        functools.partial(_proj_ln_body, nk, alpha),
        grid=(m // tm, nk),
        in_specs=[pl.BlockSpec((tm, tk), lambda i, kk: (i, kk)),
                  pl.BlockSpec((None, tk, n), lambda i, kk: (w_layer, kk, 0)),
                  pl.BlockSpec((tm, n), lambda i, kk: (i, 0)),
                  pl.BlockSpec((None, 1, n), lambda i, kk: (layer, 0, 0)),
                  pl.BlockSpec((None, 1, n), lambda i, kk: (layer, 0, 0))],
        out_specs=pl.BlockSpec((tm, n), lambda i, kk: (i, 0)),
        out_shape=jax.ShapeDtypeStruct((m, n), F32),
        scratch_shapes=[] if nk == 1 else [pltpu.VMEM((tm, n), F32)],
        compiler_params=_params("arbitrary", "arbitrary"),
        name="proj_ln",
    )(a, w, x, g, b)


def _ffn_body(tm, tf, ns, tiles_per_seq, nc, alpha, x_ref, wu_ref, wd_ref, cwg_ref, cwv_ref, g_ref, b_ref,
              o_ref, tail_g_ref, tail_v_ref, xb_ref, ext, carry):
    i = pl.program_id(0)
    c = pl.program_id(1)
    ts = tm // ns

    @pl.when(c == 0)
    def _():
        xb_ref[...] = x_ref[...].astype(BF16)
        o_ref[...] = jnp.zeros_like(o_ref)

    seq_start = (i % tiles_per_seq) == 0

    def up_proj(h, prev_tail):
        up = _dot(xb_ref[pl.ds(h * ts, ts)], wu_ref[...])
        if h == 0:
            @pl.when(seq_start)
            def _():
                ext[0, 0:SUBLANES] = jnp.zeros((SUBLANES, 2 * tf), F32)

            @pl.when(jnp.logical_not(seq_start))
            def _():
                ext[0, 0:SUBLANES] = carry[c]
        else:
            ext[h, 0:SUBLANES] = prev_tail
        ext[h, SUBLANES:SUBLANES + ts] = up
        tail = up[ts - SUBLANES:ts]
        if h == ns - 1:
            carry[c] = tail
            tail_g_ref[...] = tail[:, :tf]
            tail_v_ref[...] = tail[:, tf:]
        return tail

    def gate(h):
        cw = jnp.concatenate([cwg_ref[...], cwv_ref[...]], axis=1)
        hc = ext[h, SUBLANES - 2:SUBLANES - 2 + ts] * cw[0:1]
        hc = hc + ext[h, SUBLANES - 1:SUBLANES - 1 + ts] * cw[1:2]
        hc = hc + ext[h, SUBLANES:SUBLANES + ts] * cw[2:3]
        return (jax.nn.silu(hc[:, :tf]) * hc[:, tf:]).astype(BF16)

    def down_proj(h, act):
        o_ref[pl.ds(h * ts, ts)] += _dot(act, wd_ref[...])

    tail = up_proj(0, None)
    for h in range(ns):
        act = gate(h)
        if h + 1 < ns:
            tail = up_proj(h + 1, tail)
        down_proj(h, act)

    @pl.when(c == nc - 1)
    def _():
        o_ref[...] = _layer_norm(alpha * x_ref[...] + o_ref[...], g_ref[...], b_ref[...])


def _ffn_prompt(x, w_up_il, w_down, w_layer, cw, g, b, layer, alpha, seq, tm, tf, ns):
    m = x.shape[0]
    nc = D_FF // tf
    tiles_per_seq = seq // tm
    tail_spec = pl.BlockSpec((None, SUBLANES, tf), lambda i, c: (i, 0, c))
    tail_shape = jax.ShapeDtypeStruct((m // tm, SUBLANES, D_FF), F32)
    out, tail_g, tail_v = pl.pallas_call(
        functools.partial(_ffn_body, tm, tf, ns, tiles_per_seq, nc, alpha),
        grid=(m // tm, nc),
        in_specs=[pl.BlockSpec((tm, D_MODEL), lambda i, c: (i, 0), pipeline_mode=pl.Buffered(1)),
                  pl.BlockSpec((None, D_MODEL, 2 * tf), lambda i, c: (w_layer, 0, c)),
                  pl.BlockSpec((None, tf, D_MODEL), lambda i, c: (w_layer, c, 0)),
                  pl.BlockSpec((None, CONV_K, tf), lambda i, c: (layer, 0, c)),
                  pl.BlockSpec((None, CONV_K, tf), lambda i, c: (layer, 0, nc + c)),
                  pl.BlockSpec((None, 1, D_MODEL), lambda i, c: (layer, 0, 0)),
                  pl.BlockSpec((None, 1, D_MODEL), lambda i, c: (layer, 0, 0))],
        out_specs=[pl.BlockSpec((tm, D_MODEL), lambda i, c: (i, 0)), tail_spec, tail_spec],
        out_shape=[jax.ShapeDtypeStruct((m, D_MODEL), F32), tail_shape, tail_shape],
        scratch_shapes=[pltpu.VMEM((tm, D_MODEL), BF16),
                        pltpu.VMEM((ns, SUBLANES + tm // ns, 2 * tf), F32),
                        pltpu.VMEM((nc, SUBLANES, 2 * tf), F32)],
        compiler_params=_params("arbitrary", "arbitrary"),
        name="ffn_prompt",
    )(x, w_up_il, w_down, cw, cw, g, b)
    last = slice(tiles_per_seq - 1, None, tiles_per_seq)
    return out, jnp.concatenate([tail_g[last], tail_v[last]], -1)


def _gate_decode_body(ug_ref, uv_ref, sg_ref, sv_ref, cwg_ref, cwv_ref, o_ref):
    def conv(u_ref, s_ref, cw_ref):
        cw = cw_ref[...]
        y = s_ref[:, 0, :] * cw[0:1]
        y = y + s_ref[:, 1, :] * cw[1:2]
        return y + u_ref[...] * cw[2:3]

    hg = conv(ug_ref, sg_ref, cwg_ref)
    hv = conv(uv_ref, sv_ref, cwv_ref)
    o_ref[...] = (jax.nn.silu(hg) * hv).astype(BF16)


def _gate_decode(up, state, cw, layer, tf):
    b = up.shape[0]
    nc = D_FF // tf
    return pl.pallas_call(
        _gate_decode_body,
        grid=(nc,),
        in_specs=[pl.BlockSpec((b, tf), lambda c: (0, c)),
                  pl.BlockSpec((b, tf), lambda c: (0, nc + c)),
                  pl.BlockSpec((b, CONV_K - 1, tf), lambda c: (0, 0, c)),
                  pl.BlockSpec((b, CONV_K - 1, tf), lambda c: (0, 0, nc + c)),
                  pl.BlockSpec((None, CONV_K, tf), lambda c: (layer, 0, c)),
                  pl.BlockSpec((None, CONV_K, tf), lambda c: (layer, 0, nc + c))],
        out_specs=pl.BlockSpec((b, tf), lambda c: (0, c)),
        out_shape=jax.ShapeDtypeStruct((b, D_FF), BF16),
        compiler_params=_params("arbitrary"),
        name="gate_decode",
    )(up, up, state, state, cw, cw)


def kernel(x_prompt, x_sample, cache_k, cache_v, state_conv, state_pool, state_ffn, rel_table, w_in, conv_w,
           pool_w, pool_scale, sinks, w_o, ln1_g, ln1_b, w_up, ffn_conv_w, w_down, ln2_g, ln2_b):
    depth = w_in.shape[0]
    bp_, seq, _ = x_prompt.shape
    bs = x_sample.shape[0]
    assert x_sample.shape[1] == 1 and cache_k.shape[2] == WINDOW
    alpha = (2 * depth) ** 0.25
    nc = D_FF // FFN_TF

    w_in_b = _cast_w_in(w_in, 256)
    w_o_b = _cast_w_o(w_o)
    w_up_il, w_down_b = [], []
    natural_col = lambda j: jnp.where(j % 2 == 0, j // 2, nc + j // 2)
    pool_w_b = pool_w.astype(BF16)

    bias_by_dist = rel_table.astype(F32)[_t5_bucket(jnp.arange(WINDOW))]
    bias_p = _bias_table(bias_by_dist, WINDOW)
    bias_s = _bias_table(bias_by_dist, DECODE_ROWS)

    ln1_g, ln1_b, ln2_g, ln2_b = (t[:, None, :] for t in (ln1_g, ln1_b, ln2_g, ln2_b))

    xp = x_prompt.reshape(bp_ * seq, D_MODEL)
    xs = x_sample.reshape(bs, D_MODEL)

    outs_p, outs_s = [], []
    for l in range(depth):
        tab_p = (bias_p, _sink_table(sinks[l], WINDOW))
        tab_s = (bias_s, _sink_table(sinks[l], DECODE_ROWS))
        ps = pool_scale[l][None, :]

        z = _matmul(xp, w_in_b, l, 1024, IN_W // 2).reshape(bp_, seq, IN_W)
        cast_layers = [cl for cl in ((0, 1) if l == 0 else (l + 1,)) if cl < depth]
        cast_next = (w_up, w_down, cast_layers, FFN_TF) if cast_layers else None
        x1, u_tail, *next_w = _mixers_prompt(z, tab_p, conv_w[l], pool_w_b[l], ps, w_o_b, l,
                                             xp.reshape(bp_, seq, D_MODEL), ln1_g, ln1_b, alpha, cast_next)
        w_up_il.extend(next_w[0::2])
        w_down_b.extend(next_w[1::2])
        x1 = x1.reshape(bp_ * seq, D_MODEL)
        xp, up_tail = _ffn_prompt(x1, w_up_il[l], w_down_b[l], 0, ffn_conv_w, ln2_g, ln2_b, l, alpha, seq,
                                  FFN_TM, FFN_TF, FFN_NS)
        outs_p.append((
            z[:, seq - WINDOW:, ATTN_W:ATTN_W + KV_W].reshape(bp_, WINDOW, KV_HEADS, HEAD_DIM),
            z[:, seq - WINDOW:, ATTN_W + KV_W:ATTN_W + 2 * KV_W].reshape(bp_, WINDOW, KV_HEADS, HEAD_DIM),
            u_tail,
            z[:, seq - POOL_PREV:, IN_W - POOL_W:],
            up_tail[:, SUBLANES - (CONV_K - 1):],
        ))

        zs = _matmul(xs, w_in_b, l, bs, 512)
        ck = cache_k[l].reshape(bs, WINDOW, KV_W)
        cv = cache_v[l].reshape(bs, WINDOW, KV_W)
        cat_s, u_new, k_s, v_s = _mixers_decode(zs[:, None, :], ck, cv, state_conv[l], state_pool[l], tab_s,
                                                conv_w[l], pool_w_b[l], ps)
        x1s = _proj_ln(cat_s.reshape(bs, D_MODEL), w_o_b, l, xs, ln1_g, ln1_b, l, alpha, bs, D_MODEL)
        up_s = _matmul(x1s, w_up_il[l], 0, bs, FFN_TF, natural_col)
        act_s = _gate_decode(up_s, state_ffn[l], ffn_conv_w, l, FFN_TF)
        xs = _proj_ln(act_s, w_down_b[l], 0, x1s, ln2_g, ln2_b, l, alpha, bs, 512)
        outs_s.append((
            k_s.reshape(bs, WINDOW, KV_HEADS, HEAD_DIM),
            v_s.reshape(bs, WINDOW, KV_HEADS, HEAD_DIM),
            u_new,
            jnp.concatenate([state_pool[l][:, 1:], zs[:, None, IN_W - POOL_W:]], 1),
            jnp.concatenate([state_ffn[l][:, 1:], up_s[:, None, :]], 1),
        ))

    st = lambda lst, i: jnp.stack([e[i] for e in lst], 0)
    return (xp.reshape(bp_, seq, D_MODEL), xs.reshape(bs, 1, D_MODEL),
            st(outs_p, 0), st(outs_p, 1), st(outs_p, 2), st(outs_p, 3), st(outs_p, 4),
            st(outs_s, 0), st(outs_s, 1), st(outs_s, 2), st(outs_s, 3), st(outs_s, 4))
```

```python
import functools
import math

import jax
import jax.numpy as jnp
from jax import lax
from jax.experimental import pallas as pl
from jax.experimental.pallas import tpu as pltpu

D_MODEL = 2048
N_HEADS = 16
KV_HEADS = 4
HEAD_DIM = 64
GROUP = N_HEADS // KV_HEADS
ATTN_W = N_HEADS * HEAD_DIM
KV_W = KV_HEADS * HEAD_DIM
WINDOW = 128
ATTN_SCALE = HEAD_DIM ** -0.5
NEG_INF = -1e30
N_BUCKETS = 32
MAX_DISTANCE = 128
CONV_W = D_MODEL // 4
CONV_K = 3
POOL_W = D_MODEL // 4
POOL_WINDOWS = (2, 4, 8, 16)
POOL_GROUP = POOL_W // len(POOL_WINDOWS)
POOL_PREV = 15
IN_W = ATTN_W + 2 * KV_W + 3 * CONV_W + POOL_W
D_FF = 5632
PAST_LEN = 16384
LN_EPS = 1e-5

HALO = 16
SUBLANES = 8
DECODE_ROWS = 128 // GROUP
DECODE_SEQS_PER_STEP = 4
PROMPT_TILES_PER_STEP = 2
Q_BLK = 0
K_BLK = ATTN_W // KV_W
V_BLK = K_BLK + 1
GB_BLK = (ATTN_W + 2 * KV_W) // CONV_W
GC_BLK = GB_BLK + 1
H_BLK = GB_BLK + 2
PIN_BLK = GB_BLK + 3

FFN_TM = 1024
FFN_TF = 512
FFN_NS = 4

VMEM_LIMIT = 60 * 1024 * 1024

BF16 = jnp.bfloat16
F32 = jnp.float32


def _params(*sem, flags=None):
    return pltpu.CompilerParams(dimension_semantics=sem, vmem_limit_bytes=VMEM_LIMIT, flags=flags)


def _dot(a, b):
    return jnp.dot(a, b, preferred_element_type=F32)


def _layer_norm(y, g, b):
    mu = jnp.mean(y, -1, keepdims=True)
    yc = y - mu
    var = jnp.mean(yc * yc, -1, keepdims=True)
    return yc * lax.rsqrt(var + LN_EPS) * g + b


def _head_block_perm(blk):
    return (blk % GROUP) * KV_HEADS + blk // GROUP


def _cast_w_in_body(w_ref, o_ref):
    w = w_ref[...]
    order = sorted(range(N_HEADS), key=_head_block_perm)
    q = jnp.concatenate([w[:, h * HEAD_DIM:(h + 1) * HEAD_DIM] for h in order], axis=1)
    o_ref[...] = jnp.concatenate([q, w[:, ATTN_W:]], axis=1).astype(BF16)


def _cast_w_in(w, rows):
    depth, k, n = w.shape
    return pl.pallas_call(
        _cast_w_in_body,
        grid=(depth, k // rows),
        in_specs=[pl.BlockSpec((None, rows, n), lambda l, i: (l, i, 0))],
        out_specs=pl.BlockSpec((None, rows, n), lambda l, i: (l, i, 0)),
        out_shape=jax.ShapeDtypeStruct(w.shape, BF16),
        compiler_params=_params("arbitrary", "arbitrary"),
        name="cast_w_in",
    )(w)


def _cast_w_o_body(w_ref, o_ref):
    @pl.when(pl.program_id(1) == 0)
    def _():
        w = w_ref[...]
        order = sorted(range(N_HEADS), key=_head_block_perm)
        o_ref[...] = jnp.concatenate([w[h * HEAD_DIM:(h + 1) * HEAD_DIM] for h in order], axis=0).astype(BF16)

    @pl.when(pl.program_id(1) > 0)
    def _():
        o_ref[...] = w_ref[...].astype(BF16)


def _cast_w_o(w):
    depth, k, n = w.shape
    return pl.pallas_call(
        _cast_w_o_body,
        grid=(depth, k // ATTN_W),
        in_specs=[pl.BlockSpec((None, ATTN_W, n), lambda l, i: (l, i, 0))],
        out_specs=pl.BlockSpec((None, ATTN_W, n), lambda l, i: (l, i, 0)),
        out_shape=jax.ShapeDtypeStruct(w.shape, BF16),
        compiler_params=_params("arbitrary", "arbitrary"),
        name="cast_w_o",
    )(w)


def _matmul_body(x_ref, w_ref, o_ref, xb_ref):
    @pl.when(pl.program_id(1) == 0)
    def _():
        xb_ref[...] = x_ref[...].astype(BF16)

    o_ref[...] = _dot(xb_ref[...], w_ref[...])


def _matmul(x, w, layer, tm, tn, out_col_perm=None):
    m, k = x.shape
    n = w.shape[2]
    out_col_perm = (lambda j: j) if out_col_perm is None else out_col_perm
    return pl.pallas_call(
        _matmul_body,
        grid=(m // tm, n // tn),
        in_specs=[pl.BlockSpec((tm, k), lambda i, j: (i, 0)),
                  pl.BlockSpec((None, k, tn), lambda i, j: (layer, 0, j))],
        out_specs=pl.BlockSpec((tm, tn), lambda i, j: (i, out_col_perm(j))),
        out_shape=jax.ShapeDtypeStruct((m, n), F32),
        scratch_shapes=[pltpu.VMEM((tm, k), BF16)],
        compiler_params=_params("arbitrary", "arbitrary"),
        name="matmul",
    )(x, w)


def _mixers_body(decode, tq, new_rows, pos0, alpha, cast_tf, n, *refs):
    if decode:
        (q_ref, kc_ref, vc_ref, kp_ref, vp_ref, gb_ref, gc_ref, h_ref, pin_ref, us_ref, ps_state_ref,
         bias_ref, sink_ref, cw_ref, pw_ref, ps_ref,
         cat_ref, uo_ref, ko_ref, vo_ref, ext_u, ext_p) = refs
    else:
        (q_ref, kc_ref, vc_ref, kp_ref, vp_ref, gb_ref, gc_ref, h_ref, pin_ref,
         gcp_ref, hp_ref, pinp_ref,
         bias_ref, sink_ref, cw_ref, pw_ref, ps_ref, wo_ref, x_ref, lng_ref, lnb_ref) = refs[:21]
        ext_u, ext_p = refs[-2:]
        cast_src = refs[21:-4 - (len(refs) - 25) // 2]
        x1_ref, uo_ref = refs[21 + len(cast_src):23 + len(cast_src)]
        cast_dst = refs[23 + len(cast_src):-2]
        for k, (src, dst) in enumerate(zip(cast_src, cast_dst)):
            w = src[...]
            if k % 2 == 0:
                w = jnp.concatenate(
                    [w[:, (j % 2) * D_FF + (j // 2) * cast_tf:(j % 2) * D_FF + (j // 2 + 1) * cast_tf]
                     for j in range(2 * (D_FF // cast_tf))], axis=1)
            dst[...] = w.astype(BF16)
    rows = GROUP * tq

    def new_rows_of(ref, nrows=tq):
        x = ref[...]
        if x.shape[0] == nrows:
            return x
        real = lax.broadcasted_iota(jnp.int32, (nrows, x.shape[1]), 0) < x.shape[0]
        return jnp.where(real, jnp.broadcast_to(x, (nrows, x.shape[1])), 0.0)

    nk = 2 * WINDOW
    q = new_rows_of(q_ref)
    qs = jnp.concatenate([q[:, g * KV_W:(g + 1) * KV_W] for g in range(GROUP)], axis=0)
    qs = (qs * ATTN_SCALE).astype(BF16)
    k_all = jnp.concatenate([kp_ref[...], new_rows_of(kc_ref, WINDOW)], axis=0).astype(BF16)
    v_all = jnp.concatenate([vp_ref[...], new_rows_of(vc_ref, WINDOW)], axis=0)
    if decode:
        for cache_ref, new_ref, out_ref in ((kp_ref, kc_ref, ko_ref), (vp_ref, vc_ref, vo_ref)):
            out_ref[0:WINDOW - new_rows] = cache_ref[new_rows:WINDOW]
            out_ref[WINDOW - new_rows:WINDOW] = new_ref[...]
    lane_head = lax.broadcasted_iota(jnp.int32, (nk, KV_W), 1) // HEAD_DIM
    k_cat = jnp.concatenate([jnp.where(lane_head == kvh, k_all, jnp.zeros((), BF16)) for kvh in range(KV_HEADS)],
                            axis=0)
    st_all = lax.dot_general(k_cat, qs, (((1,), (1,)), ((), ())), preferred_element_type=F32)
    kj = lax.broadcasted_iota(jnp.int32, (nk, rows), 0)
    qi = lax.broadcasted_iota(jnp.int32, (nk, rows), 1) & (tq - 1)
    mask = jnp.logical_and(kj > qi, kj <= qi + WINDOW)
    if not decode:
        mask = jnp.logical_and(mask, jnp.logical_or(n > 0, kj >= WINDOW))
    probs = []
    for kvh in range(KV_HEADS):
        s = jnp.where(mask, st_all[kvh * nk:(kvh + 1) * nk] + bias_ref[kvh], NEG_INF)
        sink = sink_ref[kvh]
        m = jnp.maximum(jnp.max(s, 0, keepdims=True), sink)
        p = jnp.exp(s - m)
        den = jnp.sum(p, 0, keepdims=True) + jnp.exp(sink - m)
        probs.append((p * (1.0 / den)).astype(BF16))
    pt_all = jnp.concatenate(probs, axis=0)
    vt = v_all.T
    row_head = lax.broadcasted_iota(jnp.int32, (KV_W, KV_HEADS * nk), 0) // HEAD_DIM
    col_head = lax.broadcasted_iota(jnp.int32, (KV_W, KV_HEADS * nk), 1) // nk
    vt_cat = jnp.where(row_head == col_head, jnp.concatenate([vt] * KV_HEADS, axis=1), 0.0).astype(BF16)
    o = _dot(vt_cat, pt_all).T
    attn = jnp.concatenate([o[g * tq:(g + 1) * tq] for g in range(GROUP)], axis=1)

    u = new_rows_of(gc_ref) * new_rows_of(h_ref)
    if decode:
        ext_u[HALO - (CONV_K - 1):HALO] = us_ref[...]
    else:
        ext_u[0:HALO] = jnp.where(n > 0, gcp_ref[...] * hp_ref[...], 0.0)
    ext_u[HALO:HALO + tq] = u
    cw = cw_ref[...]
    conv = ext_u[HALO - 2:HALO - 2 + tq] * cw[0:1]
    conv = conv + ext_u[HALO - 1:HALO - 1 + tq] * cw[1:2]
    conv = conv + u * cw[2:3]
    c = new_rows_of(gb_ref) * conv
    uo_ref[...] = ext_u[new_rows + HALO - (CONV_K - 1):new_rows + HALO]

    pin = new_rows_of(pin_ref)
    if decode:
        ext_p[0:HALO - POOL_PREV] = jnp.zeros((HALO - POOL_PREV, POOL_W), F32)
        ext_p[HALO - POOL_PREV:HALO] = ps_state_ref[...]
    else:
        ext_p[0:HALO] = jnp.where(n > 0, pinp_ref[...], 0.0)
    ext_p[HALO:HALO + tq] = pin
    pos = pos0 + n * tq + lax.broadcasted_iota(jnp.int32, (tq, 1), 0)
    assert POOL_WINDOWS == tuple(2 << g for g in range(len(POOL_WINDOWS)))
    sums = ext_p[...]
    trailing = []
    for w in POOL_WINDOWS:
        sums = sums + pltpu.roll(sums, w // 2, 0)
        trailing.append(sums[HALO:HALO + tq, 0:POOL_GROUP])
        sums = sums[:, POOL_GROUP:]
    pooled = []
    for g, w in enumerate(POOL_WINDOWS):
        lo = g * POOL_GROUP
        cur = pin[:, lo:lo + POOL_GROUP]
        win = trailing[g]
        cnt = jnp.minimum(pos + 1, w).astype(F32)
        d = (win / cnt - cur).astype(BF16)
        pooled.append(_dot(d, pw_ref[g]))
    pm = jnp.concatenate(pooled, axis=1) * ps_ref[...]

    cat = jnp.concatenate([attn, c, pm], axis=1).astype(BF16)
    if decode:
        cat_ref[...] = cat[0:new_rows]
    else:
        x1_ref[...] = _layer_norm(alpha * x_ref[...] + _dot(cat, wo_ref[...]), lng_ref[...], lnb_ref[...])


def _mixers_decode_body(nseq, tq, *refs):
    n_in, n_shared, n_out = 11, 5, 4
    for s in range(nseq):
        own = lambda group: [r.at[s] for r in group]
        _mixers_body(True, tq, 1, PAST_LEN, None, None, 0, *own(refs[:n_in]), *refs[n_in:n_in + n_shared],
                     *own(refs[n_in + n_shared:n_in + n_shared + n_out]), *own(refs[n_in + n_shared + n_out:]))


def _mixers_prompt_body(ntile, tq, alpha, cast_tf, *refs):
    (q, kc, vc, kp, vp, gb, gc, h, pin, gcp, hp, pinp) = refs[:12]
    shared = refs[12:21]
    x_ref = shared[6]
    ncast = (len(refs) - 25) // 2
    cast_src = refs[21:21 + ncast]
    x1, uo = refs[21 + ncast:23 + ncast]
    cast_dst = refs[23 + ncast:-2]
    ext_u, ext_p = refs[-2:]
    n0 = pl.program_id(1) * ntile
    for j in range(ntile):
        rows = lambda ref, j=j: ref.at[pl.ds(j * tq, tq)]
        tail = lambda ref, j=j: ref.at[pl.ds(j * tq - HALO, HALO)]
        if j == 0:
            earlier = [kp, vp, gcp, hp, pinp]
        else:
            earlier = [rows(kc, j - 1), rows(vc, j - 1), tail(gc), tail(h), tail(pin)]
        tile_shared = shared[:6] + (rows(x_ref),) + shared[7:]
        _mixers_body(False, tq, tq, 0, alpha, cast_tf if j == 0 else None, n0 + j,
                     rows(q), rows(kc), rows(vc), earlier[0], earlier[1], rows(gb), rows(gc), rows(h), rows(pin),
                     *earlier[2:], *tile_shared, *(cast_src if j == 0 else ()), rows(x1), uo,
                     *(cast_dst if j == 0 else ()), ext_u.at[j], ext_p.at[j])


def _mixers_prompt(z, tables, cw, pw, ps, w_o, layer, x, ln_g, ln_b, alpha, cast_next=None):
    b, s, _ = z.shape
    tq = WINDOW
    ntile = PROMPT_TILES_PER_STEP
    rows = ntile * tq
    steps = s // rows
    bias, sink = tables
    cur = lambda blk: (lambda i, n: (i, n, blk))
    prev = lambda blk: (lambda i, n: (i, jnp.maximum(n * ntile - 1, 0), blk))
    halo = lambda blk: (lambda i, n: (i, jnp.maximum(n * (rows // HALO) - 1, 0), blk))
    const3 = lambda i, n: (0, 0, 0)
    const2 = lambda i, n: (0, 0)
    in_specs = [
        pl.BlockSpec((None, rows, ATTN_W), cur(Q_BLK)),
        pl.BlockSpec((None, rows, KV_W), cur(K_BLK)),
        pl.BlockSpec((None, rows, KV_W), cur(V_BLK)),
        pl.BlockSpec((None, tq, KV_W), prev(K_BLK)),
        pl.BlockSpec((None, tq, KV_W), prev(V_BLK)),
        pl.BlockSpec((None, rows, CONV_W), cur(GB_BLK)),
        pl.BlockSpec((None, rows, CONV_W), cur(GC_BLK)),
        pl.BlockSpec((None, rows, CONV_W), cur(H_BLK)),
        pl.BlockSpec((None, rows, POOL_W), cur(PIN_BLK)),
        pl.BlockSpec((None, HALO, CONV_W), halo(GC_BLK)),
        pl.BlockSpec((None, HALO, CONV_W), halo(H_BLK)),
        pl.BlockSpec((None, HALO, POOL_W), halo(PIN_BLK)),
        pl.BlockSpec(bias.shape, const3),
        pl.BlockSpec(sink.shape, const3),
        pl.BlockSpec(cw.shape, const2),
        pl.BlockSpec(pw.shape, const3),
        pl.BlockSpec(ps.shape, const2),
        pl.BlockSpec((None, D_MODEL, D_MODEL), lambda i, n: (layer, 0, 0), pipeline_mode=pl.Buffered(1)),
        pl.BlockSpec((None, rows, D_MODEL), lambda i, n: (i, n, 0)),
        pl.BlockSpec((None, 1, D_MODEL), lambda i, n: (layer, 0, 0)),
        pl.BlockSpec((None, 1, D_MODEL), lambda i, n: (layer, 0, 0)),
    ]
    operands = [z] * 12 + [bias, sink, cw, pw, ps, w_o, x, ln_g, ln_b]
    out_specs = [pl.BlockSpec((None, rows, D_MODEL), lambda i, n: (i, n, 0)),
                 pl.BlockSpec((None, CONV_K - 1, CONV_W), lambda i, n: (i, 0, 0))]
    out_shape = [jax.ShapeDtypeStruct((b, s, D_MODEL), F32),
                 jax.ShapeDtypeStruct((b, CONV_K - 1, CONV_W), F32)]
    cast_tf = None
    if cast_next is not None:
        w_up, w_down, cast_layers, cast_tf = cast_next
        for cast_layer, w in [(cl, w) for cl in cast_layers for w in (w_up, w_down)]:
            _, k, width = w.shape
            wrows = next(r for r in range(2 * SUBLANES, k + 1, 2 * SUBLANES) if k % r == 0 and k // r <= b * steps)
            nblocks = k // wrows
            index = lambda i, n, nblocks=nblocks: (jnp.minimum(i * steps + n, nblocks - 1), 0)
            in_specs.append(pl.BlockSpec((None, wrows, width),
                                         lambda i, n, index=index, cl=cast_layer: (cl,) + index(i, n)))
            operands.append(w)
            out_specs.append(pl.BlockSpec((None, wrows, width), lambda i, n, index=index: (0,) + index(i, n)))
            out_shape.append(jax.ShapeDtypeStruct((1, k, width), BF16))
    return pl.pallas_call(
        functools.partial(_mixers_prompt_body, ntile, tq, alpha, cast_tf),
        grid=(b, steps),
        in_specs=in_specs,
        out_specs=out_specs,
        out_shape=out_shape,
        scratch_shapes=[pltpu.VMEM((ntile, HALO + tq, CONV_W), F32), pltpu.VMEM((ntile, HALO + tq, POOL_W), F32)],
        compiler_params=_params("arbitrary", "arbitrary"),
        name="mixers_prompt",
    )(*operands)


def _mixers_decode(zs, ck, cv, conv_state, pool_state, tables, cw, pw, ps):
    b = zs.shape[0]
    tq = DECODE_ROWS
    nseq = DECODE_SEQS_PER_STEP
    bias, sink = tables
    new = lambda width, blk: pl.BlockSpec((nseq, 1, width), lambda i, n: (i, 0, blk))
    whole = lambda rows, width: pl.BlockSpec((nseq, rows, width), lambda i, n: (i, 0, 0))
    const3 = lambda i, n: (0, 0, 0)
    const2 = lambda i, n: (0, 0)
    in_specs = [
        new(ATTN_W, Q_BLK), new(KV_W, K_BLK), new(KV_W, V_BLK),
        whole(WINDOW, KV_W), whole(WINDOW, KV_W),
        new(CONV_W, GB_BLK), new(CONV_W, GC_BLK), new(CONV_W, H_BLK), new(POOL_W, PIN_BLK),
        whole(CONV_K - 1, CONV_W), whole(POOL_PREV, POOL_W),
        pl.BlockSpec(bias.shape, const3),
        pl.BlockSpec(sink.shape, const3),
        pl.BlockSpec(cw.shape, const2),
        pl.BlockSpec(pw.shape, const3),
        pl.BlockSpec(ps.shape, const2),
    ]
    return pl.pallas_call(
        functools.partial(_mixers_decode_body, nseq, tq),
        grid=(b // nseq, 1),
        in_specs=in_specs,
        out_specs=[whole(1, D_MODEL), whole(CONV_K - 1, CONV_W), whole(WINDOW, KV_W), whole(WINDOW, KV_W)],
        out_shape=[jax.ShapeDtypeStruct((b, 1, D_MODEL), BF16),
                   jax.ShapeDtypeStruct((b, CONV_K - 1, CONV_W), F32),
                   jax.ShapeDtypeStruct((b, WINDOW, KV_W), F32),
                   jax.ShapeDtypeStruct((b, WINDOW, KV_W), F32)],
        scratch_shapes=[pltpu.VMEM((nseq, HALO + tq, CONV_W), F32), pltpu.VMEM((nseq, HALO + tq, POOL_W), F32)],
        compiler_params=_params("arbitrary", "arbitrary"),
        name="mixers_decode",
    )(zs, zs, zs, ck, cv, zs, zs, zs, zs, conv_state, pool_state, bias, sink, cw, pw, ps)


def _bias_table(bias_by_dist, tq):
    nk = 2 * WINDOW
    by_offset = jnp.zeros((N_HEADS, nk + 1), F32).at[:, 1:WINDOW + 1].set(bias_by_dist[::-1].T)
    rows = jnp.tile(by_offset, (1, tq))[:, :tq * nk].reshape(N_HEADS, tq, nk)
    return jnp.swapaxes(rows.reshape(KV_HEADS, GROUP * tq, nk), 1, 2)


def _sink_table(sinks_l, tq):
    return jnp.broadcast_to(sinks_l.astype(F32).reshape(KV_HEADS, GROUP, 1, 1),
                            (KV_HEADS, GROUP, 1, tq)).reshape(KV_HEADS, 1, GROUP * tq)


def _t5_bucket(n):
    max_exact = N_BUCKETS // 2
    nf = jnp.maximum(n, 1).astype(F32)
    large = max_exact + (jnp.log(nf / max_exact) / math.log(MAX_DISTANCE / max_exact)
                         * (N_BUCKETS - max_exact)).astype(jnp.int32)
    large = jnp.minimum(large, N_BUCKETS - 1)
    return jnp.where(n < max_exact, n, large)


def _proj_ln_body(nk, alpha, a_ref, w_ref, x_ref, g_ref, b_ref, o_ref, *scratch):
    part = _dot(a_ref[...], w_ref[...])
    if nk == 1:
        o_ref[...] = _layer_norm(alpha * x_ref[...] + part, g_ref[...], b_ref[...])
        return
    acc_ref, = scratch
    k = pl.program_id(1)

    @pl.when(k == 0)
    def _():
        acc_ref[...] = jnp.zeros_like(acc_ref)

    acc_ref[...] += part

    @pl.when(k == nk - 1)
    def _():
        o_ref[...] = _layer_norm(alpha * x_ref[...] + acc_ref[...], g_ref[...], b_ref[...])


def _proj_ln(a, w, w_layer, x, g, b, layer, alpha, tm, tk):
    m, k = a.shape
    n = w.shape[2]
    nk = k // tk
    return pl.pallas_call(
        functools.partial(_proj_ln_body, nk, alpha),
        grid=(m // tm, nk),
        in_specs=[pl.BlockSpec((tm, tk), lambda i, kk: (i, kk)),
                  pl.BlockSpec((None, tk, n), lambda i, kk: (w_layer, kk, 0)),
                  pl.BlockSpec((tm, n), lambda i, kk: (i, 0)),
                  pl.BlockSpec((None, 1, n), lambda i, kk: (layer, 0, 0)),
                  pl.BlockSpec((None, 1, n), lambda i, kk: (layer, 0, 0))],
        out_specs=pl.BlockSpec((tm, n), lambda i, kk: (i, 0)),
        out_shape=jax.ShapeDtypeStruct((m, n), F32),
        scratch_shapes=[] if nk == 1 else [pltpu.VMEM((tm, n), F32)],
        compiler_params=_params("arbitrary", "arbitrary"),
        name="proj_ln",
    )(a, w, x, g, b)


def _ffn_body(tm, tf, ns, tiles_per_seq, nc, alpha, x_ref, wu_ref, wd_ref, cwg_ref, cwv_ref, g_ref, b_ref,
              o_ref, tail_g_ref, tail_v_ref, xb_ref, ext, carry):
    i = pl.program_id(0)
    c = pl.program_id(1)
    ts = tm // ns

    @pl.when(c == 0)
    def _():
        xb_ref[...] = x_ref[...].astype(BF16)
        o_ref[...] = jnp.zeros_like(o_ref)

    seq_start = (i % tiles_per_seq) == 0

    def up_proj(h, prev_tail):
        up = _dot(xb_ref[pl.ds(h * ts, ts)], wu_ref[...])
        if h == 0:
            @pl.when(seq_start)
            def _():
                ext[0, 0:SUBLANES] = jnp.zeros((SUBLANES, 2 * tf), F32)

            @pl.when(jnp.logical_not(seq_start))
            def _():
                ext[0, 0:SUBLANES] = carry[c]
        else:
            ext[h, 0:SUBLANES] = prev_tail
        ext[h, SUBLANES:SUBLANES + ts] = up
        tail = up[ts - SUBLANES:ts]
        if h == ns - 1:
            carry[c] = tail
            tail_g_ref[...] = tail[:, :tf]
            tail_v_ref[...] = tail[:, tf:]
        return tail

    def gate(h):
        cw = jnp.concatenate([cwg_ref[...], cwv_ref[...]], axis=1)
        hc = ext[h, SUBLANES - 2:SUBLANES - 2 + ts] * cw[0:1]
        hc = hc + ext[h, SUBLANES - 1:SUBLANES - 1 + ts] * cw[1:2]
        hc = hc + ext[h, SUBLANES:SUBLANES + ts] * cw[2:3]
        return (jax.nn.silu(hc[:, :tf]) * hc[:, tf:]).astype(BF16)

    def down_proj(h, act):
        o_ref[pl.ds(h * ts, ts)] += _dot(act, wd_ref[...])

    tail = up_proj(0, None)
    for h in range(ns):
        act = gate(h)
        if h + 1 < ns:
            tail = up_proj(h + 1, tail)
        down_proj(h, act)

    @pl.when(c == nc - 1)
    def _():
        o_ref[...] = _layer_norm(alpha * x_ref[...] + o_ref[...], g_ref[...], b_ref[...])


def _ffn_prompt(x, w_up_il, w_down, w_layer, cw, g, b, layer, alpha, seq, tm, tf, ns):
    m = x.shape[0]
    nc = D_FF // tf
    tiles_per_seq = seq // tm
    tail_spec = pl.BlockSpec((None, SUBLANES, tf), lambda i, c: (i, 0, c))
    tail_shape = jax.ShapeDtypeStruct((m // tm, SUBLANES, D_FF), F32)
    out, tail_g, tail_v = pl.pallas_call(
        functools.partial(_ffn_body, tm, tf, ns, tiles_per_seq, nc, alpha),
        grid=(m // tm, nc),
        in_specs=[pl.BlockSpec((tm, D_MODEL), lambda i, c: (i, 0), pipeline_mode=pl.Buffered(1)),
                  pl.BlockSpec((None, D_MODEL, 2 * tf), lambda i, c: (w_layer, 0, c)),
                  pl.BlockSpec((None, tf, D_MODEL), lambda i, c: (w_layer, c, 0)),
                  pl.BlockSpec((None, CONV_K, tf), lambda i, c: (layer, 0, c)),
                  pl.BlockSpec((None, CONV_K, tf), lambda i, c: (layer, 0, nc + c)),
                  pl.BlockSpec((None, 1, D_MODEL), lambda i, c: (layer, 0, 0)),
                  pl.BlockSpec((None, 1, D_MODEL), lambda i, c: (layer, 0, 0))],
        out_specs=[pl.BlockSpec((tm, D_MODEL), lambda i, c: (i, 0)), tail_spec, tail_spec],
        out_shape=[jax.ShapeDtypeStruct((m, D_MODEL), F32), tail_shape, tail_shape],
        scratch_shapes=[pltpu.VMEM((tm, D_MODEL), BF16),
                        pltpu.VMEM((ns, SUBLANES + tm // ns, 2 * tf), F32),
                        pltpu.VMEM((nc, SUBLANES, 2 * tf), F32)],
        compiler_params=_params("arbitrary", "arbitrary"),
        name="ffn_prompt",
    )(x, w_up_il, w_down, cw, cw, g, b)
    last = slice(tiles_per_seq - 1, None, tiles_per_seq)
    return out, jnp.concatenate([tail_g[last], tail_v[last]], -1)


def _gate_decode_body(ug_ref, uv_ref, sg_ref, sv_ref, cwg_ref, cwv_ref, o_ref):
    def conv(u_ref, s_ref, cw_ref):
        cw = cw_ref[...]
        y = s_ref[:, 0, :] * cw[0:1]
        y = y + s_ref[:, 1, :] * cw[1:2]
        return y + u_ref[...] * cw[2:3]

    hg = conv(ug_ref, sg_ref, cwg_ref)
    hv = conv(uv_ref, sv_ref, cwv_ref)
    o_ref[...] = (jax.nn.silu(hg) * hv).astype(BF16)


def _gate_decode(up, state, cw, layer, tf):
    b = up.shape[0]
    nc = D_FF // tf
    return pl.pallas_call(
        _gate_decode_body,
        grid=(nc,),
        in_specs=[pl.BlockSpec((b, tf), lambda c: (0, c)),
                  pl.BlockSpec((b, tf), lambda c: (0, nc + c)),
                  pl.BlockSpec((b, CONV_K - 1, tf), lambda c: (0, 0, c)),
                  pl.BlockSpec((b, CONV_K - 1, tf), lambda c: (0, 0, nc + c)),
                  pl.BlockSpec((None, CONV_K, tf), lambda c: (layer, 0, c)),
                  pl.BlockSpec((None, CONV_K, tf), lambda c: (layer, 0, nc + c))],
        out_specs=pl.BlockSpec((b, tf), lambda c: (0, c)),
        out_shape=jax.ShapeDtypeStruct((b, D_FF), BF16),
        compiler_params=_params("arbitrary"),
        name="gate_decode",
    )(up, up, state, state, cw, cw)


def kernel(x_prompt, x_sample, cache_k, cache_v, state_conv, state_pool, state_ffn, rel_table, w_in, conv_w,
           pool_w, pool_scale, sinks, w_o, ln1_g, ln1_b, w_up, ffn_conv_w, w_down, ln2_g, ln2_b):
    depth = w_in.shape[0]
    bp_, seq, _ = x_prompt.shape
    bs = x_sample.shape[0]
    assert x_sample.shape[1] == 1 and cache_k.shape[2] == WINDOW
    alpha = (2 * depth) ** 0.25
    nc = D_FF // FFN_TF

    w_in_b = _cast_w_in(w_in, 256)
    w_o_b = _cast_w_o(w_o)
    w_up_il, w_down_b = [], []
    natural_col = lambda j: jnp.where(j % 2 == 0, j // 2, nc + j // 2)
    pool_w_b = pool_w.astype(BF16)

    bias_by_dist = rel_table.astype(F32)[_t5_bucket(jnp.arange(WINDOW))]
    bias_p = _bias_table(bias_by_dist, WINDOW)
    bias_s = _bias_table(bias_by_dist, DECODE_ROWS)

    ln1_g, ln1_b, ln2_g, ln2_b = (t[:, None, :] for t in (ln1_g, ln1_b, ln2_g, ln2_b))

    xp = x_prompt.reshape(bp_ * seq, D_MODEL)
    xs = x_sample.reshape(bs, D_MODEL)

    outs_p, outs_s = [], []
    for l in range(depth):
        tab_p = (bias_p, _sink_table(sinks[l], WINDOW))
        tab_s = (bias_s, _sink_table(sinks[l], DECODE_ROWS))
        ps = pool_scale[l][None, :]

        z = _matmul(xp, w_in_b, l, 1024, IN_W // 2).reshape(bp_, seq, IN_W)
        cast_layers = [cl for cl in ((0, 1) if l == 0 else (l + 1,)) if cl < depth]
        cast_next = (w_up, w_down, cast_layers, FFN_TF) if cast_layers else None
        x1, u_tail, *next_w = _mixers_prompt(z, tab_p, conv_w[l], pool_w_b[l], ps, w_o_b, l,
                                             xp.reshape(bp_, seq, D_MODEL), ln1_g, ln1_b, alpha, cast_next)
        w_up_il.extend(next_w[0::2])
        w_down_b.extend(next_w[1::2])
        x1 = x1.reshape(bp_ * seq, D_MODEL)
        xp, up_tail = _ffn_prompt(x1, w_up_il[l], w_down_b[l], 0, ffn_conv_w, ln2_g, ln2_b, l, alpha, seq,
                                  FFN_TM, FFN_TF, FFN_NS)
        outs_p.append((
            z[:, seq - WINDOW:, ATTN_W:ATTN_W + KV_W].reshape(bp_, WINDOW, KV_HEADS, HEAD_DIM),
            z[:, seq - WINDOW:, ATTN_W + KV_W:ATTN_W + 2 * KV_W].reshape(bp_, WINDOW, KV_HEADS, HEAD_DIM),
            u_tail,
            z[:, seq - POOL_PREV:, IN_W - POOL_W:],
            up_tail[:, SUBLANES - (CONV_K - 1):],
        ))

        zs = _matmul(xs, w_in_b, l, bs, 512)
        ck = cache_k[l].reshape(bs, WINDOW, KV_W)
        cv = cache_v[l].reshape(bs, WINDOW, KV_W)
        cat_s, u_new, k_s, v_s = _mixers_decode(zs[:, None, :], ck, cv, state_conv[l], state_pool[l], tab_s,
                                                conv_w[l], pool_w_b[l], ps)
        x1s = _proj_ln(cat_s.reshape(bs, D_MODEL), w_o_b, l, xs, ln1_g, ln1_b, l, alpha, bs, D_MODEL)
        up_s = _matmul(x1s, w_up_il[l], 0, bs, FFN_TF, natural_col)
        act_s = _gate_decode(up_s, state_ffn[l], ffn_conv_w, l, FFN_TF)
        xs = _proj_ln(act_s, w_down_b[l], 0, x1s, ln2_g, ln2_b, l, alpha, bs, 512)
        outs_s.append((
            k_s.reshape(bs, WINDOW, KV_HEADS, HEAD_DIM),
            v_s.reshape(bs, WINDOW, KV_HEADS, HEAD_DIM),
            u_new,
            jnp.concatenate([state_pool[l][:, 1:], zs[:, None, IN_W - POOL_W:]], 1),
            jnp.concatenate([state_ffn[l][:, 1:], up_s[:, None, :]], 1),
        ))

    st = lambda lst, i: jnp.stack([e[i] for e in lst], 0)
    return (xp.reshape(bp_, seq, D_MODEL), xs.reshape(bs, 1, D_MODEL),
            st(outs_p, 0), st(outs_p, 1), st(outs_p, 2), st(outs_p, 3), st(outs_p, 4),
            st(outs_s, 0), st(outs_s, 1), st(outs_s, 2), st(outs_s, 3), st(outs_s, 4))
```

```python
import functools
import math

import jax
import jax.numpy as jnp
from jax import lax
from jax.experimental import pallas as pl
from jax.experimental.pallas import tpu as pltpu

D_MODEL = 2048
N_HEADS = 16
KV_HEADS = 4
HEAD_DIM = 64
GROUP = N_HEADS // KV_HEADS
ATTN_W = N_HEADS * HEAD_DIM
KV_W = KV_HEADS * HEAD_DIM
WINDOW = 128
ATTN_SCALE = HEAD_DIM ** -0.5
NEG_INF = -1e30
N_BUCKETS = 32
MAX_DISTANCE = 128
CONV_W = D_MODEL // 4
CONV_K = 3
POOL_W = D_MODEL // 4
POOL_WINDOWS = (2, 4, 8, 16)
POOL_GROUP = POOL_W // len(POOL_WINDOWS)
POOL_PREV = 15
IN_W = ATTN_W + 2 * KV_W + 3 * CONV_W + POOL_W
D_FF = 5632
PAST_LEN = 16384
LN_EPS = 1e-5

HALO = 16
SUBLANES = 8
DECODE_ROWS = 128 // GROUP
DECODE_SEQS_PER_STEP = 4
PROMPT_TILES_PER_STEP = 2
Q_BLK = 0
K_BLK = ATTN_W // KV_W
V_BLK = K_BLK + 1
GB_BLK = (ATTN_W + 2 * KV_W) // CONV_W
GC_BLK = GB_BLK + 1
H_BLK = GB_BLK + 2
PIN_BLK = GB_BLK + 3

FFN_TM = 1024
FFN_TF = 512
FFN_NS = 4

VMEM_LIMIT = 60 * 1024 * 1024

BF16 = jnp.bfloat16
F32 = jnp.float32


def _params(*sem, flags=None):
    return pltpu.CompilerParams(dimension_semantics=sem, vmem_limit_bytes=VMEM_LIMIT, flags=flags)


def _dot(a, b):
    return jnp.dot(a, b, preferred_element_type=F32)


def _layer_norm(y, g, b):
    mu = jnp.mean(y, -1, keepdims=True)
    yc = y - mu
    var = jnp.mean(yc * yc, -1, keepdims=True)
    return yc * lax.rsqrt(var + LN_EPS) * g + b


def _head_block_perm(blk):
    return (blk % GROUP) * KV_HEADS + blk // GROUP


def _cast_w_in_body(w_ref, o_ref):
    w = w_ref[...]
    order = sorted(range(N_HEADS), key=_head_block_perm)
    q = jnp.concatenate([w[:, h * HEAD_DIM:(h + 1) * HEAD_DIM] for h in order], axis=1)
    o_ref[...] = jnp.concatenate([q, w[:, ATTN_W:]], axis=1).astype(BF16)


def _cast_w_in(w, rows):
    depth, k, n = w.shape
    return pl.pallas_call(
        _cast_w_in_body,
        grid=(depth, k // rows),
        in_specs=[pl.BlockSpec((None, rows, n), lambda l, i: (l, i, 0))],
        out_specs=pl.BlockSpec((None, rows, n), lambda l, i: (l, i, 0)),
        out_shape=jax.ShapeDtypeStruct(w.shape, BF16),
        compiler_params=_params("arbitrary", "arbitrary"),
        name="cast_w_in",
    )(w)


def _cast_w_o_body(w_ref, o_ref):
    @pl.when(pl.program_id(1) == 0)
    def _():
        w = w_ref[...]
        order = sorted(range(N_HEADS), key=_head_block_perm)
        o_ref[...] = jnp.concatenate([w[h * HEAD_DIM:(h + 1) * HEAD_DIM] for h in order], axis=0).astype(BF16)

    @pl.when(pl.program_id(1) > 0)
    def _():
        o_ref[...] = w_ref[...].astype(BF16)


def _cast_w_o(w):
    depth, k, n = w.shape
    return pl.pallas_call(
        _cast_w_o_body,
        grid=(depth, k // ATTN_W),
        in_specs=[pl.BlockSpec((None, ATTN_W, n), lambda l, i: (l, i, 0))],
        out_specs=pl.BlockSpec((None, ATTN_W, n), lambda l, i: (l, i, 0)),
        out_shape=jax.ShapeDtypeStruct(w.shape, BF16),
        compiler_params=_params("arbitrary", "arbitrary"),
        name="cast_w_o",
    )(w)


def _matmul_body(x_ref, w_ref, o_ref, xb_ref):
    @pl.when(pl.program_id(1) == 0)
    def _():
        xb_ref[...] = x_ref[...].astype(BF16)

    o_ref[...] = _dot(xb_ref[...], w_ref[...])


def _matmul(x, w, layer, tm, tn, out_col_perm=None):
    m, k = x.shape
    n = w.shape[2]
    out_col_perm = (lambda j: j) if out_col_perm is None else out_col_perm
    return pl.pallas_call(
        _matmul_body,
        grid=(m // tm, n // tn),
        in_specs=[pl.BlockSpec((tm, k), lambda i, j: (i, 0)),
                  pl.BlockSpec((None, k, tn), lambda i, j: (layer, 0, j))],
        out_specs=pl.BlockSpec((tm, tn), lambda i, j: (i, out_col_perm(j))),
        out_shape=jax.ShapeDtypeStruct((m, n), F32),
        scratch_shapes=[pltpu.VMEM((tm, k), BF16)],
        compiler_params=_params("arbitrary", "arbitrary"),
        name="matmul",
    )(x, w)


def _mixers_body(decode, tq, new_rows, pos0, alpha, cast_tf, n, *refs):
    if decode:
        (q_ref, kc_ref, vc_ref, kp_ref, vp_ref, gb_ref, gc_ref, h_ref, pin_ref, us_ref, ps_state_ref,
         bias_ref, sink_ref, cw_ref, pw_ref, ps_ref,
         cat_ref, uo_ref, ko_ref, vo_ref, ext_u, ext_p) = refs
    else:
        (q_ref, kc_ref, vc_ref, kp_ref, vp_ref, gb_ref, gc_ref, h_ref, pin_ref,
         gcp_ref, hp_ref, pinp_ref,
         bias_ref, sink_ref, cw_ref, pw_ref, ps_ref, wo_ref, x_ref, lng_ref, lnb_ref) = refs[:21]
        ext_u, ext_p = refs[-2:]
        cast_src = refs[21:-4 - (len(refs) - 25) // 2]
        x1_ref, uo_ref = refs[21 + len(cast_src):23 + len(cast_src)]
        cast_dst = refs[23 + len(cast_src):-2]
        for k, (src, dst) in enumerate(zip(cast_src, cast_dst)):
            w = src[...]
            if k % 2 == 0:
                w = jnp.concatenate(
                    [w[:, (j % 2) * D_FF + (j // 2) * cast_tf:(j % 2) * D_FF + (j // 2 + 1) * cast_tf]
                     for j in range(2 * (D_FF // cast_tf))], axis=1)
            dst[...] = w.astype(BF16)
    rows = GROUP * tq

    def new_rows_of(ref, nrows=tq):
        x = ref[...]
        if x.shape[0] == nrows:
            return x
        real = lax.broadcasted_iota(jnp.int32, (nrows, x.shape[1]), 0) < x.shape[0]
        return jnp.where(real, jnp.broadcast_to(x, (nrows, x.shape[1])), 0.0)

    nk = 2 * WINDOW
    q = new_rows_of(q_ref)
    qs = jnp.concatenate([q[:, g * KV_W:(g + 1) * KV_W] for g in range(GROUP)], axis=0)
    qs = (qs * ATTN_SCALE).astype(BF16)
    k_all = jnp.concatenate([kp_ref[...], new_rows_of(kc_ref, WINDOW)], axis=0).astype(BF16)
    v_all = jnp.concatenate([vp_ref[...], new_rows_of(vc_ref, WINDOW)], axis=0)
    if decode:
        for cache_ref, new_ref, out_ref in ((kp_ref, kc_ref, ko_ref), (vp_ref, vc_ref, vo_ref)):
            out_ref[0:WINDOW - new_rows] = cache_ref[new_rows:WINDOW]
            out_ref[WINDOW - new_rows:WINDOW] = new_ref[...]
    lane_head = lax.broadcasted_iota(jnp.int32, (nk, KV_W), 1) // HEAD_DIM
    k_cat = jnp.concatenate([jnp.where(lane_head == kvh, k_all, jnp.zeros((), BF16)) for kvh in range(KV_HEADS)],
                            axis=0)
    st_all = lax.dot_general(k_cat, qs, (((1,), (1,)), ((), ())), preferred_element_type=F32)
    kj = lax.broadcasted_iota(jnp.int32, (nk, rows), 0)
    qi = lax.broadcasted_iota(jnp.int32, (nk, rows), 1) & (tq - 1)
    mask = jnp.logical_and(kj > qi, kj <= qi + WINDOW)
    if not decode:
        mask = jnp.logical_and(mask, jnp.logical_or(n > 0, kj >= WINDOW))
    probs = []
    for kvh in range(KV_HEADS):
        s = jnp.where(mask, st_all[kvh * nk:(kvh + 1) * nk] + bias_ref[kvh], NEG_INF)
        sink = sink_ref[kvh]
        m = jnp.maximum(jnp.max(s, 0, keepdims=True), sink)
        p = jnp.exp(s - m)
        den = jnp.sum(p, 0, keepdims=True) + jnp.exp(sink - m)
        probs.append((p * (1.0 / den)).astype(BF16))
    pt_all = jnp.concatenate(probs, axis=0)
    vt = v_all.T
    row_head = lax.broadcasted_iota(jnp.int32, (KV_W, KV_HEADS * nk), 0) // HEAD_DIM
    col_head = lax.broadcasted_iota(jnp.int32, (KV_W, KV_HEADS * nk), 1) // nk
    vt_cat = jnp.where(row_head == col_head, jnp.concatenate([vt] * KV_HEADS, axis=1), 0.0).astype(BF16)
    o = _dot(vt_cat, pt_all).T
    attn = jnp.concatenate([o[g * tq:(g + 1) * tq] for g in range(GROUP)], axis=1)

    u = new_rows_of(gc_ref) * new_rows_of(h_ref)
    if decode:
        ext_u[HALO - (CONV_K - 1):HALO] = us_ref[...]
    else:
        ext_u[0:HALO] = jnp.where(n > 0, gcp_ref[...] * hp_ref[...], 0.0)
    ext_u[HALO:HALO + tq] = u
    cw = cw_ref[...]
    conv = ext_u[HALO - 2:HALO - 2 + tq] * cw[0:1]
    conv = conv + ext_u[HALO - 1:HALO - 1 + tq] * cw[1:2]
    conv = conv + u * cw[2:3]
    c = new_rows_of(gb_ref) * conv
    uo_ref[...] = ext_u[new_rows + HALO - (CONV_K - 1):new_rows + HALO]

    pin = new_rows_of(pin_ref)
    if decode:
        ext_p[0:HALO - POOL_PREV] = jnp.zeros((HALO - POOL_PREV, POOL_W), F32)
        ext_p[HALO - POOL_PREV:HALO] = ps_state_ref[...]
    else:
        ext_p[0:HALO] = jnp.where(n > 0, pinp_ref[...], 0.0)
    ext_p[HALO:HALO + tq] = pin
    pos = pos0 + n * tq + lax.broadcasted_iota(jnp.int32, (tq, 1), 0)
    assert POOL_WINDOWS == tuple(2 << g for g in range(len(POOL_WINDOWS)))
    sums = ext_p[...]
    trailing = []
    for w in POOL_WINDOWS:
        sums = sums + pltpu.roll(sums, w // 2, 0)
        trailing.append(sums[HALO:HALO + tq, 0:POOL_GROUP])
        sums = sums[:, POOL_GROUP:]
    pooled = []
    for g, w in enumerate(POOL_WINDOWS):
        lo = g * POOL_GROUP
        cur = pin[:, lo:lo + POOL_GROUP]
        win = trailing[g]
        cnt = jnp.minimum(pos + 1, w).astype(F32)
        d = (win / cnt - cur).astype(BF16)
        pooled.append(_dot(d, pw_ref[g]))
    pm = jnp.concatenate(pooled, axis=1) * ps_ref[...]

    cat = jnp.concatenate([attn, c, pm], axis=1).astype(BF16)
    if decode:
        cat_ref[...] = cat[0:new_rows]
    else:
        x1_ref[...] = _layer_norm(alpha * x_ref[...] + _dot(cat, wo_ref[...]), lng_ref[...], lnb_ref[...])


def _mixers_decode_body(nseq, tq, *refs):
    n_in, n_shared, n_out = 11, 5, 4
    for s in range(nseq):
        own = lambda group: [r.at[s] for r in group]
        _mixers_body(True, tq, 1, PAST_LEN, None, None, 0, *own(refs[:n_in]), *refs[n_in:n_in + n_shared],
                     *own(refs[n_in + n_shared:n_in + n_shared + n_out]), *own(refs[n_in + n_shared + n_out:]))


def _mixers_prompt_body(ntile, tq, alpha, cast_tf, *refs):
    x_ref, win_ref = refs[:2]
    shared = refs[2:10]
    ncast = (len(refs) - 17) // 2
    cast_src = refs[10:10 + ncast]
    x1, uo, kvo, pino = refs[10 + ncast:14 + ncast]
    cast_dst = refs[14 + ncast:-3]
    z_ref, ext_u, ext_p = refs[-3:]
    n = pl.program_id(1)
    rows = ntile * tq

    @pl.when(n == 0)
    def _():
        z_ref[0:WINDOW] = jnp.zeros((WINDOW, IN_W), F32)

    @pl.when(n > 0)
    def _():
        z_ref[0:WINDOW] = z_ref[rows:rows + WINDOW]

    z_ref[WINDOW:WINDOW + rows] = _dot(x_ref[...].astype(BF16), win_ref[...])
    kvo[...] = z_ref[rows:rows + WINDOW, ATTN_W:ATTN_W + 2 * KV_W]
    pino[...] = z_ref[rows + WINDOW - HALO:rows + WINDOW, IN_W - POOL_W:IN_W]

    def cols(blk, width):
        return pl.ds(blk * width, width)

    for j in range(ntile):
        base = WINDOW + j * tq
        own = lambda blk, width, base=base: z_ref.at[pl.ds(base, tq), cols(blk, width)]
        above = lambda nrows, blk, width, base=base: z_ref.at[pl.ds(base - nrows, nrows), cols(blk, width)]
        tile_shared = shared[:6] + (x_ref.at[pl.ds(j * tq, tq)],) + shared[6:]
        _mixers_body(False, tq, tq, 0, alpha, cast_tf if j == 0 else None, n * ntile + j,
                     own(Q_BLK, ATTN_W), own(K_BLK, KV_W), own(V_BLK, KV_W),
                     above(WINDOW, K_BLK, KV_W), above(WINDOW, V_BLK, KV_W),
                     own(GB_BLK, CONV_W), own(GC_BLK, CONV_W), own(H_BLK, CONV_W), own(PIN_BLK, POOL_W),
                     above(HALO, GC_BLK, CONV_W), above(HALO, H_BLK, CONV_W), above(HALO, PIN_BLK, POOL_W),
                     *tile_shared, *(cast_src if j == 0 else ()), x1.at[pl.ds(j * tq, tq)], uo,
                     *(cast_dst if j == 0 else ()), ext_u.at[j], ext_p.at[j])


def _mixers_prompt(x, w_in, tables, cw, pw, ps, w_o, layer, ln_g, ln_b, alpha, cast_next=None):
    b, s, _ = x.shape
    tq = WINDOW
    ntile = PROMPT_TILES_PER_STEP
    rows = ntile * tq
    steps = s // rows
    bias, sink = tables
    const3 = lambda i, n: (0, 0, 0)
    const2 = lambda i, n: (0, 0)
    per_seq = lambda nrows, width: pl.BlockSpec((None, nrows, width), lambda i, n: (i, 0, 0))
    resident = lambda k, width: pl.BlockSpec((None, k, width), lambda i, n: (layer, 0, 0),
                                             pipeline_mode=pl.Buffered(1))
    in_specs = [
        pl.BlockSpec((None, rows, D_MODEL), lambda i, n: (i, n, 0)),
        resident(D_MODEL, IN_W),
        pl.BlockSpec(bias.shape, const3),
        pl.BlockSpec(sink.shape, const3),
        pl.BlockSpec(cw.shape, const2),
        pl.BlockSpec(pw.shape, const3),
        pl.BlockSpec(ps.shape, const2),
        resident(D_MODEL, D_MODEL),
        pl.BlockSpec((None, 1, D_MODEL), lambda i, n: (layer, 0, 0)),
        pl.BlockSpec((None, 1, D_MODEL), lambda i, n: (layer, 0, 0)),
    ]
    operands = [x, w_in, bias, sink, cw, pw, ps, w_o, ln_g, ln_b]
    out_specs = [pl.BlockSpec((None, rows, D_MODEL), lambda i, n: (i, n, 0)),
                 per_seq(CONV_K - 1, CONV_W), per_seq(WINDOW, 2 * KV_W), per_seq(HALO, POOL_W)]
    out_shape = [jax.ShapeDtypeStruct((b, s, D_MODEL), F32),
                 jax.ShapeDtypeStruct((b, CONV_K - 1, CONV_W), F32),
                 jax.ShapeDtypeStruct((b, WINDOW, 2 * KV_W), F32),
                 jax.ShapeDtypeStruct((b, HALO, POOL_W), F32)]
    cast_tf = None
    if cast_next is not None:
        w_up, w_down, cast_layers, cast_tf = cast_next
        for cast_layer, w in [(cl, w) for cl in cast_layers for w in (w_up, w_down)]:
            _, k, width = w.shape
            wrows = next(r for r in range(2 * SUBLANES, k + 1, 2 * SUBLANES) if k % r == 0 and k // r <= b * steps)
            nblocks = k // wrows
            index = lambda i, n, nblocks=nblocks: (jnp.minimum(i * steps + n, nblocks - 1), 0)
            in_specs.append(pl.BlockSpec((None, wrows, width),
                                         lambda i, n, index=index, cl=cast_layer: (cl,) + index(i, n)))
            operands.append(w)
            out_specs.append(pl.BlockSpec((None, wrows, width), lambda i, n, index=index: (0,) + index(i, n)))
            out_shape.append(jax.ShapeDtypeStruct((1, k, width), BF16))
    return pl.pallas_call(
        functools.partial(_mixers_prompt_body, ntile, tq, alpha, cast_tf),
        grid=(b, steps),
        in_specs=in_specs,
        out_specs=out_specs,
        out_shape=out_shape,
        scratch_shapes=[pltpu.VMEM((WINDOW + rows, IN_W), F32),
                        pltpu.VMEM((ntile, HALO + tq, CONV_W), F32), pltpu.VMEM((ntile, HALO + tq, POOL_W), F32)],
        compiler_params=_params("arbitrary", "arbitrary"),
        name="mixers_prompt",
    )(*operands)


def _mixers_decode(zs, ck, cv, conv_state, pool_state, tables, cw, pw, ps):
    b = zs.shape[0]
    tq = DECODE_ROWS
    nseq = DECODE_SEQS_PER_STEP
    bias, sink = tables
    new = lambda width, blk: pl.BlockSpec((nseq, 1, width), lambda i, n: (i, 0, blk))
    whole = lambda rows, width: pl.BlockSpec((nseq, rows, width), lambda i, n: (i, 0, 0))
    const3 = lambda i, n: (0, 0, 0)
    const2 = lambda i, n: (0, 0)
    in_specs = [
        new(ATTN_W, Q_BLK), new(KV_W, K_BLK), new(KV_W, V_BLK),
        whole(WINDOW, KV_W), whole(WINDOW, KV_W),
        new(CONV_W, GB_BLK), new(CONV_W, GC_BLK), new(CONV_W, H_BLK), new(POOL_W, PIN_BLK),
        whole(CONV_K - 1, CONV_W), whole(POOL_PREV, POOL_W),
        pl.BlockSpec(bias.shape, const3),
        pl.BlockSpec(sink.shape, const3),
        pl.BlockSpec(cw.shape, const2),
        pl.BlockSpec(pw.shape, const3),
        pl.BlockSpec(ps.shape, const2),
    ]
    return pl.pallas_call(
        functools.partial(_mixers_decode_body, nseq, tq),
        grid=(b // nseq, 1),
        in_specs=in_specs,
        out_specs=[whole(1, D_MODEL), whole(CONV_K - 1, CONV_W), whole(WINDOW, KV_W), whole(WINDOW, KV_W)],
        out_shape=[jax.ShapeDtypeStruct((b, 1, D_MODEL), BF16),
                   jax.ShapeDtypeStruct((b, CONV_K - 1, CONV_W), F32),
                   jax.ShapeDtypeStruct((b, WINDOW, KV_W), F32),
                   jax.ShapeDtypeStruct((b, WINDOW, KV_W), F32)],
        scratch_shapes=[pltpu.VMEM((nseq, HALO + tq, CONV_W), F32), pltpu.VMEM((nseq, HALO + tq, POOL_W), F32)],
        compiler_params=_params("arbitrary", "arbitrary"),
        name="mixers_decode",
    )(zs, zs, zs, ck, cv, zs, zs, zs, zs, conv_state, pool_state, bias, sink, cw, pw, ps)


def _bias_table(bias_by_dist, tq):
    nk = 2 * WINDOW
    by_offset = jnp.zeros((N_HEADS, nk + 1), F32).at[:, 1:WINDOW + 1].set(bias_by_dist[::-1].T)
    rows = jnp.tile(by_offset, (1, tq))[:, :tq * nk].reshape(N_HEADS, tq, nk)
    return jnp.swapaxes(rows.reshape(KV_HEADS, GROUP * tq, nk), 1, 2)


def _sink_table(sinks_l, tq):
    return jnp.broadcast_to(sinks_l.astype(F32).reshape(KV_HEADS, GROUP, 1, 1),
                            (KV_HEADS, GROUP, 1, tq)).reshape(KV_HEADS, 1, GROUP * tq)


def _t5_bucket(n):
    max_exact = N_BUCKETS // 2
    nf = jnp.maximum(n, 1).astype(F32)
    large = max_exact + (jnp.log(nf / max_exact) / math.log(MAX_DISTANCE / max_exact)
                         * (N_BUCKETS - max_exact)).astype(jnp.int32)
    large = jnp.minimum(large, N_BUCKETS - 1)
    return jnp.where(n < max_exact, n, large)


def _proj_ln_body(nk, alpha, a_ref, w_ref, x_ref, g_ref, b_ref, o_ref, *scratch):
    part = _dot(a_ref[...], w_ref[...])
    if nk == 1:
        o_ref[...] = _layer_norm(alpha * x_ref[...] + part, g_ref[...], b_ref[...])
        return
    acc_ref, = scratch
    k = pl.program_id(1)

    @pl.when(k == 0)
    def _():
        acc_ref[...] = jnp.zeros_like(acc_ref)

    acc_ref[...] += part

    @pl.when(k == nk - 1)
    def _():
        o_ref[...] = _layer_norm(alpha * x_ref[...] + acc_ref[...], g_ref[...], b_ref[...])


def _proj_ln(a, w, w_layer, x, g, b, layer, alpha, tm, tk):
    m, k = a.shape
    n = w.shape[2]
    nk = k // tk
    return pl.pallas_call(
        functools.partial(_proj_ln_body, nk, alpha),
        grid=(m // tm, nk),
        in_specs=[pl.BlockSpec((tm, tk), lambda i, kk: (i, kk)),
                  pl.BlockSpec((None, tk, n), lambda i, kk: (w_layer, kk, 0)),
                  pl.BlockSpec((tm, n), lambda i, kk: (i, 0)),
                  pl.BlockSpec((None, 1, n), lambda i, kk: (layer, 0, 0)),
                  pl.BlockSpec((None, 1, n), lambda i, kk: (layer, 0, 0))],
        out_specs=pl.BlockSpec((tm, n), lambda i, kk: (i, 0)),
        out_shape=jax.ShapeDtypeStruct((m, n), F32),
        scratch_shapes=[] if nk == 1 else [pltpu.VMEM((tm, n), F32)],
        compiler_params=_params("arbitrary", "arbitrary"),
        name="proj_ln",
    )(a, w, x, g, b)


def _ffn_body(tm, tf, ns, tiles_per_seq, nc, alpha, x_ref, wu_ref, wd_ref, cwg_ref, cwv_ref, g_ref, b_ref,
              o_ref, tail_g_ref, tail_v_ref, xb_ref, ext, carry):
    i = pl.program_id(0)
    c = pl.program_id(1)
    ts = tm // ns

    @pl.when(c == 0)
    def _():
        xb_ref[...] = x_ref[...].astype(BF16)
        o_ref[...] = jnp.zeros_like(o_ref)

    seq_start = (i % tiles_per_seq) == 0

    def up_proj(h, prev_tail):
        up = _dot(xb_ref[pl.ds(h * ts, ts)], wu_ref[...])
        if h == 0:
            @pl.when(seq_start)
            def _():
                ext[0, 0:SUBLANES] = jnp.zeros((SUBLANES, 2 * tf), F32)

            @pl.when(jnp.logical_not(seq_start))
            def _():
                ext[0, 0:SUBLANES] = carry[c]
        else:
            ext[h, 0:SUBLANES] = prev_tail
        ext[h, SUBLANES:SUBLANES + ts] = up
        tail = up[ts - SUBLANES:ts]
        if h == ns - 1:
            carry[c] = tail
            tail_g_ref[...] = tail[:, :tf]
            tail_v_ref[...] = tail[:, tf:]
        return tail

    def gate(h):
        cw = jnp.concatenate([cwg_ref[...], cwv_ref[...]], axis=1)
        hc = ext[h, SUBLANES - 2:SUBLANES - 2 + ts] * cw[0:1]
        hc = hc + ext[h, SUBLANES - 1:SUBLANES - 1 + ts] * cw[1:2]
        hc = hc + ext[h, SUBLANES:SUBLANES + ts] * cw[2:3]
        return (jax.nn.silu(hc[:, :tf]) * hc[:, tf:]).astype(BF16)

    def down_proj(h, act):
        o_ref[pl.ds(h * ts, ts)] += _dot(act, wd_ref[...])

    tail = up_proj(0, None)
    for h in range(ns):
        act = gate(h)
        if h + 1 < ns:
            tail = up_proj(h + 1, tail)
        down_proj(h, act)

    @pl.when(c == nc - 1)
    def _():
        o_ref[...] = _layer_norm(alpha * x_ref[...] + o_ref[...], g_ref[...], b_ref[...])


def _ffn_prompt(x, w_up_il, w_down, w_layer, cw, g, b, layer, alpha, seq, tm, tf, ns):
    m = x.shape[0]
    nc = D_FF // tf
    tiles_per_seq = seq // tm
    tail_spec = pl.BlockSpec((None, SUBLANES, tf), lambda i, c: (i, 0, c))
    tail_shape = jax.ShapeDtypeStruct((m // tm, SUBLANES, D_FF), F32)
    out, tail_g, tail_v = pl.pallas_call(
        functools.partial(_ffn_body, tm, tf, ns, tiles_per_seq, nc, alpha),
        grid=(m // tm, nc),
        in_specs=[pl.BlockSpec((tm, D_MODEL), lambda i, c: (i, 0), pipeline_mode=pl.Buffered(1)),
                  pl.BlockSpec((None, D_MODEL, 2 * tf), lambda i, c: (w_layer, 0, c)),
                  pl.BlockSpec((None, tf, D_MODEL), lambda i, c: (w_layer, c, 0)),
                  pl.BlockSpec((None, CONV_K, tf), lambda i, c: (layer, 0, c)),
                  pl.BlockSpec((None, CONV_K, tf), lambda i, c: (layer, 0, nc + c)),
                  pl.BlockSpec((None, 1, D_MODEL), lambda i, c: (layer, 0, 0)),
                  pl.BlockSpec((None, 1, D_MODEL), lambda i, c: (layer, 0, 0))],
        out_specs=[pl.BlockSpec((tm, D_MODEL), lambda i, c: (i, 0)), tail_spec, tail_spec],
        out_shape=[jax.ShapeDtypeStruct((m, D_MODEL), F32), tail_shape, tail_shape],
        scratch_shapes=[pltpu.VMEM((tm, D_MODEL), BF16),
                        pltpu.VMEM((ns, SUBLANES + tm // ns, 2 * tf), F32),
                        pltpu.VMEM((nc, SUBLANES, 2 * tf), F32)],
        compiler_params=_params("arbitrary", "arbitrary"),
        name="ffn_prompt",
    )(x, w_up_il, w_down, cw, cw, g, b)
    last = slice(tiles_per_seq - 1, None, tiles_per_seq)
    return out, jnp.concatenate([tail_g[last], tail_v[last]], -1)


def _gate_decode_body(ug_ref, uv_ref, sg_ref, sv_ref, cwg_ref, cwv_ref, o_ref):
    def conv(u_ref, s_ref, cw_ref):
        cw = cw_ref[...]
        y = s_ref[:, 0, :] * cw[0:1]
        y = y + s_ref[:, 1, :] * cw[1:2]
        return y + u_ref[...] * cw[2:3]

    hg = conv(ug_ref, sg_ref, cwg_ref)
    hv = conv(uv_ref, sv_ref, cwv_ref)
    o_ref[...] = (jax.nn.silu(hg) * hv).astype(BF16)


def _gate_decode(up, state, cw, layer, tf):
    b = up.shape[0]
    nc = D_FF // tf
    return pl.pallas_call(
        _gate_decode_body,
        grid=(nc,),
        in_specs=[pl.BlockSpec((b, tf), lambda c: (0, c)),
                  pl.BlockSpec((b, tf), lambda c: (0, nc + c)),
                  pl.BlockSpec((b, CONV_K - 1, tf), lambda c: (0, 0, c)),
                  pl.BlockSpec((b, CONV_K - 1, tf), lambda c: (0, 0, nc + c)),
                  pl.BlockSpec((None, CONV_K, tf), lambda c: (layer, 0, c)),
                  pl.BlockSpec((None, CONV_K, tf), lambda c: (layer, 0, nc + c))],
        out_specs=pl.BlockSpec((b, tf), lambda c: (0, c)),
        out_shape=jax.ShapeDtypeStruct((b, D_FF), BF16),
        compiler_params=_params("arbitrary"),
        name="gate_decode",
    )(up, up, state, state, cw, cw)


def kernel(x_prompt, x_sample, cache_k, cache_v, state_conv, state_pool, state_ffn, rel_table, w_in, conv_w,
           pool_w, pool_scale, sinks, w_o, ln1_g, ln1_b, w_up, ffn_conv_w, w_down, ln2_g, ln2_b):
    depth = w_in.shape[0]
    bp_, seq, _ = x_prompt.shape
    bs = x_sample.shape[0]
    assert x_sample.shape[1] == 1 and cache_k.shape[2] == WINDOW
    alpha = (2 * depth) ** 0.25
    nc = D_FF // FFN_TF

    w_in_b = _cast_w_in(w_in, 256)
    w_o_b = _cast_w_o(w_o)
    w_up_il, w_down_b = [], []
    natural_col = lambda j: jnp.where(j % 2 == 0, j // 2, nc + j // 2)
    pool_w_b = pool_w.astype(BF16)

    bias_by_dist = rel_table.astype(F32)[_t5_bucket(jnp.arange(WINDOW))]
    bias_p = _bias_table(bias_by_dist, WINDOW)
    bias_s = _bias_table(bias_by_dist, DECODE_ROWS)

    ln1_g, ln1_b, ln2_g, ln2_b = (t[:, None, :] for t in (ln1_g, ln1_b, ln2_g, ln2_b))

    xp = x_prompt.reshape(bp_ * seq, D_MODEL)
    xs = x_sample.reshape(bs, D_MODEL)

    outs_p, outs_s = [], []
    for l in range(depth):
        tab_p = (bias_p, _sink_table(sinks[l], WINDOW))
        tab_s = (bias_s, _sink_table(sinks[l], DECODE_ROWS))
        ps = pool_scale[l][None, :]

        x1, u_tail, kv_tail, pin_tail, w_up_l, w_down_l = _mixers_prompt(
            xp.reshape(bp_, seq, D_MODEL), w_in_b, tab_p, conv_w[l], pool_w_b[l], ps, w_o_b, l, ln1_g, ln1_b, alpha,
            (w_up, w_down, [l], FFN_TF))
        w_up_il.append(w_up_l)
        w_down_b.append(w_down_l)
        x1 = x1.reshape(bp_ * seq, D_MODEL)
        xp, up_tail = _ffn_prompt(x1, w_up_il[l], w_down_b[l], 0, ffn_conv_w, ln2_g, ln2_b, l, alpha, seq,
                                  FFN_TM, FFN_TF, FFN_NS)
        outs_p.append((
            kv_tail[:, :, :KV_W].reshape(bp_, WINDOW, KV_HEADS, HEAD_DIM),
            kv_tail[:, :, KV_W:].reshape(bp_, WINDOW, KV_HEADS, HEAD_DIM),
            u_tail,
            pin_tail[:, HALO - POOL_PREV:],
            up_tail[:, SUBLANES - (CONV_K - 1):],
        ))

        zs = _matmul(xs, w_in_b, l, bs, 512)
        ck = cache_k[l].reshape(bs, WINDOW, KV_W)
        cv = cache_v[l].reshape(bs, WINDOW, KV_W)
        cat_s, u_new, k_s, v_s = _mixers_decode(zs[:, None, :], ck, cv, state_conv[l], state_pool[l], tab_s,
                                                conv_w[l], pool_w_b[l], ps)
        x1s = _proj_ln(cat_s.reshape(bs, D_MODEL), w_o_b, l, xs, ln1_g, ln1_b, l, alpha, bs, D_MODEL)
        up_s = _matmul(x1s, w_up_il[l], 0, bs, FFN_TF, natural_col)
        act_s = _gate_decode(up_s, state_ffn[l], ffn_conv_w, l, FFN_TF)
        xs = _proj_ln(act_s, w_down_b[l], 0, x1s, ln2_g, ln2_b, l, alpha, bs, 512)
        outs_s.append((
            k_s.reshape(bs, WINDOW, KV_HEADS, HEAD_DIM),
            v_s.reshape(bs, WINDOW, KV_HEADS, HEAD_DIM),
            u_new,
            jnp.concatenate([state_pool[l][:, 1:], zs[:, None, IN_W - POOL_W:]], 1),
            jnp.concatenate([state_ffn[l][:, 1:], up_s[:, None, :]], 1),
        ))

    st = lambda lst, i: jnp.stack([e[i] for e in lst], 0)
    return (xp.reshape(bp_, seq, D_MODEL), xs.reshape(bs, 1, D_MODEL),
            st(outs_p, 0), st(outs_p, 1), st(outs_p, 2), st(outs_p, 3), st(outs_p, 4),
            st(outs_s, 0), st(outs_s, 1), st(outs_s, 2), st(outs_s, 3), st(outs_s, 4))
```

```python
import functools
import math

import jax
import jax.numpy as jnp
from jax import lax
from jax.experimental import pallas as pl
from jax.experimental.pallas import tpu as pltpu

D_MODEL = 2048
N_HEADS = 16
KV_HEADS = 4
HEAD_DIM = 64
GROUP = N_HEADS // KV_HEADS
ATTN_W = N_HEADS * HEAD_DIM
KV_W = KV_HEADS * HEAD_DIM
WINDOW = 128
ATTN_SCALE = HEAD_DIM ** -0.5
NEG_INF = -1e30
N_BUCKETS = 32
MAX_DISTANCE = 128
CONV_W = D_MODEL // 4
CONV_K = 3
POOL_W = D_MODEL // 4
POOL_WINDOWS = (2, 4, 8, 16)
POOL_GROUP = POOL_W // len(POOL_WINDOWS)
POOL_PREV = 15
IN_W = ATTN_W + 2 * KV_W + 3 * CONV_W + POOL_W
D_FF = 5632
PAST_LEN = 16384
LN_EPS = 1e-5

HALO = 16
SUBLANES = 8
DECODE_ROWS = 128 // GROUP
DECODE_SEQS_PER_STEP = 4
PROMPT_TILES_PER_STEP = 2
Q_BLK = 0
K_BLK = ATTN_W // KV_W
V_BLK = K_BLK + 1
GB_BLK = (ATTN_W + 2 * KV_W) // CONV_W
GC_BLK = GB_BLK + 1
H_BLK = GB_BLK + 2
PIN_BLK = GB_BLK + 3

FFN_TM = 1024
FFN_TF = 512
FFN_NS = 4

VMEM_LIMIT = 60 * 1024 * 1024

BF16 = jnp.bfloat16
F32 = jnp.float32


def _params(*sem, flags=None):
    return pltpu.CompilerParams(dimension_semantics=sem, vmem_limit_bytes=VMEM_LIMIT, flags=flags)


def _dot(a, b):
    return jnp.dot(a, b, preferred_element_type=F32)


def _layer_norm(y, g, b):
    mu = jnp.mean(y, -1, keepdims=True)
    yc = y - mu
    var = jnp.mean(yc * yc, -1, keepdims=True)
    return yc * lax.rsqrt(var + LN_EPS) * g + b


def _head_block_perm(blk):
    return (blk % GROUP) * KV_HEADS + blk // GROUP


def _cast_w_in_body(w_ref, o_ref):
    w = w_ref[...]
    order = sorted(range(N_HEADS), key=_head_block_perm)
    q = jnp.concatenate([w[:, h * HEAD_DIM:(h + 1) * HEAD_DIM] for h in order], axis=1)
    o_ref[...] = jnp.concatenate([q, w[:, ATTN_W:]], axis=1).astype(BF16)


def _cast_w_in(w, rows):
    depth, k, n = w.shape
    return pl.pallas_call(
        _cast_w_in_body,
        grid=(depth, k // rows),
        in_specs=[pl.BlockSpec((None, rows, n), lambda l, i: (l, i, 0))],
        out_specs=pl.BlockSpec((None, rows, n), lambda l, i: (l, i, 0)),
        out_shape=jax.ShapeDtypeStruct(w.shape, BF16),
        compiler_params=_params("arbitrary", "arbitrary"),
        name="cast_w_in",
    )(w)


def _cast_w_o_body(w_ref, o_ref):
    @pl.when(pl.program_id(1) == 0)
    def _():
        w = w_ref[...]
        order = sorted(range(N_HEADS), key=_head_block_perm)
        o_ref[...] = jnp.concatenate([w[h * HEAD_DIM:(h + 1) * HEAD_DIM] for h in order], axis=0).astype(BF16)

    @pl.when(pl.program_id(1) > 0)
    def _():
        o_ref[...] = w_ref[...].astype(BF16)


def _cast_w_o(w):
    depth, k, n = w.shape
    return pl.pallas_call(
        _cast_w_o_body,
        grid=(depth, k // ATTN_W),
        in_specs=[pl.BlockSpec((None, ATTN_W, n), lambda l, i: (l, i, 0))],
        out_specs=pl.BlockSpec((None, ATTN_W, n), lambda l, i: (l, i, 0)),
        out_shape=jax.ShapeDtypeStruct(w.shape, BF16),
        compiler_params=_params("arbitrary", "arbitrary"),
        name="cast_w_o",
    )(w)


def _matmul_body(x_ref, w_ref, o_ref, xb_ref):
    @pl.when(pl.program_id(1) == 0)
    def _():
        xb_ref[...] = x_ref[...].astype(BF16)

    o_ref[...] = _dot(xb_ref[...], w_ref[...])


def _matmul(x, w, layer, tm, tn):
    m, k = x.shape
    n = w.shape[2]
    return pl.pallas_call(
        _matmul_body,
        grid=(m // tm, n // tn),
        in_specs=[pl.BlockSpec((tm, k), lambda i, j: (i, 0)),
                  pl.BlockSpec((None, k, tn), lambda i, j: (layer, 0, j))],
        out_specs=pl.BlockSpec((tm, tn), lambda i, j: (i, j)),
        out_shape=jax.ShapeDtypeStruct((m, n), F32),
        scratch_shapes=[pltpu.VMEM((tm, k), BF16)],
        compiler_params=_params("arbitrary", "arbitrary"),
        name="matmul",
    )(x, w)


def _mixers_body(decode, tq, new_rows, pos0, alpha, cast_tf, n, *refs):
    if decode:
        (q_ref, kc_ref, vc_ref, kp_ref, vp_ref, gb_ref, gc_ref, h_ref, pin_ref, us_ref, ps_state_ref,
         bias_ref, sink_ref, cw_ref, pw_ref, ps_ref,
         cat_ref, uo_ref, ko_ref, vo_ref, ext_u, ext_p) = refs
    else:
        (q_ref, kc_ref, vc_ref, kp_ref, vp_ref, gb_ref, gc_ref, h_ref, pin_ref,
         gcp_ref, hp_ref, pinp_ref,
         bias_ref, sink_ref, cw_ref, pw_ref, ps_ref, wo_ref, x_ref, lng_ref, lnb_ref) = refs[:21]
        ext_u, ext_p = refs[-2:]
        cast_src = refs[21:-4 - (len(refs) - 25) // 2]
        x1_ref, uo_ref = refs[21 + len(cast_src):23 + len(cast_src)]
        cast_dst = refs[23 + len(cast_src):-2]
        for k, (src, dst) in enumerate(zip(cast_src, cast_dst)):
            w = src[...]
            if k % 2 == 0:
                w = jnp.concatenate(
                    [w[:, (j % 2) * D_FF + (j // 2) * cast_tf:(j % 2) * D_FF + (j // 2 + 1) * cast_tf]
                     for j in range(2 * (D_FF // cast_tf))], axis=1)
            dst[...] = w.astype(BF16)
    rows = GROUP * tq

    def new_rows_of(ref, nrows=tq):
        x = ref[...]
        if x.shape[0] == nrows:
            return x
        real = lax.broadcasted_iota(jnp.int32, (nrows, x.shape[1]), 0) < x.shape[0]
        return jnp.where(real, jnp.broadcast_to(x, (nrows, x.shape[1])), 0.0)

    nk = 2 * WINDOW
    q = new_rows_of(q_ref)
    qs = jnp.concatenate([q[:, g * KV_W:(g + 1) * KV_W] for g in range(GROUP)], axis=0)
    qs = (qs * ATTN_SCALE).astype(BF16)
    k_all = jnp.concatenate([kp_ref[...], new_rows_of(kc_ref, WINDOW)], axis=0).astype(BF16)
    v_all = jnp.concatenate([vp_ref[...], new_rows_of(vc_ref, WINDOW)], axis=0)
    if decode:
        for cache_ref, new_ref, out_ref in ((kp_ref, kc_ref, ko_ref), (vp_ref, vc_ref, vo_ref)):
            out_ref[0:WINDOW - new_rows] = cache_ref[new_rows:WINDOW]
            out_ref[WINDOW - new_rows:WINDOW] = new_ref[...]
    lane_head = lax.broadcasted_iota(jnp.int32, (nk, KV_W), 1) // HEAD_DIM
    k_cat = jnp.concatenate([jnp.where(lane_head == kvh, k_all, jnp.zeros((), BF16)) for kvh in range(KV_HEADS)],
                            axis=0)
    st_all = lax.dot_general(k_cat, qs, (((1,), (1,)), ((), ())), preferred_element_type=F32)
    kj = lax.broadcasted_iota(jnp.int32, (nk, rows), 0)
    qi = lax.broadcasted_iota(jnp.int32, (nk, rows), 1) & (tq - 1)
    mask = jnp.logical_and(kj > qi, kj <= qi + WINDOW)
    if not decode:
        mask = jnp.logical_and(mask, jnp.logical_or(n > 0, kj >= WINDOW))
    probs = []
    for kvh in range(KV_HEADS):
        s = jnp.where(mask, st_all[kvh * nk:(kvh + 1) * nk] + bias_ref[kvh], NEG_INF)
        sink = sink_ref[kvh]
        m = jnp.maximum(jnp.max(s, 0, keepdims=True), sink)
        p = jnp.exp(s - m)
        den = jnp.sum(p, 0, keepdims=True) + jnp.exp(sink - m)
        probs.append((p * (1.0 / den)).astype(BF16))
    pt_all = jnp.concatenate(probs, axis=0)
    vt = v_all.T
    row_head = lax.broadcasted_iota(jnp.int32, (KV_W, KV_HEADS * nk), 0) // HEAD_DIM
    col_head = lax.broadcasted_iota(jnp.int32, (KV_W, KV_HEADS * nk), 1) // nk
    vt_cat = jnp.where(row_head == col_head, jnp.concatenate([vt] * KV_HEADS, axis=1), 0.0).astype(BF16)
    o = _dot(vt_cat, pt_all).T
    attn = jnp.concatenate([o[g * tq:(g + 1) * tq] for g in range(GROUP)], axis=1)

    u = new_rows_of(gc_ref) * new_rows_of(h_ref)
    if decode:
        ext_u[HALO - (CONV_K - 1):HALO] = us_ref[...]
    else:
        ext_u[0:HALO] = jnp.where(n > 0, gcp_ref[...] * hp_ref[...], 0.0)
    ext_u[HALO:HALO + tq] = u
    cw = cw_ref[...]
    conv = ext_u[HALO - 2:HALO - 2 + tq] * cw[0:1]
    conv = conv + ext_u[HALO - 1:HALO - 1 + tq] * cw[1:2]
    conv = conv + u * cw[2:3]
    c = new_rows_of(gb_ref) * conv
    uo_ref[...] = ext_u[new_rows + HALO - (CONV_K - 1):new_rows + HALO]

    pin = new_rows_of(pin_ref)
    if decode:
        ext_p[0:HALO - POOL_PREV] = jnp.zeros((HALO - POOL_PREV, POOL_W), F32)
        ext_p[HALO - POOL_PREV:HALO] = ps_state_ref[...]
    else:
        ext_p[0:HALO] = jnp.where(n > 0, pinp_ref[...], 0.0)
    ext_p[HALO:HALO + tq] = pin
    pos = pos0 + n * tq + lax.broadcasted_iota(jnp.int32, (tq, 1), 0)
    assert POOL_WINDOWS == tuple(2 << g for g in range(len(POOL_WINDOWS)))
    sums = ext_p[...]
    trailing = []
    for w in POOL_WINDOWS:
        sums = sums + pltpu.roll(sums, w // 2, 0)
        trailing.append(sums[HALO:HALO + tq, 0:POOL_GROUP])
        sums = sums[:, POOL_GROUP:]
    pooled = []
    for g, w in enumerate(POOL_WINDOWS):
        lo = g * POOL_GROUP
        cur = pin[:, lo:lo + POOL_GROUP]
        win = trailing[g]
        cnt = jnp.minimum(pos + 1, w).astype(F32)
        d = (win / cnt - cur).astype(BF16)
        pooled.append(_dot(d, pw_ref[g]))
    pm = jnp.concatenate(pooled, axis=1) * ps_ref[...]

    cat = jnp.concatenate([attn, c, pm], axis=1).astype(BF16)
    if decode:
        cat_ref[...] = cat[0:new_rows]
    else:
        x1_ref[...] = _layer_norm(alpha * x_ref[...] + _dot(cat, wo_ref[...]), lng_ref[...], lnb_ref[...])


def _mixers_decode_body(nseq, tq, *refs):
    n_in, n_shared, n_out = 11, 5, 4
    for s in range(nseq):
        own = lambda group: [r.at[s] for r in group]
        _mixers_body(True, tq, 1, PAST_LEN, None, None, 0, *own(refs[:n_in]), *refs[n_in:n_in + n_shared],
                     *own(refs[n_in + n_shared:n_in + n_shared + n_out]), *own(refs[n_in + n_shared + n_out:]))


def _mixers_prompt_body(ntile, tq, alpha, cast_tf, *refs):
    x_ref, win_ref = refs[:2]
    shared = refs[2:10]
    ncast = (len(refs) - 17) // 2
    cast_src = refs[10:10 + ncast]
    x1, uo, kvo, pino = refs[10 + ncast:14 + ncast]
    cast_dst = refs[14 + ncast:-3]
    z_ref, ext_u, ext_p = refs[-3:]
    n = pl.program_id(1)
    rows = ntile * tq

    @pl.when(n == 0)
    def _():
        z_ref[0:WINDOW] = jnp.zeros((WINDOW, IN_W), F32)

    @pl.when(n > 0)
    def _():
        z_ref[0:WINDOW] = z_ref[rows:rows + WINDOW]

    z_ref[WINDOW:WINDOW + rows] = _dot(x_ref[...].astype(BF16), win_ref[...])
    kvo[...] = z_ref[rows:rows + WINDOW, ATTN_W:ATTN_W + 2 * KV_W]
    pino[...] = z_ref[rows + WINDOW - HALO:rows + WINDOW, IN_W - POOL_W:IN_W]

    def cols(blk, width):
        return pl.ds(blk * width, width)

    for j in range(ntile):
        base = WINDOW + j * tq
        own = lambda blk, width, base=base: z_ref.at[pl.ds(base, tq), cols(blk, width)]
        above = lambda nrows, blk, width, base=base: z_ref.at[pl.ds(base - nrows, nrows), cols(blk, width)]
        tile_shared = shared[:6] + (x_ref.at[pl.ds(j * tq, tq)],) + shared[6:]
        _mixers_body(False, tq, tq, 0, alpha, cast_tf if j == 0 else None, n * ntile + j,
                     own(Q_BLK, ATTN_W), own(K_BLK, KV_W), own(V_BLK, KV_W),
                     above(WINDOW, K_BLK, KV_W), above(WINDOW, V_BLK, KV_W),
                     own(GB_BLK, CONV_W), own(GC_BLK, CONV_W), own(H_BLK, CONV_W), own(PIN_BLK, POOL_W),
                     above(HALO, GC_BLK, CONV_W), above(HALO, H_BLK, CONV_W), above(HALO, PIN_BLK, POOL_W),
                     *tile_shared, *(cast_src if j == 0 else ()), x1.at[pl.ds(j * tq, tq)], uo,
                     *(cast_dst if j == 0 else ()), ext_u.at[j], ext_p.at[j])


def _mixers_prompt(x, w_in, tables, cw, pw, ps, w_o, layer, ln_g, ln_b, alpha, cast_next=None):
    b, s, _ = x.shape
    tq = WINDOW
    ntile = PROMPT_TILES_PER_STEP
    rows = ntile * tq
    steps = s // rows
    bias, sink = tables
    const3 = lambda i, n: (0, 0, 0)
    const2 = lambda i, n: (0, 0)
    per_seq = lambda nrows, width: pl.BlockSpec((None, nrows, width), lambda i, n: (i, 0, 0))
    resident = lambda k, width: pl.BlockSpec((None, k, width), lambda i, n: (layer, 0, 0),
                                             pipeline_mode=pl.Buffered(1))
    in_specs = [
        pl.BlockSpec((None, rows, D_MODEL), lambda i, n: (i, n, 0)),
        resident(D_MODEL, IN_W),
        pl.BlockSpec(bias.shape, const3),
        pl.BlockSpec(sink.shape, const3),
        pl.BlockSpec(cw.shape, const2),
        pl.BlockSpec(pw.shape, const3),
        pl.BlockSpec(ps.shape, const2),
        resident(D_MODEL, D_MODEL),
        pl.BlockSpec((None, 1, D_MODEL), lambda i, n: (layer, 0, 0)),
        pl.BlockSpec((None, 1, D_MODEL), lambda i, n: (layer, 0, 0)),
    ]
    operands = [x, w_in, bias, sink, cw, pw, ps, w_o, ln_g, ln_b]
    out_specs = [pl.BlockSpec((None, rows, D_MODEL), lambda i, n: (i, n, 0)),
                 per_seq(CONV_K - 1, CONV_W), per_seq(WINDOW, 2 * KV_W), per_seq(HALO, POOL_W)]
    out_shape = [jax.ShapeDtypeStruct((b, s, D_MODEL), F32),
                 jax.ShapeDtypeStruct((b, CONV_K - 1, CONV_W), F32),
                 jax.ShapeDtypeStruct((b, WINDOW, 2 * KV_W), F32),
                 jax.ShapeDtypeStruct((b, HALO, POOL_W), F32)]
    cast_tf = None
    if cast_next is not None:
        w_up, w_down, cast_layers, cast_tf = cast_next
        for cast_layer, w in [(cl, w) for cl in cast_layers for w in (w_up, w_down)]:
            _, k, width = w.shape
            wrows = next(r for r in range(2 * SUBLANES, k + 1, 2 * SUBLANES) if k % r == 0 and k // r <= b * steps)
            nblocks = k // wrows
            index = lambda i, n, nblocks=nblocks: (jnp.minimum(i * steps + n, nblocks - 1), 0)
            in_specs.append(pl.BlockSpec((None, wrows, width),
                                         lambda i, n, index=index, cl=cast_layer: (cl,) + index(i, n)))
            operands.append(w)
            out_specs.append(pl.BlockSpec((None, wrows, width), lambda i, n, index=index: (0,) + index(i, n)))
            out_shape.append(jax.ShapeDtypeStruct((1, k, width), BF16))
    return pl.pallas_call(
        functools.partial(_mixers_prompt_body, ntile, tq, alpha, cast_tf),
        grid=(b, steps),
        in_specs=in_specs,
        out_specs=out_specs,
        out_shape=out_shape,
        scratch_shapes=[pltpu.VMEM((WINDOW + rows, IN_W), F32),
                        pltpu.VMEM((ntile, HALO + tq, CONV_W), F32), pltpu.VMEM((ntile, HALO + tq, POOL_W), F32)],
        compiler_params=_params("arbitrary", "arbitrary"),
        name="mixers_prompt",
    )(*operands)


def _mixers_decode(zs, ck, cv, conv_state, pool_state, tables, cw, pw, ps):
    b = zs.shape[0]
    tq = DECODE_ROWS
    nseq = DECODE_SEQS_PER_STEP
    bias, sink = tables
    new = lambda width, blk: pl.BlockSpec((nseq, 1, width), lambda i, n: (i, 0, blk))
    whole = lambda rows, width: pl.BlockSpec((nseq, rows, width), lambda i, n: (i, 0, 0))
    const3 = lambda i, n: (0, 0, 0)
    const2 = lambda i, n: (0, 0)
    in_specs = [
        new(ATTN_W, Q_BLK), new(KV_W, K_BLK), new(KV_W, V_BLK),
        whole(WINDOW, KV_W), whole(WINDOW, KV_W),
        new(CONV_W, GB_BLK), new(CONV_W, GC_BLK), new(CONV_W, H_BLK), new(POOL_W, PIN_BLK),
        whole(CONV_K - 1, CONV_W), whole(POOL_PREV, POOL_W),
        pl.BlockSpec(bias.shape, const3),
        pl.BlockSpec(sink.shape, const3),
        pl.BlockSpec(cw.shape, const2),
        pl.BlockSpec(pw.shape, const3),
        pl.BlockSpec(ps.shape, const2),
    ]
    return pl.pallas_call(
        functools.partial(_mixers_decode_body, nseq, tq),
        grid=(b // nseq, 1),
        in_specs=in_specs,
        out_specs=[whole(1, D_MODEL), whole(CONV_K - 1, CONV_W), whole(WINDOW, KV_W), whole(WINDOW, KV_W)],
        out_shape=[jax.ShapeDtypeStruct((b, 1, D_MODEL), BF16),
                   jax.ShapeDtypeStruct((b, CONV_K - 1, CONV_W), F32),
                   jax.ShapeDtypeStruct((b, WINDOW, KV_W), F32),
                   jax.ShapeDtypeStruct((b, WINDOW, KV_W), F32)],
        scratch_shapes=[pltpu.VMEM((nseq, HALO + tq, CONV_W), F32), pltpu.VMEM((nseq, HALO + tq, POOL_W), F32)],
        compiler_params=_params("arbitrary", "arbitrary"),
        name="mixers_decode",
    )(zs, zs, zs, ck, cv, zs, zs, zs, zs, conv_state, pool_state, bias, sink, cw, pw, ps)


def _bias_table(bias_by_dist, tq):
    nk = 2 * WINDOW
    by_offset = jnp.zeros((N_HEADS, nk + 1), F32).at[:, 1:WINDOW + 1].set(bias_by_dist[::-1].T)
    rows = jnp.tile(by_offset, (1, tq))[:, :tq * nk].reshape(N_HEADS, tq, nk)
    return jnp.swapaxes(rows.reshape(KV_HEADS, GROUP * tq, nk), 1, 2)


def _sink_table(sinks_l, tq):
    return jnp.broadcast_to(sinks_l.astype(F32).reshape(KV_HEADS, GROUP, 1, 1),
                            (KV_HEADS, GROUP, 1, tq)).reshape(KV_HEADS, 1, GROUP * tq)


def _t5_bucket(n):
    max_exact = N_BUCKETS // 2
    nf = jnp.maximum(n, 1).astype(F32)
    large = max_exact + (jnp.log(nf / max_exact) / math.log(MAX_DISTANCE / max_exact)
                         * (N_BUCKETS - max_exact)).astype(jnp.int32)
    large = jnp.minimum(large, N_BUCKETS - 1)
    return jnp.where(n < max_exact, n, large)


def _proj_ln_body(alpha, a_ref, w_ref, x_ref, g_ref, b_ref, o_ref):
    o_ref[...] = _layer_norm(alpha * x_ref[...] + _dot(a_ref[...], w_ref[...]), g_ref[...], b_ref[...])


def _proj_ln(a, w, layer, x, g, b, alpha, tm):
    m, k = a.shape
    n = w.shape[2]
    return pl.pallas_call(
        functools.partial(_proj_ln_body, alpha),
        grid=(m // tm,),
        in_specs=[pl.BlockSpec((tm, k), lambda i: (i, 0)),
                  pl.BlockSpec((None, k, n), lambda i: (layer, 0, 0)),
                  pl.BlockSpec((tm, n), lambda i: (i, 0)),
                  pl.BlockSpec((None, 1, n), lambda i: (layer, 0, 0)),
                  pl.BlockSpec((None, 1, n), lambda i: (layer, 0, 0))],
        out_specs=pl.BlockSpec((tm, n), lambda i: (i, 0)),
        out_shape=jax.ShapeDtypeStruct((m, n), F32),
        compiler_params=_params("arbitrary"),
        name="proj_ln",
    )(a, w, x, g, b)


def _ffn_body(tm, tf, ns, tiles_per_seq, nc, alpha, x_ref, wu_ref, wd_ref, cwg_ref, cwv_ref, g_ref, b_ref,
              xs_ref, sg_ref, sv_ref, o_ref, tail_g_ref, tail_v_ref, xso_ref, ugo_ref, uvo_ref,
              xb_ref, ext, carry, xsb_ref, accs_ref):
    i = pl.program_id(0)
    c = pl.program_id(1)
    ts = tm // ns

    @pl.when(c == 0)
    def _():
        xb_ref[...] = x_ref[...].astype(BF16)
        o_ref[...] = jnp.zeros_like(o_ref)

    seq_start = (i % tiles_per_seq) == 0

    def up_proj(h, prev_tail):
        up = _dot(xb_ref[pl.ds(h * ts, ts)], wu_ref[...])
        if h == 0:
            @pl.when(seq_start)
            def _():
                ext[0, 0:SUBLANES] = jnp.zeros((SUBLANES, 2 * tf), F32)

            @pl.when(jnp.logical_not(seq_start))
            def _():
                ext[0, 0:SUBLANES] = carry[c]
        else:
            ext[h, 0:SUBLANES] = prev_tail
        ext[h, SUBLANES:SUBLANES + ts] = up
        tail = up[ts - SUBLANES:ts]
        if h == ns - 1:
            carry[c] = tail
            tail_g_ref[...] = tail[:, :tf]
            tail_v_ref[...] = tail[:, tf:]
        return tail

    def gate(h):
        cw = jnp.concatenate([cwg_ref[...], cwv_ref[...]], axis=1)
        hc = ext[h, SUBLANES - 2:SUBLANES - 2 + ts] * cw[0:1]
        hc = hc + ext[h, SUBLANES - 1:SUBLANES - 1 + ts] * cw[1:2]
        hc = hc + ext[h, SUBLANES:SUBLANES + ts] * cw[2:3]
        return (jax.nn.silu(hc[:, :tf]) * hc[:, tf:]).astype(BF16)

    def down_proj(h, act):
        o_ref[pl.ds(h * ts, ts)] += _dot(act, wd_ref[...])

    tail = up_proj(0, None)
    for h in range(ns):
        act = gate(h)
        if h + 1 < ns:
            tail = up_proj(h + 1, tail)
        down_proj(h, act)

    @pl.when(c == nc - 1)
    def _():
        o_ref[...] = _layer_norm(alpha * x_ref[...] + o_ref[...], g_ref[...], b_ref[...])

    @pl.when(i == pl.num_programs(0) - 1)
    def _():
        @pl.when(c == 0)
        def _():
            xsb_ref[...] = xs_ref[...].astype(BF16)
            accs_ref[...] = jnp.zeros_like(accs_ref)

        up = _dot(xsb_ref[...], wu_ref[...])
        ugo_ref[...] = up[:, :tf]
        uvo_ref[...] = up[:, tf:]

        def conv(u, s_ref, cw_ref):
            cw = cw_ref[...]
            y = s_ref[:, 0, :] * cw[0:1]
            y = y + s_ref[:, 1, :] * cw[1:2]
            return y + u * cw[2:3]

        act = (jax.nn.silu(conv(up[:, :tf], sg_ref, cwg_ref)) * conv(up[:, tf:], sv_ref, cwv_ref)).astype(BF16)
        accs_ref[...] += _dot(act, wd_ref[...])

        @pl.when(c == nc - 1)
        def _():
            xso_ref[...] = _layer_norm(alpha * xs_ref[...] + accs_ref[...], g_ref[...], b_ref[...])


def _ffn_prompt(x, xs, state, w_up_il, w_down, w_layer, cw, g, b, layer, alpha, seq, tm, tf, ns):
    m = x.shape[0]
    bs = xs.shape[0]
    nc = D_FF // tf
    tiles_per_seq = seq // tm
    last_tile = m // tm - 1
    tail_spec = pl.BlockSpec((None, SUBLANES, tf), lambda i, c: (i, 0, c))
    tail_shape = jax.ShapeDtypeStruct((m // tm, SUBLANES, D_FF), F32)
    dec_chunk = lambda i, c: jnp.where(i == last_tile, c, 0)
    dec_spec = pl.BlockSpec((bs, tf), lambda i, c: (0, dec_chunk(i, c)))
    dec_shape = jax.ShapeDtypeStruct((bs, D_FF), F32)
    out, tail_g, tail_v, xs_new, up_g, up_v = pl.pallas_call(
        functools.partial(_ffn_body, tm, tf, ns, tiles_per_seq, nc, alpha),
        grid=(m // tm, nc),
        in_specs=[pl.BlockSpec((tm, D_MODEL), lambda i, c: (i, 0), pipeline_mode=pl.Buffered(1)),
                  pl.BlockSpec((None, D_MODEL, 2 * tf), lambda i, c: (w_layer, 0, c)),
                  pl.BlockSpec((None, tf, D_MODEL), lambda i, c: (w_layer, c, 0)),
                  pl.BlockSpec((None, CONV_K, tf), lambda i, c: (layer, 0, c)),
                  pl.BlockSpec((None, CONV_K, tf), lambda i, c: (layer, 0, nc + c)),
                  pl.BlockSpec((None, 1, D_MODEL), lambda i, c: (layer, 0, 0)),
                  pl.BlockSpec((None, 1, D_MODEL), lambda i, c: (layer, 0, 0)),
                  pl.BlockSpec((bs, D_MODEL), lambda i, c: (0, 0)),
                  pl.BlockSpec((bs, CONV_K - 1, tf), lambda i, c: (0, 0, dec_chunk(i, c))),
                  pl.BlockSpec((bs, CONV_K - 1, tf), lambda i, c: (0, 0, nc + dec_chunk(i, c)))],
        out_specs=[pl.BlockSpec((tm, D_MODEL), lambda i, c: (i, 0)), tail_spec, tail_spec,
                   pl.BlockSpec((bs, D_MODEL), lambda i, c: (0, 0)), dec_spec, dec_spec],
        out_shape=[jax.ShapeDtypeStruct((m, D_MODEL), F32), tail_shape, tail_shape,
                   jax.ShapeDtypeStruct((bs, D_MODEL), F32), dec_shape, dec_shape],
        scratch_shapes=[pltpu.VMEM((tm, D_MODEL), BF16),
                        pltpu.VMEM((ns, SUBLANES + tm // ns, 2 * tf), F32),
                        pltpu.VMEM((nc, SUBLANES, 2 * tf), F32),
                        pltpu.VMEM((bs, D_MODEL), BF16),
                        pltpu.VMEM((bs, D_MODEL), F32)],
        compiler_params=_params("arbitrary", "arbitrary"),
        name="ffn_prompt",
    )(x, w_up_il, w_down, cw, cw, g, b, xs, state, state)
    last = slice(tiles_per_seq - 1, None, tiles_per_seq)
    return out, jnp.concatenate([tail_g[last], tail_v[last]], -1), xs_new, jnp.concatenate([up_g, up_v], -1)


def kernel(x_prompt, x_sample, cache_k, cache_v, state_conv, state_pool, state_ffn, rel_table, w_in, conv_w,
           pool_w, pool_scale, sinks, w_o, ln1_g, ln1_b, w_up, ffn_conv_w, w_down, ln2_g, ln2_b):
    depth = w_in.shape[0]
    bp_, seq, _ = x_prompt.shape
    bs = x_sample.shape[0]
    assert x_sample.shape[1] == 1 and cache_k.shape[2] == WINDOW
    alpha = (2 * depth) ** 0.25

    w_in_b = _cast_w_in(w_in, 256)
    w_o_b = _cast_w_o(w_o)
    w_up_il, w_down_b = [], []
    pool_w_b = pool_w.astype(BF16)

    bias_by_dist = rel_table.astype(F32)[_t5_bucket(jnp.arange(WINDOW))]
    bias_p = _bias_table(bias_by_dist, WINDOW)
    bias_s = _bias_table(bias_by_dist, DECODE_ROWS)

    ln1_g, ln1_b, ln2_g, ln2_b = (t[:, None, :] for t in (ln1_g, ln1_b, ln2_g, ln2_b))

    xp = x_prompt.reshape(bp_ * seq, D_MODEL)
    xs = x_sample.reshape(bs, D_MODEL)

    outs_p, outs_s = [], []
    for l in range(depth):
        tab_p = (bias_p, _sink_table(sinks[l], WINDOW))
        tab_s = (bias_s, _sink_table(sinks[l], DECODE_ROWS))
        ps = pool_scale[l][None, :]

        x1, u_tail, kv_tail, pin_tail, w_up_l, w_down_l = _mixers_prompt(
            xp.reshape(bp_, seq, D_MODEL), w_in_b, tab_p, conv_w[l], pool_w_b[l], ps, w_o_b, l, ln1_g, ln1_b, alpha,
            (w_up, w_down, [l], FFN_TF))
        w_up_il.append(w_up_l)
        w_down_b.append(w_down_l)
        x1 = x1.reshape(bp_ * seq, D_MODEL)

        zs = _matmul(xs, w_in_b, l, bs, 512)
        ck = cache_k[l].reshape(bs, WINDOW, KV_W)
        cv = cache_v[l].reshape(bs, WINDOW, KV_W)
        cat_s, u_new, k_s, v_s = _mixers_decode(zs[:, None, :], ck, cv, state_conv[l], state_pool[l], tab_s,
                                                conv_w[l], pool_w_b[l], ps)
        x1s = _proj_ln(cat_s.reshape(bs, D_MODEL), w_o_b, l, xs, ln1_g, ln1_b, alpha, bs)

        xp, up_tail, xs, up_s = _ffn_prompt(x1, x1s, state_ffn[l], w_up_il[l], w_down_b[l], 0, ffn_conv_w, ln2_g, ln2_b,
                                            l, alpha, seq, FFN_TM, FFN_TF, FFN_NS)
        outs_p.append((
            kv_tail[:, :, :KV_W].reshape(bp_, WINDOW, KV_HEADS, HEAD_DIM),
            kv_tail[:, :, KV_W:].reshape(bp_, WINDOW, KV_HEADS, HEAD_DIM),
            u_tail,
            pin_tail[:, HALO - POOL_PREV:],
            up_tail[:, SUBLANES - (CONV_K - 1):],
        ))

        outs_s.append((
            k_s.reshape(bs, WINDOW, KV_HEADS, HEAD_DIM),
            v_s.reshape(bs, WINDOW, KV_HEADS, HEAD_DIM),
            u_new,
            jnp.concatenate([state_pool[l][:, 1:], zs[:, None, IN_W - POOL_W:]], 1),
            jnp.concatenate([state_ffn[l][:, 1:], up_s[:, None, :]], 1),
        ))

    st = lambda lst, i: jnp.stack([e[i] for e in lst], 0)
    return (xp.reshape(bp_, seq, D_MODEL), xs.reshape(bs, 1, D_MODEL),
            st(outs_p, 0), st(outs_p, 1), st(outs_p, 2), st(outs_p, 3), st(outs_p, 4),
            st(outs_s, 0), st(outs_s, 1), st(outs_s, 2), st(outs_s, 3), st(outs_s, 4))
```

```python
import functools
import math

import jax
import jax.numpy as jnp
from jax import lax
from jax.experimental import pallas as pl
from jax.experimental.pallas import tpu as pltpu

D_MODEL = 2048
N_HEADS = 16
KV_HEADS = 4
HEAD_DIM = 64
GROUP = N_HEADS // KV_HEADS
ATTN_W = N_HEADS * HEAD_DIM
KV_W = KV_HEADS * HEAD_DIM
WINDOW = 128
ATTN_SCALE = HEAD_DIM ** -0.5
NEG_INF = -1e30
N_BUCKETS = 32
MAX_DISTANCE = 128
CONV_W = D_MODEL // 4
CONV_K = 3
POOL_W = D_MODEL // 4
POOL_WINDOWS = (2, 4, 8, 16)
POOL_GROUP = POOL_W // len(POOL_WINDOWS)
POOL_PREV = 15
IN_W = ATTN_W + 2 * KV_W + 3 * CONV_W + POOL_W
D_FF = 5632
PAST_LEN = 16384
LN_EPS = 1e-5

HALO = 16
SUBLANES = 8
DECODE_ROWS = 128 // GROUP
DECODE_SEQS_PER_STEP = 4
PROMPT_TILES_PER_STEP = 2
Q_BLK = 0
K_BLK = ATTN_W // KV_W
V_BLK = K_BLK + 1
GB_BLK = (ATTN_W + 2 * KV_W) // CONV_W
GC_BLK = GB_BLK + 1
H_BLK = GB_BLK + 2
PIN_BLK = GB_BLK + 3

FFN_TM = 1024
FFN_TF = 512
FFN_NS = 4

VMEM_LIMIT = 62 * 1024 * 1024

BF16 = jnp.bfloat16
F32 = jnp.float32


def _params(*sem, flags=None):
    return pltpu.CompilerParams(dimension_semantics=sem, vmem_limit_bytes=VMEM_LIMIT, flags=flags)


def _dot(a, b):
    return jnp.dot(a, b, preferred_element_type=F32)


def _layer_norm(y, g, b):
    mu = jnp.mean(y, -1, keepdims=True)
    yc = y - mu
    var = jnp.mean(yc * yc, -1, keepdims=True)
    return yc * lax.rsqrt(var + LN_EPS) * g + b


def _head_block_perm(blk):
    return (blk % GROUP) * KV_HEADS + blk // GROUP


def _cast_w_in_body(w_ref, o_ref):
    w = w_ref[...]
    order = sorted(range(N_HEADS), key=_head_block_perm)
    q = jnp.concatenate([w[:, h * HEAD_DIM:(h + 1) * HEAD_DIM] for h in order], axis=1)
    o_ref[...] = jnp.concatenate([q, w[:, ATTN_W:]], axis=1).astype(BF16)


def _cast_w_in(w, rows):
    depth, k, n = w.shape
    return pl.pallas_call(
        _cast_w_in_body,
        grid=(depth, k // rows),
        in_specs=[pl.BlockSpec((None, rows, n), lambda l, i: (l, i, 0))],
        out_specs=pl.BlockSpec((None, rows, n), lambda l, i: (l, i, 0)),
        out_shape=jax.ShapeDtypeStruct(w.shape, BF16),
        compiler_params=_params("arbitrary", "arbitrary"),
        name="cast_w_in",
    )(w)


def _cast_w_o_body(w_ref, o_ref):
    @pl.when(pl.program_id(1) == 0)
    def _():
        w = w_ref[...]
        order = sorted(range(N_HEADS), key=_head_block_perm)
        o_ref[...] = jnp.concatenate([w[h * HEAD_DIM:(h + 1) * HEAD_DIM] for h in order], axis=0).astype(BF16)

    @pl.when(pl.program_id(1) > 0)
    def _():
        o_ref[...] = w_ref[...].astype(BF16)


def _cast_w_o(w):
    depth, k, n = w.shape
    return pl.pallas_call(
        _cast_w_o_body,
        grid=(depth, k // ATTN_W),
        in_specs=[pl.BlockSpec((None, ATTN_W, n), lambda l, i: (l, i, 0))],
        out_specs=pl.BlockSpec((None, ATTN_W, n), lambda l, i: (l, i, 0)),
        out_shape=jax.ShapeDtypeStruct(w.shape, BF16),
        compiler_params=_params("arbitrary", "arbitrary"),
        name="cast_w_o",
    )(w)


def _matmul_body(x_ref, w_ref, o_ref, xb_ref):
    @pl.when(pl.program_id(1) == 0)
    def _():
        xb_ref[...] = x_ref[...].astype(BF16)

    o_ref[...] = _dot(xb_ref[...], w_ref[...])


def _matmul(x, w, layer, tm, tn):
    m, k = x.shape
    n = w.shape[2]
    return pl.pallas_call(
        _matmul_body,
        grid=(m // tm, n // tn),
        in_specs=[pl.BlockSpec((tm, k), lambda i, j: (i, 0)),
                  pl.BlockSpec((None, k, tn), lambda i, j: (layer, 0, j))],
        out_specs=pl.BlockSpec((tm, tn), lambda i, j: (i, j)),
        out_shape=jax.ShapeDtypeStruct((m, n), F32),
        scratch_shapes=[pltpu.VMEM((tm, k), BF16)],
        compiler_params=_params("arbitrary", "arbitrary"),
        name="matmul",
    )(x, w)


def _mixers_body(decode, tq, new_rows, pos0, alpha, n, fillers, *refs):
    if decode:
        (q_ref, kc_ref, vc_ref, kp_ref, vp_ref, gb_ref, gc_ref, h_ref, pin_ref, us_ref, ps_state_ref,
         bias_ref, sink_ref, cw_ref, pw_ref, ps_ref,
         cat_ref, uo_ref, ko_ref, vo_ref, ext_u, ext_p) = refs
    else:
        (q_ref, kc_ref, vc_ref, kp_ref, vp_ref, gb_ref, gc_ref, h_ref, pin_ref,
         gcp_ref, hp_ref, pinp_ref,
         bias_ref, sink_ref, cw_ref, pw_ref, ps_ref, wo_ref, x_ref, lng_ref, lnb_ref,
         x1_ref, uo_ref, ext_u, ext_p) = refs
    rows = GROUP * tq

    def new_rows_of(ref, nrows=tq):
        x = ref[...]
        if x.shape[0] == nrows:
            return x
        real = lax.broadcasted_iota(jnp.int32, (nrows, x.shape[1]), 0) < x.shape[0]
        return jnp.where(real, jnp.broadcast_to(x, (nrows, x.shape[1])), 0.0)

    nk = 2 * WINDOW
    q = new_rows_of(q_ref)
    qs = jnp.concatenate([q[:, g * KV_W:(g + 1) * KV_W] for g in range(GROUP)], axis=0)
    qs = (qs * ATTN_SCALE).astype(BF16)
    k_all = jnp.concatenate([kp_ref[...], new_rows_of(kc_ref, WINDOW)], axis=0).astype(BF16)
    v_all = jnp.concatenate([vp_ref[...], new_rows_of(vc_ref, WINDOW)], axis=0)
    if decode:
        for cache_ref, new_ref, out_ref in ((kp_ref, kc_ref, ko_ref), (vp_ref, vc_ref, vo_ref)):
            out_ref[0:WINDOW - new_rows] = cache_ref[new_rows:WINDOW]
            out_ref[WINDOW - new_rows:WINDOW] = new_ref[...]
    lane_head = lax.broadcasted_iota(jnp.int32, (nk, KV_W), 1) // HEAD_DIM
    k_cat = jnp.concatenate([jnp.where(lane_head == kvh, k_all, jnp.zeros((), BF16)) for kvh in range(KV_HEADS)],
                            axis=0)
    st_all = lax.dot_general(k_cat, qs, (((1,), (1,)), ((), ())), preferred_element_type=F32)
    if fillers:
        fillers[0]()
    kj = lax.broadcasted_iota(jnp.int32, (nk, rows), 0)
    qi = lax.broadcasted_iota(jnp.int32, (nk, rows), 1) & (tq - 1)
    mask = jnp.logical_and(kj > qi, kj <= qi + WINDOW)
    if not decode:
        mask = jnp.logical_and(mask, jnp.logical_or(n > 0, kj >= WINDOW))
    probs = []
    for kvh in range(KV_HEADS):
        s = jnp.where(mask, st_all[kvh * nk:(kvh + 1) * nk] + bias_ref[kvh], NEG_INF)
        sink = sink_ref[kvh]
        m = jnp.maximum(jnp.max(s, 0, keepdims=True), sink)
        p = jnp.exp(s - m)
        den = jnp.sum(p, 0, keepdims=True) + jnp.exp(sink - m)
        probs.append((p * (1.0 / den)).astype(BF16))
    pt_all = jnp.concatenate(probs, axis=0)
    vt = v_all.T
    row_head = lax.broadcasted_iota(jnp.int32, (KV_W, KV_HEADS * nk), 0) // HEAD_DIM
    col_head = lax.broadcasted_iota(jnp.int32, (KV_W, KV_HEADS * nk), 1) // nk
    vt_cat = jnp.where(row_head == col_head, jnp.concatenate([vt] * KV_HEADS, axis=1), 0.0).astype(BF16)
    o = _dot(vt_cat, pt_all).T
    if fillers:
        fillers[1]()
    attn = jnp.concatenate([o[g * tq:(g + 1) * tq] for g in range(GROUP)], axis=1)

    u = new_rows_of(gc_ref) * new_rows_of(h_ref)
    if decode:
        ext_u[HALO - (CONV_K - 1):HALO] = us_ref[...]
    else:
        ext_u[0:HALO] = jnp.where(n > 0, gcp_ref[...] * hp_ref[...], 0.0)
    ext_u[HALO:HALO + tq] = u
    cw = cw_ref[...]
    conv = ext_u[HALO - 2:HALO - 2 + tq] * cw[0:1]
    conv = conv + ext_u[HALO - 1:HALO - 1 + tq] * cw[1:2]
    conv = conv + u * cw[2:3]
    c = new_rows_of(gb_ref) * conv
    uo_ref[...] = ext_u[new_rows + HALO - (CONV_K - 1):new_rows + HALO]

    pin = new_rows_of(pin_ref)
    if decode:
        ext_p[0:HALO - POOL_PREV] = jnp.zeros((HALO - POOL_PREV, POOL_W), F32)
        ext_p[HALO - POOL_PREV:HALO] = ps_state_ref[...]
    else:
        ext_p[0:HALO] = jnp.where(n > 0, pinp_ref[...], 0.0)
    ext_p[HALO:HALO + tq] = pin
    pos = pos0 + n * tq + lax.broadcasted_iota(jnp.int32, (tq, 1), 0)
    assert POOL_WINDOWS == tuple(2 << g for g in range(len(POOL_WINDOWS)))
    sums = ext_p[...]
    trailing = []
    for w in POOL_WINDOWS:
        sums = sums + pltpu.roll(sums, w // 2, 0)
        trailing.append(sums[HALO:HALO + tq, 0:POOL_GROUP])
        sums = sums[:, POOL_GROUP:]
    pooled = []
    for g, w in enumerate(POOL_WINDOWS):
        lo = g * POOL_GROUP
        cur = pin[:, lo:lo + POOL_GROUP]
        win = trailing[g]
        cnt = jnp.minimum(pos + 1, w).astype(F32)
        d = (win / cnt - cur).astype(BF16)
        pooled.append(_dot(d, pw_ref[g]))
    pm = jnp.concatenate(pooled, axis=1) * ps_ref[...]

    cat = jnp.concatenate([attn, c, pm], axis=1).astype(BF16)
    if decode:
        cat_ref[...] = cat[0:new_rows]
    else:
        mix = _dot(cat, wo_ref[...])
        if fillers:
            fillers[2]()
        x1_ref[...] = _layer_norm(alpha * x_ref[...] + mix, lng_ref[...], lnb_ref[...])


def _mixers_decode_body(nseq, tq, *refs):
    n_in, n_shared, n_out = 11, 5, 4
    for s in range(nseq):
        own = lambda group: [r.at[s] for r in group]
        _mixers_body(True, tq, 1, PAST_LEN, None, 0, None, *own(refs[:n_in]), *refs[n_in:n_in + n_shared],
                     *own(refs[n_in + n_shared:n_in + n_shared + n_out]), *own(refs[n_in + n_shared + n_out:]))


def _mixers_prompt_body(ntile, tq, steps, alpha, cast_tf, *refs):
    x_ref, win_ref = refs[:2]
    shared = refs[2:10]
    ncast = (len(refs) - 21) // 2
    cast_src = refs[10:10 + ncast]
    x1, uo, kvo, pino = refs[10 + ncast:14 + ncast]
    cast_dst = refs[14 + ncast:14 + 2 * ncast]
    z_bufs = refs[-7:-5]
    xres, carry_kv, carry_halo, ext_u, ext_p = refs[-5:]
    t = pl.program_id(1)
    rows = ntile * tq
    halo_w = 2 * CONV_W + POOL_W

    for k, (src, dst) in enumerate(zip(cast_src, cast_dst)):
        w = src[...]
        if k % 2 == 0:
            w = jnp.concatenate(
                [w[:, (j % 2) * D_FF + (j // 2) * cast_tf:(j % 2) * D_FF + (j // 2 + 1) * cast_tf]
                 for j in range(2 * (D_FF // cast_tf))], axis=1)
        dst[...] = w.astype(BF16)

    def cols(blk, width):
        return pl.ds(blk * width, width)

    nchunk = IN_W // CONV_W
    per_point = [list(range(k * nchunk // (3 * ntile), (k + 1) * nchunk // (3 * ntile))) for k in range(3 * ntile)]

    def step(p_project, p_mix):
        def project(chunks, xb):
            for k in chunks:
                cols_k = slice(k * CONV_W, (k + 1) * CONV_W)
                z_bufs[p_project][:, cols_k] = _dot(xb, win_ref[:, cols_k])

        if p_project is not None:
            xb = x_ref[...].astype(BF16)
        if p_mix is None:
            project(range(nchunk), xb)
            carry_kv[...] = jnp.zeros_like(carry_kv)
            carry_halo[...] = jnp.zeros_like(carry_halo)
        else:
            zc = z_bufs[p_mix]
            kvo[...] = zc[rows - WINDOW:rows, ATTN_W:ATTN_W + 2 * KV_W]
            pino[...] = zc[rows - HALO:rows, IN_W - POOL_W:IN_W]
            for j in range(ntile):
                own = lambda blk, width, j=j: zc.at[pl.ds(j * tq, tq), cols(blk, width)]
                if j == 0:
                    earlier = [carry_kv.at[:, pl.ds(0, KV_W)], carry_kv.at[:, pl.ds(KV_W, KV_W)],
                               carry_halo.at[:, pl.ds(0, CONV_W)], carry_halo.at[:, pl.ds(CONV_W, CONV_W)],
                               carry_halo.at[:, pl.ds(2 * CONV_W, POOL_W)]]
                else:
                    above = lambda nrows, blk, width, j=j: zc.at[pl.ds(j * tq - nrows, nrows), cols(blk, width)]
                    earlier = [above(WINDOW, K_BLK, KV_W), above(WINDOW, V_BLK, KV_W), above(HALO, GC_BLK, CONV_W),
                               above(HALO, H_BLK, CONV_W), above(HALO, PIN_BLK, POOL_W)]
                fillers = None
                if p_project is not None:
                    fillers = [functools.partial(project, per_point[3 * j + k], xb) for k in range(3)]
                _mixers_body(False, tq, tq, 0, alpha, (t - 1) * ntile + j, fillers,
                             own(Q_BLK, ATTN_W), own(K_BLK, KV_W), own(V_BLK, KV_W), earlier[0], earlier[1],
                             own(GB_BLK, CONV_W), own(GC_BLK, CONV_W), own(H_BLK, CONV_W), own(PIN_BLK, POOL_W),
                             *earlier[2:], *shared[:6], xres.at[pl.ds(j * tq, tq)], *shared[6:],
                             x1.at[pl.ds(j * tq, tq)], uo, ext_u.at[j], ext_p.at[j])
            carry_kv[...] = zc[rows - WINDOW:rows, ATTN_W:ATTN_W + 2 * KV_W]
            carry_halo[...] = zc[rows - HALO:rows, IN_W - halo_w:IN_W]
        if p_project is not None:
            xres[...] = x_ref[...]

    pl.when(t == 0)(functools.partial(step, 0, None))
    for p in range(2):
        pl.when(jnp.logical_and(jnp.logical_and(t > 0, t < steps), t % 2 == p))(functools.partial(step, p, 1 - p))
    pl.when(t == steps)(functools.partial(step, None, (steps - 1) % 2))


def _mixers_prompt(x, w_in, tables, cw, pw, ps, w_o, layer, ln_g, ln_b, alpha, cast_next=None):
    b, s, _ = x.shape
    tq = WINDOW
    ntile = PROMPT_TILES_PER_STEP
    rows = ntile * tq
    steps = s // rows
    bias, sink = tables
    const3 = lambda i, n: (0, 0, 0)
    const2 = lambda i, n: (0, 0)
    per_seq = lambda nrows, width: pl.BlockSpec((None, nrows, width), lambda i, n: (i, 0, 0))
    resident = lambda k, width: pl.BlockSpec((None, k, width), lambda i, n: (layer, 0, 0),
                                             pipeline_mode=pl.Buffered(1))
    in_specs = [
        pl.BlockSpec((None, rows, D_MODEL), lambda i, n: (i, jnp.minimum(n, steps - 1), 0)),
        resident(D_MODEL, IN_W),
        pl.BlockSpec(bias.shape, const3),
        pl.BlockSpec(sink.shape, const3),
        pl.BlockSpec(cw.shape, const2),
        pl.BlockSpec(pw.shape, const3),
        pl.BlockSpec(ps.shape, const2),
        resident(D_MODEL, D_MODEL),
        pl.BlockSpec((None, 1, D_MODEL), lambda i, n: (layer, 0, 0)),
        pl.BlockSpec((None, 1, D_MODEL), lambda i, n: (layer, 0, 0)),
    ]
    operands = [x, w_in, bias, sink, cw, pw, ps, w_o, ln_g, ln_b]
    out_specs = [pl.BlockSpec((None, rows, D_MODEL), lambda i, n: (i, jnp.maximum(n - 1, 0), 0)),
                 per_seq(CONV_K - 1, CONV_W), per_seq(WINDOW, 2 * KV_W), per_seq(HALO, POOL_W)]
    out_shape = [jax.ShapeDtypeStruct((b, s, D_MODEL), F32),
                 jax.ShapeDtypeStruct((b, CONV_K - 1, CONV_W), F32),
                 jax.ShapeDtypeStruct((b, WINDOW, 2 * KV_W), F32),
                 jax.ShapeDtypeStruct((b, HALO, POOL_W), F32)]
    cast_tf = None
    if cast_next is not None:
        w_up, w_down, cast_layers, cast_tf = cast_next
        for cast_layer, w in [(cl, w) for cl in cast_layers for w in (w_up, w_down)]:
            _, k, width = w.shape
            wrows = next(r for r in range(2 * SUBLANES, k + 1, 2 * SUBLANES) if k % r == 0 and k // r <= b * steps)
            nblocks = k // wrows
            index = lambda i, n, nblocks=nblocks: (jnp.minimum(i * (steps + 1) + n, nblocks - 1), 0)
            in_specs.append(pl.BlockSpec((None, wrows, width),
                                         lambda i, n, index=index, cl=cast_layer: (cl,) + index(i, n)))
            operands.append(w)
            out_specs.append(pl.BlockSpec((None, wrows, width), lambda i, n, index=index: (0,) + index(i, n)))
            out_shape.append(jax.ShapeDtypeStruct((1, k, width), BF16))
    return pl.pallas_call(
        functools.partial(_mixers_prompt_body, ntile, tq, steps, alpha, cast_tf),
        grid=(b, steps + 1),
        in_specs=in_specs,
        out_specs=out_specs,
        out_shape=out_shape,
        scratch_shapes=[pltpu.VMEM((rows, IN_W), F32),
                        pltpu.VMEM((rows, IN_W), F32),
                        pltpu.VMEM((rows, D_MODEL), F32),
                        pltpu.VMEM((WINDOW, 2 * KV_W), F32),
                        pltpu.VMEM((HALO, 2 * CONV_W + POOL_W), F32),
                        pltpu.VMEM((ntile, HALO + tq, CONV_W), F32), pltpu.VMEM((ntile, HALO + tq, POOL_W), F32)],
        compiler_params=_params("arbitrary", "arbitrary"),
        name="mixers_prompt",
    )(*operands)


def _mixers_decode(zs, ck, cv, conv_state, pool_state, tables, cw, pw, ps):
    b = zs.shape[0]
    tq = DECODE_ROWS
    nseq = DECODE_SEQS_PER_STEP
    bias, sink = tables
    new = lambda width, blk: pl.BlockSpec((nseq, 1, width), lambda i, n: (i, 0, blk))
    whole = lambda rows, width: pl.BlockSpec((nseq, rows, width), lambda i, n: (i, 0, 0))
    const3 = lambda i, n: (0, 0, 0)
    const2 = lambda i, n: (0, 0)
    in_specs = [
        new(ATTN_W, Q_BLK), new(KV_W, K_BLK), new(KV_W, V_BLK),
        whole(WINDOW, KV_W), whole(WINDOW, KV_W),
        new(CONV_W, GB_BLK), new(CONV_W, GC_BLK), new(CONV_W, H_BLK), new(POOL_W, PIN_BLK),
        whole(CONV_K - 1, CONV_W), whole(POOL_PREV, POOL_W),
        pl.BlockSpec(bias.shape, const3),
        pl.BlockSpec(sink.shape, const3),
        pl.BlockSpec(cw.shape, const2),
        pl.BlockSpec(pw.shape, const3),
        pl.BlockSpec(ps.shape, const2),
    ]
    return pl.pallas_call(
        functools.partial(_mixers_decode_body, nseq, tq),
        grid=(b // nseq, 1),
        in_specs=in_specs,
        out_specs=[whole(1, D_MODEL), whole(CONV_K - 1, CONV_W), whole(WINDOW, KV_W), whole(WINDOW, KV_W)],
        out_shape=[jax.ShapeDtypeStruct((b, 1, D_MODEL), BF16),
                   jax.ShapeDtypeStruct((b, CONV_K - 1, CONV_W), F32),
                   jax.ShapeDtypeStruct((b, WINDOW, KV_W), F32),
                   jax.ShapeDtypeStruct((b, WINDOW, KV_W), F32)],
        scratch_shapes=[pltpu.VMEM((nseq, HALO + tq, CONV_W), F32), pltpu.VMEM((nseq, HALO + tq, POOL_W), F32)],
        compiler_params=_params("arbitrary", "arbitrary"),
        name="mixers_decode",
    )(zs, zs, zs, ck, cv, zs, zs, zs, zs, conv_state, pool_state, bias, sink, cw, pw, ps)


def _bias_table(bias_by_dist, tq):
    nk = 2 * WINDOW
    by_offset = jnp.zeros((N_HEADS, nk + 1), F32).at[:, 1:WINDOW + 1].set(bias_by_dist[::-1].T)
    rows = jnp.tile(by_offset, (1, tq))[:, :tq * nk].reshape(N_HEADS, tq, nk)
    return jnp.swapaxes(rows.reshape(KV_HEADS, GROUP * tq, nk), 1, 2)


def _sink_table(sinks_l, tq):
    return jnp.broadcast_to(sinks_l.astype(F32).reshape(KV_HEADS, GROUP, 1, 1),
                            (KV_HEADS, GROUP, 1, tq)).reshape(KV_HEADS, 1, GROUP * tq)


def _t5_bucket(n):
    max_exact = N_BUCKETS // 2
    nf = jnp.maximum(n, 1).astype(F32)
    large = max_exact + (jnp.log(nf / max_exact) / math.log(MAX_DISTANCE / max_exact)
                         * (N_BUCKETS - max_exact)).astype(jnp.int32)
    large = jnp.minimum(large, N_BUCKETS - 1)
    return jnp.where(n < max_exact, n, large)


def _proj_ln_body(alpha, a_ref, w_ref, x_ref, g_ref, b_ref, o_ref):
    o_ref[...] = _layer_norm(alpha * x_ref[...] + _dot(a_ref[...], w_ref[...]), g_ref[...], b_ref[...])


def _proj_ln(a, w, layer, x, g, b, alpha, tm):
    m, k = a.shape
    n = w.shape[2]
    return pl.pallas_call(
        functools.partial(_proj_ln_body, alpha),
        grid=(m // tm,),
        in_specs=[pl.BlockSpec((tm, k), lambda i: (i, 0)),
                  pl.BlockSpec((None, k, n), lambda i: (layer, 0, 0)),
                  pl.BlockSpec((tm, n), lambda i: (i, 0)),
                  pl.BlockSpec((None, 1, n), lambda i: (layer, 0, 0)),
                  pl.BlockSpec((None, 1, n), lambda i: (layer, 0, 0))],
        out_specs=pl.BlockSpec((tm, n), lambda i: (i, 0)),
        out_shape=jax.ShapeDtypeStruct((m, n), F32),
        compiler_params=_params("arbitrary"),
        name="proj_ln",
    )(a, w, x, g, b)


def _ffn_body(tm, tf, ns, tiles_per_seq, nc, alpha, x_ref, wu_ref, wd_ref, cwg_ref, cwv_ref, g_ref, b_ref,
              xs_ref, sg_ref, sv_ref, o_ref, tail_g_ref, tail_v_ref, xso_ref, ugo_ref, uvo_ref,
              xb_ref, ext, carry, xsb_ref, accs_ref):
    i = pl.program_id(0)
    c = pl.program_id(1)
    ts = tm // ns

    @pl.when(c == 0)
    def _():
        xb_ref[...] = x_ref[...].astype(BF16)
        o_ref[...] = jnp.zeros_like(o_ref)

    seq_start = (i % tiles_per_seq) == 0

    def up_proj(h, prev_tail):
        up = _dot(xb_ref[pl.ds(h * ts, ts)], wu_ref[...])
        if h == 0:
            @pl.when(seq_start)
            def _():
                ext[0, 0:SUBLANES] = jnp.zeros((SUBLANES, 2 * tf), F32)

            @pl.when(jnp.logical_not(seq_start))
            def _():
                ext[0, 0:SUBLANES] = carry[c]
        else:
            ext[h, 0:SUBLANES] = prev_tail
        ext[h, SUBLANES:SUBLANES + ts] = up
        tail = up[ts - SUBLANES:ts]
        if h == ns - 1:
            carry[c] = tail
            tail_g_ref[...] = tail[:, :tf]
            tail_v_ref[...] = tail[:, tf:]
        return tail

    def gate(h):
        cw = jnp.concatenate([cwg_ref[...], cwv_ref[...]], axis=1)
        hc = ext[h, SUBLANES - 2:SUBLANES - 2 + ts] * cw[0:1]
        hc = hc + ext[h, SUBLANES - 1:SUBLANES - 1 + ts] * cw[1:2]
        hc = hc + ext[h, SUBLANES:SUBLANES + ts] * cw[2:3]
        return (jax.nn.silu(hc[:, :tf]) * hc[:, tf:]).astype(BF16)

    def down_proj(h, act):
        o_ref[pl.ds(h * ts, ts)] += _dot(act, wd_ref[...])

    tail = up_proj(0, None)
    for h in range(ns):
        act = gate(h)
        if h + 1 < ns:
            tail = up_proj(h + 1, tail)
        down_proj(h, act)

    @pl.when(c == nc - 1)
    def _():
        o_ref[...] = _layer_norm(alpha * x_ref[...] + o_ref[...], g_ref[...], b_ref[...])

    @pl.when(i == pl.num_programs(0) - 1)
    def _():
        @pl.when(c == 0)
        def _():
            xsb_ref[...] = xs_ref[...].astype(BF16)
            accs_ref[...] = jnp.zeros_like(accs_ref)

        up = _dot(xsb_ref[...], wu_ref[...])
        ugo_ref[...] = up[:, :tf]
        uvo_ref[...] = up[:, tf:]

        def conv(u, s_ref, cw_ref):
            cw = cw_ref[...]
            y = s_ref[:, 0, :] * cw[0:1]
            y = y + s_ref[:, 1, :] * cw[1:2]
            return y + u * cw[2:3]

        act = (jax.nn.silu(conv(up[:, :tf], sg_ref, cwg_ref)) * conv(up[:, tf:], sv_ref, cwv_ref)).astype(BF16)
        accs_ref[...] += _dot(act, wd_ref[...])

        @pl.when(c == nc - 1)
        def _():
            xso_ref[...] = _layer_norm(alpha * xs_ref[...] + accs_ref[...], g_ref[...], b_ref[...])


def _ffn_prompt(x, xs, state, w_up_il, w_down, w_layer, cw, g, b, layer, alpha, seq, tm, tf, ns):
    m = x.shape[0]
    bs = xs.shape[0]
    nc = D_FF // tf
    tiles_per_seq = seq // tm
    last_tile = m // tm - 1
    tail_spec = pl.BlockSpec((None, SUBLANES, tf), lambda i, c: (i, 0, c))
    tail_shape = jax.ShapeDtypeStruct((m // tm, SUBLANES, D_FF), F32)
    dec_chunk = lambda i, c: jnp.where(i == last_tile, c, 0)
    dec_spec = pl.BlockSpec((bs, tf), lambda i, c: (0, dec_chunk(i, c)))
    dec_shape = jax.ShapeDtypeStruct((bs, D_FF), F32)
    out, tail_g, tail_v, xs_new, up_g, up_v = pl.pallas_call(
        functools.partial(_ffn_body, tm, tf, ns, tiles_per_seq, nc, alpha),
        grid=(m // tm, nc),
        in_specs=[pl.BlockSpec((tm, D_MODEL), lambda i, c: (i, 0), pipeline_mode=pl.Buffered(1)),
                  pl.BlockSpec((None, D_MODEL, 2 * tf), lambda i, c: (w_layer, 0, c)),
                  pl.BlockSpec((None, tf, D_MODEL), lambda i, c: (w_layer, c, 0)),
                  pl.BlockSpec((None, CONV_K, tf), lambda i, c: (layer, 0, c)),
                  pl.BlockSpec((None, CONV_K, tf), lambda i, c: (layer, 0, nc + c)),
                  pl.BlockSpec((None, 1, D_MODEL), lambda i, c: (layer, 0, 0)),
                  pl.BlockSpec((None, 1, D_MODEL), lambda i, c: (layer, 0, 0)),
                  pl.BlockSpec((bs, D_MODEL), lambda i, c: (0, 0)),
                  pl.BlockSpec((bs, CONV_K - 1, tf), lambda i, c: (0, 0, dec_chunk(i, c))),
                  pl.BlockSpec((bs, CONV_K - 1, tf), lambda i, c: (0, 0, nc + dec_chunk(i, c)))],
        out_specs=[pl.BlockSpec((tm, D_MODEL), lambda i, c: (i, 0)), tail_spec, tail_spec,
                   pl.BlockSpec((bs, D_MODEL), lambda i, c: (0, 0)), dec_spec, dec_spec],
        out_shape=[jax.ShapeDtypeStruct((m, D_MODEL), F32), tail_shape, tail_shape,
                   jax.ShapeDtypeStruct((bs, D_MODEL), F32), dec_shape, dec_shape],
        scratch_shapes=[pltpu.VMEM((tm, D_MODEL), BF16),
                        pltpu.VMEM((ns, SUBLANES + tm // ns, 2 * tf), F32),
                        pltpu.VMEM((nc, SUBLANES, 2 * tf), F32),
                        pltpu.VMEM((bs, D_MODEL), BF16),
                        pltpu.VMEM((bs, D_MODEL), F32)],
        compiler_params=_params("arbitrary", "arbitrary"),
        name="ffn_prompt",
    )(x, w_up_il, w_down, cw, cw, g, b, xs, state, state)
    last = slice(tiles_per_seq - 1, None, tiles_per_seq)
    return out, jnp.concatenate([tail_g[last], tail_v[last]], -1), xs_new, jnp.concatenate([up_g, up_v], -1)


def kernel(x_prompt, x_sample, cache_k, cache_v, state_conv, state_pool, state_ffn, rel_table, w_in, conv_w,
           pool_w, pool_scale, sinks, w_o, ln1_g, ln1_b, w_up, ffn_conv_w, w_down, ln2_g, ln2_b):
    depth = w_in.shape[0]
    bp_, seq, _ = x_prompt.shape
    bs = x_sample.shape[0]
    assert x_sample.shape[1] == 1 and cache_k.shape[2] == WINDOW
    alpha = (2 * depth) ** 0.25

    w_in_b = _cast_w_in(w_in, 256)
    w_o_b = _cast_w_o(w_o)
    w_up_il, w_down_b = [], []
    pool_w_b = pool_w.astype(BF16)

    bias_by_dist = rel_table.astype(F32)[_t5_bucket(jnp.arange(WINDOW))]
    bias_p = _bias_table(bias_by_dist, WINDOW)
    bias_s = _bias_table(bias_by_dist, DECODE_ROWS)

    ln1_g, ln1_b, ln2_g, ln2_b = (t[:, None, :] for t in (ln1_g, ln1_b, ln2_g, ln2_b))

    xp = x_prompt.reshape(bp_ * seq, D_MODEL)
    xs = x_sample.reshape(bs, D_MODEL)

    outs_p, outs_s = [], []
    for l in range(depth):
        tab_p = (bias_p, _sink_table(sinks[l], WINDOW))
        tab_s = (bias_s, _sink_table(sinks[l], DECODE_ROWS))
        ps = pool_scale[l][None, :]

        x1, u_tail, kv_tail, pin_tail, w_up_l, w_down_l = _mixers_prompt(
            xp.reshape(bp_, seq, D_MODEL), w_in_b, tab_p, conv_w[l], pool_w_b[l], ps, w_o_b, l, ln1_g, ln1_b, alpha,
            (w_up, w_down, [l], FFN_TF))
        w_up_il.append(w_up_l)
        w_down_b.append(w_down_l)
        x1 = x1.reshape(bp_ * seq, D_MODEL)

        zs = _matmul(xs, w_in_b, l, bs, 512)
        ck = cache_k[l].reshape(bs, WINDOW, KV_W)
        cv = cache_v[l].reshape(bs, WINDOW, KV_W)
        cat_s, u_new, k_s, v_s = _mixers_decode(zs[:, None, :], ck, cv, state_conv[l], state_pool[l], tab_s,
                                                conv_w[l], pool_w_b[l], ps)
        x1s = _proj_ln(cat_s.reshape(bs, D_MODEL), w_o_b, l, xs, ln1_g, ln1_b, alpha, bs)

        xp, up_tail, xs, up_s = _ffn_prompt(x1, x1s, state_ffn[l], w_up_il[l], w_down_b[l], 0, ffn_conv_w, ln2_g, ln2_b,
                                            l, alpha, seq, FFN_TM, FFN_TF, FFN_NS)
        outs_p.append((
            kv_tail[:, :, :KV_W].reshape(bp_, WINDOW, KV_HEADS, HEAD_DIM),
            kv_tail[:, :, KV_W:].reshape(bp_, WINDOW, KV_HEADS, HEAD_DIM),
            u_tail,
            pin_tail[:, HALO - POOL_PREV:],
            up_tail[:, SUBLANES - (CONV_K - 1):],
        ))

        outs_s.append((
            k_s.reshape(bs, WINDOW, KV_HEADS, HEAD_DIM),
            v_s.reshape(bs, WINDOW, KV_HEADS, HEAD_DIM),
            u_new,
            jnp.concatenate([state_pool[l][:, 1:], zs[:, None, IN_W - POOL_W:]], 1),
            jnp.concatenate([state_ffn[l][:, 1:], up_s[:, None, :]], 1),
        ))

    st = lambda lst, i: jnp.stack([e[i] for e in lst], 0)
    return (xp.reshape(bp_, seq, D_MODEL), xs.reshape(bs, 1, D_MODEL),
            st(outs_p, 0), st(outs_p, 1), st(outs_p, 2), st(outs_p, 3), st(outs_p, 4),
            st(outs_s, 0), st(outs_s, 1), st(outs_s, 2), st(outs_s, 3), st(outs_s, 4))
```

```python
import functools
import math

import jax
import jax.numpy as jnp
from jax import lax
from jax.experimental import pallas as pl
from jax.experimental.pallas import tpu as pltpu

D_MODEL = 2048
N_HEADS = 16
KV_HEADS = 4
HEAD_DIM = 64
GROUP = N_HEADS // KV_HEADS
ATTN_W = N_HEADS * HEAD_DIM
KV_W = KV_HEADS * HEAD_DIM
WINDOW = 128
ATTN_SCALE = HEAD_DIM ** -0.5
NEG_INF = -1e30
N_BUCKETS = 32
MAX_DISTANCE = 128
CONV_W = D_MODEL // 4
CONV_K = 3
POOL_W = D_MODEL // 4
POOL_WINDOWS = (2, 4, 8, 16)
POOL_GROUP = POOL_W // len(POOL_WINDOWS)
POOL_PREV = 15
IN_W = ATTN_W + 2 * KV_W + 3 * CONV_W + POOL_W
D_FF = 5632
PAST_LEN = 16384
LN_EPS = 1e-5

HALO = 16
SUBLANES = 8
DECODE_ROWS = 128 // GROUP
DECODE_SEQS_PER_STEP = 8
DECODE_TN = IN_W // 2
CAST_ROWS = 256
PROMPT_TILES_PER_STEP = 2
Q_BLK = 0
K_BLK = ATTN_W // KV_W
V_BLK = K_BLK + 1
GB_BLK = (ATTN_W + 2 * KV_W) // CONV_W
GC_BLK = GB_BLK + 1
H_BLK = GB_BLK + 2
PIN_BLK = GB_BLK + 3

FFN_TM = 1024
FFN_TF = 512
FFN_NS = 4

VMEM_LIMIT = 60 * 1024 * 1024

BF16 = jnp.bfloat16
F32 = jnp.float32


def _params(*sem):
    return pltpu.CompilerParams(dimension_semantics=sem, vmem_limit_bytes=VMEM_LIMIT)


def _dot(a, b):
    return jnp.dot(a, b, preferred_element_type=F32)


def _layer_norm(y, g, b):
    mu = jnp.mean(y, -1, keepdims=True)
    yc = y - mu
    var = jnp.mean(yc * yc, -1, keepdims=True)
    return yc * lax.rsqrt(var + LN_EPS) * g + b


def _head_block_perm(blk):
    return (blk % GROUP) * KV_HEADS + blk // GROUP


def _cast_w_in_body(w_ref, o_ref):
    w = w_ref[...]
    order = sorted(range(N_HEADS), key=_head_block_perm)
    q = jnp.concatenate([w[:, h * HEAD_DIM:(h + 1) * HEAD_DIM] for h in order], axis=1)
    o_ref[...] = jnp.concatenate([q, w[:, ATTN_W:]], axis=1).astype(BF16)


def _cast_w_in(w, layer, rows):
    _, k, n = w.shape
    return pl.pallas_call(
        _cast_w_in_body,
        grid=(1, k // rows),
        in_specs=[pl.BlockSpec((None, rows, n), lambda l, i: (layer, i, 0))],
        out_specs=pl.BlockSpec((None, rows, n), lambda l, i: (0, i, 0)),
        out_shape=jax.ShapeDtypeStruct((1, k, n), BF16),
        compiler_params=_params("arbitrary", "arbitrary"),
        name="cast_w_in",
    )(w)


def _cast_w_o_body(w_ref, o_ref):
    @pl.when(pl.program_id(1) == 0)
    def _():
        w = w_ref[...]
        order = sorted(range(N_HEADS), key=_head_block_perm)
        o_ref[...] = jnp.concatenate([w[h * HEAD_DIM:(h + 1) * HEAD_DIM] for h in order], axis=0).astype(BF16)

    @pl.when(pl.program_id(1) > 0)
    def _():
        o_ref[...] = w_ref[...].astype(BF16)


def _cast_w_o(w, layer):
    _, k, n = w.shape
    return pl.pallas_call(
        _cast_w_o_body,
        grid=(1, k // ATTN_W),
        in_specs=[pl.BlockSpec((None, ATTN_W, n), lambda l, i: (layer, i, 0))],
        out_specs=pl.BlockSpec((None, ATTN_W, n), lambda l, i: (0, i, 0)),
        out_shape=jax.ShapeDtypeStruct((1, k, n), BF16),
        compiler_params=_params("arbitrary", "arbitrary"),
        name="cast_w_o",
    )(w)


def _matmul_body(x_ref, w_ref, o_ref, xb_ref):
    @pl.when(pl.program_id(1) == 0)
    def _():
        xb_ref[...] = x_ref[...].astype(BF16)

    o_ref[...] = _dot(xb_ref[...], w_ref[...])


def _matmul(x, w, layer, tm, tn):
    m, k = x.shape
    n = w.shape[2]
    return pl.pallas_call(
        _matmul_body,
        grid=(m // tm, n // tn),
        in_specs=[pl.BlockSpec((tm, k), lambda i, j: (i, 0)),
                  pl.BlockSpec((None, k, tn), lambda i, j: (layer, 0, j))],
        out_specs=pl.BlockSpec((tm, tn), lambda i, j: (i, j)),
        out_shape=jax.ShapeDtypeStruct((m, n), F32),
        scratch_shapes=[pltpu.VMEM((tm, k), BF16)],
        compiler_params=_params("arbitrary", "arbitrary"),
        name="matmul",
    )(x, w)


def _mixers_body(decode, tq, new_rows, pos0, alpha, cast_tf, n, *refs):
    if decode:
        (q_ref, kc_ref, vc_ref, kp_ref, vp_ref, gb_ref, gc_ref, h_ref, pin_ref, us_ref, ps_state_ref,
         bias_ref, sink_ref, cw_ref, pw_ref, ps_ref,
         cat_ref, uo_ref, ko_ref, vo_ref, ext_u, ext_p) = refs
    else:
        (q_ref, kc_ref, vc_ref, kp_ref, vp_ref, gb_ref, gc_ref, h_ref, pin_ref,
         gcp_ref, hp_ref, pinp_ref,
         bias_ref, sink_ref, cw_ref, pw_ref, ps_ref, wo_ref, x_ref, lng_ref, lnb_ref) = refs[:21]
        ext_u, ext_p = refs[-2:]
        cast_src = refs[21:-4 - (len(refs) - 25) // 2]
        x1_ref, uo_ref = refs[21 + len(cast_src):23 + len(cast_src)]
        cast_dst = refs[23 + len(cast_src):-2]
        for k, (src, dst) in enumerate(zip(cast_src, cast_dst)):
            w = src[...]
            if k == 0:
                w = jnp.concatenate(
                    [w[:, (j % 2) * D_FF + (j // 2) * cast_tf:(j % 2) * D_FF + (j // 2 + 1) * cast_tf]
                     for j in range(2 * (D_FF // cast_tf))], axis=1)
            elif k == 2:
                order = sorted(range(N_HEADS), key=_head_block_perm)
                w = jnp.concatenate([w[:, h * HEAD_DIM:(h + 1) * HEAD_DIM] for h in order] + [w[:, ATTN_W:]], axis=1)
            dst[...] = w.astype(BF16)
    rows = GROUP * tq

    def new_rows_of(ref, nrows=tq):
        x = ref[...]
        if x.shape[0] == nrows:
            return x
        real = lax.broadcasted_iota(jnp.int32, (nrows, x.shape[1]), 0) < x.shape[0]
        return jnp.where(real, jnp.broadcast_to(x, (nrows, x.shape[1])), 0.0)

    nk = 2 * WINDOW
    q = new_rows_of(q_ref)
    qs = jnp.concatenate([q[:, g * KV_W:(g + 1) * KV_W] for g in range(GROUP)], axis=0)
    qs = (qs * ATTN_SCALE).astype(BF16)
    k_all = jnp.concatenate([kp_ref[...], new_rows_of(kc_ref, WINDOW)], axis=0).astype(BF16)
    v_all = jnp.concatenate([vp_ref[...], new_rows_of(vc_ref, WINDOW)], axis=0)
    if decode:
        for cache_ref, new_ref, out_ref in ((kp_ref, kc_ref, ko_ref), (vp_ref, vc_ref, vo_ref)):
            out_ref[0:WINDOW - new_rows] = cache_ref[new_rows:WINDOW]
            out_ref[WINDOW - new_rows:WINDOW] = new_ref[...]
    lane_head = lax.broadcasted_iota(jnp.int32, (nk, KV_W), 1) // HEAD_DIM
    k_cat = jnp.concatenate([jnp.where(lane_head == kvh, k_all, jnp.zeros((), BF16)) for kvh in range(KV_HEADS)],
                            axis=0)
    st_all = lax.dot_general(k_cat, qs, (((1,), (1,)), ((), ())), preferred_element_type=F32)
    kj = lax.broadcasted_iota(jnp.int32, (nk, rows), 0)
    qi = lax.broadcasted_iota(jnp.int32, (nk, rows), 1) & (tq - 1)
    mask = jnp.logical_and(kj > qi, kj <= qi + WINDOW)
    if not decode:
        mask = jnp.logical_and(mask, jnp.logical_or(n > 0, kj >= WINDOW))
    probs = []
    for kvh in range(KV_HEADS):
        s = jnp.where(mask, st_all[kvh * nk:(kvh + 1) * nk] + bias_ref[kvh], NEG_INF)
        sink = sink_ref[kvh]
        m = jnp.maximum(jnp.max(s, 0, keepdims=True), sink)
        p = jnp.exp(s - m)
        den = jnp.sum(p, 0, keepdims=True) + jnp.exp(sink - m)
        probs.append((p * (1.0 / den)).astype(BF16))
    pt_all = jnp.concatenate(probs, axis=0)
    vt = v_all.T
    row_head = lax.broadcasted_iota(jnp.int32, (KV_W, KV_HEADS * nk), 0) // HEAD_DIM
    col_head = lax.broadcasted_iota(jnp.int32, (KV_W, KV_HEADS * nk), 1) // nk
    vt_cat = jnp.where(row_head == col_head, jnp.concatenate([vt] * KV_HEADS, axis=1), 0.0).astype(BF16)
    o = _dot(vt_cat, pt_all).T
    attn = jnp.concatenate([o[g * tq:(g + 1) * tq] for g in range(GROUP)], axis=1)

    u = new_rows_of(gc_ref) * new_rows_of(h_ref)
    if decode:
        ext_u[HALO - (CONV_K - 1):HALO] = us_ref[...]
    else:
        ext_u[0:HALO] = jnp.where(n > 0, gcp_ref[...] * hp_ref[...], 0.0)
    ext_u[HALO:HALO + tq] = u
    cw = cw_ref[...]
    conv = ext_u[HALO - 2:HALO - 2 + tq] * cw[0:1]
    conv = conv + ext_u[HALO - 1:HALO - 1 + tq] * cw[1:2]
    conv = conv + u * cw[2:3]
    c = new_rows_of(gb_ref) * conv
    uo_ref[...] = ext_u[new_rows + HALO - (CONV_K - 1):new_rows + HALO]

    pin = new_rows_of(pin_ref)
    if decode:
        ext_p[0:HALO - POOL_PREV] = jnp.zeros((HALO - POOL_PREV, POOL_W), F32)
        ext_p[HALO - POOL_PREV:HALO] = ps_state_ref[...]
    else:
        ext_p[0:HALO] = jnp.where(n > 0, pinp_ref[...], 0.0)
    ext_p[HALO:HALO + tq] = pin
    pos = pos0 + n * tq + lax.broadcasted_iota(jnp.int32, (tq, 1), 0)
    assert POOL_WINDOWS == tuple(2 << g for g in range(len(POOL_WINDOWS)))
    sums = ext_p[...]
    trailing = []
    for w in POOL_WINDOWS:
        sums = sums + pltpu.roll(sums, w // 2, 0)
        trailing.append(sums[HALO:HALO + tq, 0:POOL_GROUP])
        sums = sums[:, POOL_GROUP:]
    pooled = []
    for g, w in enumerate(POOL_WINDOWS):
        lo = g * POOL_GROUP
        cur = pin[:, lo:lo + POOL_GROUP]
        win = trailing[g]
        cnt = jnp.minimum(pos + 1, w).astype(F32)
        d = (win / cnt - cur).astype(BF16)
        pooled.append(_dot(d, pw_ref[g]))
    pm = jnp.concatenate(pooled, axis=1) * ps_ref[...]

    cat = jnp.concatenate([attn, c, pm], axis=1).astype(BF16)
    if decode:
        cat_ref[...] = cat[0:new_rows]
    else:
        x1_ref[...] = _layer_norm(alpha * x_ref[...] + _dot(cat, wo_ref[...]), lng_ref[...], lnb_ref[...])


def _mixers_decode_body(nseq, tq, *refs):
    n_in, n_shared, n_out = 11, 5, 4
    for s in range(nseq):
        own = lambda group: [r.at[s] for r in group]
        _mixers_body(True, tq, 1, PAST_LEN, None, None, 0, *own(refs[:n_in]), *refs[n_in:n_in + n_shared],
                     *own(refs[n_in + n_shared:n_in + n_shared + n_out]), *own(refs[n_in + n_shared + n_out:]))


def _mixers_prompt_body(ntile, tq, alpha, cast_tf, *refs):
    x_ref, win_ref = refs[:2]
    shared = refs[2:10]
    ncast = (len(refs) - 17) // 2
    cast_src = refs[10:10 + ncast]
    x1, uo, kvo, pino = refs[10 + ncast:14 + ncast]
    cast_dst = refs[14 + ncast:-3]
    z_ref, ext_u, ext_p = refs[-3:]
    n = pl.program_id(1)
    rows = ntile * tq

    @pl.when(n == 0)
    def _():
        z_ref[0:WINDOW] = jnp.zeros((WINDOW, IN_W), F32)

    @pl.when(n > 0)
    def _():
        z_ref[0:WINDOW] = z_ref[rows:rows + WINDOW]

    z_ref[WINDOW:WINDOW + rows] = _dot(x_ref[...].astype(BF16), win_ref[...])
    kvo[...] = z_ref[rows:rows + WINDOW, ATTN_W:ATTN_W + 2 * KV_W]
    pino[...] = z_ref[rows + WINDOW - HALO:rows + WINDOW, IN_W - POOL_W:IN_W]

    def cols(blk, width):
        return pl.ds(blk * width, width)

    for j in range(ntile):
        base = WINDOW + j * tq
        own = lambda blk, width, base=base: z_ref.at[pl.ds(base, tq), cols(blk, width)]
        above = lambda nrows, blk, width, base=base: z_ref.at[pl.ds(base - nrows, nrows), cols(blk, width)]
        tile_shared = shared[:6] + (x_ref.at[pl.ds(j * tq, tq)],) + shared[6:]
        _mixers_body(False, tq, tq, 0, alpha, cast_tf if j == 0 else None, n * ntile + j,
                     own(Q_BLK, ATTN_W), own(K_BLK, KV_W), own(V_BLK, KV_W),
                     above(WINDOW, K_BLK, KV_W), above(WINDOW, V_BLK, KV_W),
                     own(GB_BLK, CONV_W), own(GC_BLK, CONV_W), own(H_BLK, CONV_W), own(PIN_BLK, POOL_W),
                     above(HALO, GC_BLK, CONV_W), above(HALO, H_BLK, CONV_W), above(HALO, PIN_BLK, POOL_W),
                     *tile_shared, *(cast_src if j == 0 else ()), x1.at[pl.ds(j * tq, tq)], uo,
                     *(cast_dst if j == 0 else ()), ext_u.at[j], ext_p.at[j])


def _mixers_prompt(x, w_in, tables, cw, pw, ps, w_o, layer, ln_g, ln_b, alpha, cast_next=None):
    b, s, _ = x.shape
    tq = WINDOW
    ntile = PROMPT_TILES_PER_STEP
    rows = ntile * tq
    steps = s // rows
    bias, sink = tables
    const3 = lambda i, n: (0, 0, 0)
    const2 = lambda i, n: (0, 0)
    per_seq = lambda nrows, width: pl.BlockSpec((None, nrows, width), lambda i, n: (i, 0, 0))
    resident = lambda k, width: pl.BlockSpec((None, k, width), lambda i, n: (0, 0, 0),
                                             pipeline_mode=pl.Buffered(1))
    in_specs = [
        pl.BlockSpec((None, rows, D_MODEL), lambda i, n: (i, n, 0)),
        resident(D_MODEL, IN_W),
        pl.BlockSpec(bias.shape, const3),
        pl.BlockSpec(sink.shape, const3),
        pl.BlockSpec(cw.shape, const2),
        pl.BlockSpec(pw.shape, const3),
        pl.BlockSpec(ps.shape, const2),
        resident(D_MODEL, D_MODEL),
        pl.BlockSpec((None, 1, D_MODEL), lambda i, n: (layer, 0, 0)),
        pl.BlockSpec((None, 1, D_MODEL), lambda i, n: (layer, 0, 0)),
    ]
    operands = [x, w_in, bias, sink, cw, pw, ps, w_o, ln_g, ln_b]
    out_specs = [pl.BlockSpec((None, rows, D_MODEL), lambda i, n: (i, n, 0)),
                 per_seq(CONV_K - 1, CONV_W), per_seq(WINDOW, 2 * KV_W), per_seq(HALO, POOL_W)]
    out_shape = [jax.ShapeDtypeStruct((b, s, D_MODEL), F32),
                 jax.ShapeDtypeStruct((b, CONV_K - 1, CONV_W), F32),
                 jax.ShapeDtypeStruct((b, WINDOW, 2 * KV_W), F32),
                 jax.ShapeDtypeStruct((b, HALO, POOL_W), F32)]
    cast_tf = None
    if cast_next is not None:
        weights, cast_tf = cast_next
        for pos, (w, cast_layer) in enumerate(weights):
            _, k, width = w.shape
            wrows = next(r for r in range(2 * SUBLANES, k + 1, 2 * SUBLANES) if k % r == 0 and k // r <= b * steps)
            if pos == 3:
                assert wrows <= HEAD_DIM
                wrows = HEAD_DIM
            nblocks = k // wrows
            block = lambda i, n, nblocks=nblocks: jnp.minimum(i * steps + n, nblocks - 1)
            out_block = block
            if pos == 3:
                out_block = lambda i, n, block=block: jnp.where(block(i, n) < N_HEADS, _head_block_perm(block(i, n)),
                                                                block(i, n))
            in_specs.append(pl.BlockSpec((None, wrows, width),
                                         lambda i, n, block=block, cl=cast_layer: (cl, block(i, n), 0)))
            operands.append(w)
            out_specs.append(pl.BlockSpec((None, wrows, width), lambda i, n, out_block=out_block: (0, out_block(i, n), 0)))
            out_shape.append(jax.ShapeDtypeStruct((1, k, width), BF16))
    return pl.pallas_call(
        functools.partial(_mixers_prompt_body, ntile, tq, alpha, cast_tf),
        grid=(b, steps),
        in_specs=in_specs,
        out_specs=out_specs,
        out_shape=out_shape,
        scratch_shapes=[pltpu.VMEM((WINDOW + rows, IN_W), F32),
                        pltpu.VMEM((ntile, HALO + tq, CONV_W), F32), pltpu.VMEM((ntile, HALO + tq, POOL_W), F32)],
        compiler_params=_params("arbitrary", "arbitrary"),
        name="mixers_prompt",
    )(*operands)


def _mixers_decode(zs, ck, cv, conv_state, pool_state, tables, cw, pw, ps):
    b = zs.shape[0]
    tq = DECODE_ROWS
    nseq = DECODE_SEQS_PER_STEP
    bias, sink = tables
    new = lambda width, blk: pl.BlockSpec((nseq, 1, width), lambda i, n: (i, 0, blk))
    whole = lambda rows, width: pl.BlockSpec((nseq, rows, width), lambda i, n: (i, 0, 0))
    const3 = lambda i, n: (0, 0, 0)
    const2 = lambda i, n: (0, 0)
    in_specs = [
        new(ATTN_W, Q_BLK), new(KV_W, K_BLK), new(KV_W, V_BLK),
        whole(WINDOW, KV_W), whole(WINDOW, KV_W),
        new(CONV_W, GB_BLK), new(CONV_W, GC_BLK), new(CONV_W, H_BLK), new(POOL_W, PIN_BLK),
        whole(CONV_K - 1, CONV_W), whole(POOL_PREV, POOL_W),
        pl.BlockSpec(bias.shape, const3),
        pl.BlockSpec(sink.shape, const3),
        pl.BlockSpec(cw.shape, const2),
        pl.BlockSpec(pw.shape, const3),
        pl.BlockSpec(ps.shape, const2),
    ]
    return pl.pallas_call(
        functools.partial(_mixers_decode_body, nseq, tq),
        grid=(b // nseq, 1),
        in_specs=in_specs,
        out_specs=[whole(1, D_MODEL), whole(CONV_K - 1, CONV_W), whole(WINDOW, KV_W), whole(WINDOW, KV_W)],
        out_shape=[jax.ShapeDtypeStruct((b, 1, D_MODEL), BF16),
                   jax.ShapeDtypeStruct((b, CONV_K - 1, CONV_W), F32),
                   jax.ShapeDtypeStruct((b, WINDOW, KV_W), F32),
                   jax.ShapeDtypeStruct((b, WINDOW, KV_W), F32)],
        scratch_shapes=[pltpu.VMEM((nseq, HALO + tq, CONV_W), F32), pltpu.VMEM((nseq, HALO + tq, POOL_W), F32)],
        compiler_params=_params("arbitrary", "arbitrary"),
        name="mixers_decode",
    )(zs, zs, zs, ck, cv, zs, zs, zs, zs, conv_state, pool_state, bias, sink, cw, pw, ps)


def _bias_table(bias_by_dist, tq):
    nk = 2 * WINDOW
    by_offset = jnp.zeros((N_HEADS, nk + 1), F32).at[:, 1:WINDOW + 1].set(bias_by_dist[::-1].T)
    rows = jnp.tile(by_offset, (1, tq))[:, :tq * nk].reshape(N_HEADS, tq, nk)
    return jnp.swapaxes(rows.reshape(KV_HEADS, GROUP * tq, nk), 1, 2)


def _sink_table(sinks_l, tq):
    return jnp.broadcast_to(sinks_l.astype(F32).reshape(KV_HEADS, GROUP, 1, 1),
                            (KV_HEADS, GROUP, 1, tq)).reshape(KV_HEADS, 1, GROUP * tq)


def _t5_bucket(n):
    max_exact = N_BUCKETS // 2
    nf = jnp.maximum(n, 1).astype(F32)
    large = max_exact + (jnp.log(nf / max_exact) / math.log(MAX_DISTANCE / max_exact)
                         * (N_BUCKETS - max_exact)).astype(jnp.int32)
    large = jnp.minimum(large, N_BUCKETS - 1)
    return jnp.where(n < max_exact, n, large)


def _proj_ln_body(alpha, a_ref, w_ref, x_ref, g_ref, b_ref, o_ref):
    o_ref[...] = _layer_norm(alpha * x_ref[...] + _dot(a_ref[...], w_ref[...]), g_ref[...], b_ref[...])


def _proj_ln(a, w, w_layer, x, g, b, layer, alpha, tm):
    m, k = a.shape
    n = w.shape[2]
    return pl.pallas_call(
        functools.partial(_proj_ln_body, alpha),
        grid=(m // tm,),
        in_specs=[pl.BlockSpec((tm, k), lambda i: (i, 0)),
                  pl.BlockSpec((None, k, n), lambda i: (w_layer, 0, 0)),
                  pl.BlockSpec((tm, n), lambda i: (i, 0)),
                  pl.BlockSpec((None, 1, n), lambda i: (layer, 0, 0)),
                  pl.BlockSpec((None, 1, n), lambda i: (layer, 0, 0))],
        out_specs=pl.BlockSpec((tm, n), lambda i: (i, 0)),
        out_shape=jax.ShapeDtypeStruct((m, n), F32),
        compiler_params=_params("arbitrary"),
        name="proj_ln",
    )(a, w, x, g, b)


def _ffn_body(tm, tf, ns, tiles_per_seq, nc, alpha, x_ref, wu_ref, wd_ref, cwg_ref, cwv_ref, g_ref, b_ref,
              xs_ref, sg_ref, sv_ref, o_ref, tail_g_ref, tail_v_ref, xso_ref, ugo_ref, uvo_ref,
              xb_ref, ext, carry, xsb_ref, accs_ref):
    i = pl.program_id(0)
    c = pl.program_id(1)
    ts = tm // ns

    @pl.when(c == 0)
    def _():
        xb_ref[...] = x_ref[...].astype(BF16)
        o_ref[...] = jnp.zeros_like(o_ref)

    seq_start = (i % tiles_per_seq) == 0

    def up_proj(h, prev_tail):
        up = _dot(xb_ref[pl.ds(h * ts, ts)], wu_ref[...])
        if h == 0:
            @pl.when(seq_start)
            def _():
                ext[0, 0:SUBLANES] = jnp.zeros((SUBLANES, 2 * tf), F32)

            @pl.when(jnp.logical_not(seq_start))
            def _():
                ext[0, 0:SUBLANES] = carry[c]
        else:
            ext[h, 0:SUBLANES] = prev_tail
        ext[h, SUBLANES:SUBLANES + ts] = up
        tail = up[ts - SUBLANES:ts]
        if h == ns - 1:
            carry[c] = tail
            tail_g_ref[...] = tail[:, :tf]
            tail_v_ref[...] = tail[:, tf:]
        return tail

    def gate(h):
        cw = jnp.concatenate([cwg_ref[...], cwv_ref[...]], axis=1)
        hc = ext[h, SUBLANES - 2:SUBLANES - 2 + ts] * cw[0:1]
        hc = hc + ext[h, SUBLANES - 1:SUBLANES - 1 + ts] * cw[1:2]
        hc = hc + ext[h, SUBLANES:SUBLANES + ts] * cw[2:3]
        return (jax.nn.silu(hc[:, :tf]) * hc[:, tf:]).astype(BF16)

    def down_proj(h, act):
        o_ref[pl.ds(h * ts, ts)] += _dot(act, wd_ref[...])

    tail = up_proj(0, None)
    for h in range(ns):
        act = gate(h)
        if h + 1 < ns:
            tail = up_proj(h + 1, tail)
        down_proj(h, act)

    @pl.when(c == nc - 1)
    def _():
        o_ref[...] = _layer_norm(alpha * x_ref[...] + o_ref[...], g_ref[...], b_ref[...])

    @pl.when(i == pl.num_programs(0) - 1)
    def _():
        @pl.when(c == 0)
        def _():
            xsb_ref[...] = xs_ref[...].astype(BF16)
            accs_ref[...] = jnp.zeros_like(accs_ref)

        up = _dot(xsb_ref[...], wu_ref[...])
        ugo_ref[...] = up[:, :tf]
        uvo_ref[...] = up[:, tf:]

        def conv(u, s_ref, cw_ref):
            cw = cw_ref[...]
            y = s_ref[:, 0, :] * cw[0:1]
            y = y + s_ref[:, 1, :] * cw[1:2]
            return y + u * cw[2:3]

        act = (jax.nn.silu(conv(up[:, :tf], sg_ref, cwg_ref)) * conv(up[:, tf:], sv_ref, cwv_ref)).astype(BF16)
        accs_ref[...] += _dot(act, wd_ref[...])

        @pl.when(c == nc - 1)
        def _():
            xso_ref[...] = _layer_norm(alpha * xs_ref[...] + accs_ref[...], g_ref[...], b_ref[...])


def _ffn_prompt(x, xs, state, w_up_il, w_down, w_layer, cw, g, b, layer, alpha, seq, tm, tf, ns):
    m = x.shape[0]
    bs = xs.shape[0]
    nc = D_FF // tf
    tiles_per_seq = seq // tm
    last_tile = m // tm - 1
    tail_spec = pl.BlockSpec((None, SUBLANES, tf), lambda i, c: (i, 0, c))
    tail_shape = jax.ShapeDtypeStruct((m // tm, SUBLANES, D_FF), F32)
    dec_chunk = lambda i, c: jnp.where(i == last_tile, c, 0)
    dec_spec = pl.BlockSpec((bs, tf), lambda i, c: (0, dec_chunk(i, c)))
    dec_shape = jax.ShapeDtypeStruct((bs, D_FF), F32)
    out, tail_g, tail_v, xs_new, up_g, up_v = pl.pallas_call(
        functools.partial(_ffn_body, tm, tf, ns, tiles_per_seq, nc, alpha),
        grid=(m // tm, nc),
        in_specs=[pl.BlockSpec((tm, D_MODEL), lambda i, c: (i, 0), pipeline_mode=pl.Buffered(1)),
                  pl.BlockSpec((None, D_MODEL, 2 * tf), lambda i, c: (w_layer, 0, c)),
                  pl.BlockSpec((None, tf, D_MODEL), lambda i, c: (w_layer, c, 0)),
                  pl.BlockSpec((None, CONV_K, tf), lambda i, c: (layer, 0, c)),
                  pl.BlockSpec((None, CONV_K, tf), lambda i, c: (layer, 0, nc + c)),
                  pl.BlockSpec((None, 1, D_MODEL), lambda i, c: (layer, 0, 0)),
                  pl.BlockSpec((None, 1, D_MODEL), lambda i, c: (layer, 0, 0)),
                  pl.BlockSpec((bs, D_MODEL), lambda i, c: (0, 0)),
                  pl.BlockSpec((bs, CONV_K - 1, tf), lambda i, c: (0, 0, dec_chunk(i, c))),
                  pl.BlockSpec((bs, CONV_K - 1, tf), lambda i, c: (0, 0, nc + dec_chunk(i, c)))],
        out_specs=[pl.BlockSpec((tm, D_MODEL), lambda i, c: (i, 0)), tail_spec, tail_spec,
                   pl.BlockSpec((bs, D_MODEL), lambda i, c: (0, 0)), dec_spec, dec_spec],
        out_shape=[jax.ShapeDtypeStruct((m, D_MODEL), F32), tail_shape, tail_shape,
                   jax.ShapeDtypeStruct((bs, D_MODEL), F32), dec_shape, dec_shape],
        scratch_shapes=[pltpu.VMEM((tm, D_MODEL), BF16),
                        pltpu.VMEM((ns, SUBLANES + tm // ns, 2 * tf), F32),
                        pltpu.VMEM((nc, SUBLANES, 2 * tf), F32),
                        pltpu.VMEM((bs, D_MODEL), BF16),
                        pltpu.VMEM((bs, D_MODEL), F32)],
        compiler_params=_params("arbitrary", "arbitrary"),
        name="ffn_prompt",
    )(x, w_up_il, w_down, cw, cw, g, b, xs, state, state)
    last = slice(tiles_per_seq - 1, None, tiles_per_seq)
    return out, jnp.concatenate([tail_g[last], tail_v[last]], -1), xs_new, jnp.concatenate([up_g, up_v], -1)


def kernel(x_prompt, x_sample, cache_k, cache_v, state_conv, state_pool, state_ffn, rel_table, w_in, conv_w,
           pool_w, pool_scale, sinks, w_o, ln1_g, ln1_b, w_up, ffn_conv_w, w_down, ln2_g, ln2_b):
    depth = w_in.shape[0]
    bp_, seq, _ = x_prompt.shape
    bs = x_sample.shape[0]
    assert x_sample.shape[1] == 1 and cache_k.shape[2] == WINDOW
    alpha = (2 * depth) ** 0.25

    w_in_b, w_o_b = [_cast_w_in(w_in, 0, CAST_ROWS)], [_cast_w_o(w_o, 0)]
    w_up_il, w_down_b = [], []
    pool_w_b = pool_w.astype(BF16)

    bias_by_dist = rel_table.astype(F32)[_t5_bucket(jnp.arange(WINDOW))]
    bias_p = _bias_table(bias_by_dist, WINDOW)
    bias_s = _bias_table(bias_by_dist, DECODE_ROWS)

    ln1_g, ln1_b, ln2_g, ln2_b = (t[:, None, :] for t in (ln1_g, ln1_b, ln2_g, ln2_b))

    xp = x_prompt.reshape(bp_ * seq, D_MODEL)
    xs = x_sample.reshape(bs, D_MODEL)

    outs_p, outs_s = [], []
    for l in range(depth):
        tab_p = (bias_p, _sink_table(sinks[l], WINDOW))
        tab_s = (bias_s, _sink_table(sinks[l], DECODE_ROWS))
        ps = pool_scale[l][None, :]

        side = [(w_up, l), (w_down, l)] + ([(w_in, l + 1), (w_o, l + 1)] if l + 1 < depth else [])
        x1, u_tail, kv_tail, pin_tail, *converted = _mixers_prompt(
            xp.reshape(bp_, seq, D_MODEL), w_in_b[l], tab_p, conv_w[l], pool_w_b[l], ps, w_o_b[l], l, ln1_g, ln1_b,
            alpha, (side, FFN_TF))
        for store, wb in zip((w_up_il, w_down_b, w_in_b, w_o_b), converted):
            store.append(wb)
        x1 = x1.reshape(bp_ * seq, D_MODEL)

        zs = _matmul(xs, w_in_b[l], 0, bs, DECODE_TN)
        ck = cache_k[l].reshape(bs, WINDOW, KV_W)
        cv = cache_v[l].reshape(bs, WINDOW, KV_W)
        cat_s, u_new, k_s, v_s = _mixers_decode(zs[:, None, :], ck, cv, state_conv[l], state_pool[l], tab_s,
                                                conv_w[l], pool_w_b[l], ps)
        x1s = _proj_ln(cat_s.reshape(bs, D_MODEL), w_o_b[l], 0, xs, ln1_g, ln1_b, l, alpha, bs)

        xp, up_tail, xs, up_s = _ffn_prompt(x1, x1s, state_ffn[l], w_up_il[l], w_down_b[l], 0, ffn_conv_w, ln2_g, ln2_b,
                                            l, alpha, seq, FFN_TM, FFN_TF, FFN_NS)
        outs_p.append((
            kv_tail[:, :, :KV_W].reshape(bp_, WINDOW, KV_HEADS, HEAD_DIM),
            kv_tail[:, :, KV_W:].reshape(bp_, WINDOW, KV_HEADS, HEAD_DIM),
            u_tail,
            pin_tail[:, HALO - POOL_PREV:],
            up_tail[:, SUBLANES - (CONV_K - 1):],
        ))

        outs_s.append((
            k_s.reshape(bs, WINDOW, KV_HEADS, HEAD_DIM),
            v_s.reshape(bs, WINDOW, KV_HEADS, HEAD_DIM),
            u_new,
            jnp.concatenate([state_pool[l][:, 1:], zs[:, None, IN_W - POOL_W:]], 1),
            jnp.concatenate([state_ffn[l][:, 1:], up_s[:, None, :]], 1),
        ))

    st = lambda lst, i: jnp.stack([e[i] for e in lst], 0)
    return (xp.reshape(bp_, seq, D_MODEL), xs.reshape(bs, 1, D_MODEL),
            st(outs_p, 0), st(outs_p, 1), st(outs_p, 2), st(outs_p, 3), st(outs_p, 4),
            st(outs_s, 0), st(outs_s, 1), st(outs_s, 2), st(outs_s, 3), st(outs_s, 4))
```

```python
import functools
import math

import jax
import jax.numpy as jnp
from jax import lax
from jax.experimental import pallas as pl
from jax.experimental.pallas import tpu as pltpu

D_MODEL = 2048
N_HEADS = 16
KV_HEADS = 4
HEAD_DIM = 64
GROUP = N_HEADS // KV_HEADS
ATTN_W = N_HEADS * HEAD_DIM
KV_W = KV_HEADS * HEAD_DIM
WINDOW = 128
ATTN_SCALE = HEAD_DIM ** -0.5
NEG_INF = -1e30
N_BUCKETS = 32
MAX_DISTANCE = 128
CONV_W = D_MODEL // 4
CONV_K = 3
POOL_W = D_MODEL // 4
POOL_WINDOWS = (2, 4, 8, 16)
POOL_GROUP = POOL_W // len(POOL_WINDOWS)
POOL_PREV = 15
IN_W = ATTN_W + 2 * KV_W + 3 * CONV_W + POOL_W
D_FF = 5632
PAST_LEN = 16384
LN_EPS = 1e-5

HALO = 16
SUBLANES = 8
DECODE_ROWS = 128 // GROUP
DECODE_SEQS_PER_STEP = 8
DECODE_TN = IN_W // 2
CAST_ROWS = 256
PROMPT_TILES_PER_STEP = 2
Q_BLK = 0
K_BLK = ATTN_W // KV_W
V_BLK = K_BLK + 1
GB_BLK = (ATTN_W + 2 * KV_W) // CONV_W
GC_BLK = GB_BLK + 1
H_BLK = GB_BLK + 2
PIN_BLK = GB_BLK + 3

FFN_TM = 1024
FFN_TF = 512
FFN_NS = 4

VMEM_LIMIT = 60 * 1024 * 1024

BF16 = jnp.bfloat16
F32 = jnp.float32


def _params(*sem):
    return pltpu.CompilerParams(dimension_semantics=sem, vmem_limit_bytes=VMEM_LIMIT)


def _dot(a, b):
    return jnp.dot(a, b, preferred_element_type=F32)


def _layer_norm(y, g, b):
    mu = jnp.mean(y, -1, keepdims=True)
    yc = y - mu
    var = jnp.mean(yc * yc, -1, keepdims=True)
    return yc * lax.rsqrt(var + LN_EPS) * g + b


def _head_block_perm(blk):
    return (blk % GROUP) * KV_HEADS + blk // GROUP


def _cast_w_in_body(w_ref, o_ref):
    w = w_ref[...]
    order = sorted(range(N_HEADS), key=_head_block_perm)
    q = jnp.concatenate([w[:, h * HEAD_DIM:(h + 1) * HEAD_DIM] for h in order], axis=1)
    o_ref[...] = jnp.concatenate([q, w[:, ATTN_W:]], axis=1).astype(BF16)


def _cast_w_in(w, layer, rows):
    _, k, n = w.shape
    return pl.pallas_call(
        _cast_w_in_body,
        grid=(1, k // rows),
        in_specs=[pl.BlockSpec((None, rows, n), lambda l, i: (layer, i, 0))],
        out_specs=pl.BlockSpec((None, rows, n), lambda l, i: (0, i, 0)),
        out_shape=jax.ShapeDtypeStruct((1, k, n), BF16),
        compiler_params=_params("arbitrary", "arbitrary"),
        name="cast_w_in",
    )(w)


def _cast_w_o_body(w_ref, o_ref):
    @pl.when(pl.program_id(1) == 0)
    def _():
        w = w_ref[...]
        order = sorted(range(N_HEADS), key=_head_block_perm)
        o_ref[...] = jnp.concatenate([w[h * HEAD_DIM:(h + 1) * HEAD_DIM] for h in order], axis=0).astype(BF16)

    @pl.when(pl.program_id(1) > 0)
    def _():
        o_ref[...] = w_ref[...].astype(BF16)


def _cast_w_o(w, layer):
    _, k, n = w.shape
    return pl.pallas_call(
        _cast_w_o_body,
        grid=(1, k // ATTN_W),
        in_specs=[pl.BlockSpec((None, ATTN_W, n), lambda l, i: (layer, i, 0))],
        out_specs=pl.BlockSpec((None, ATTN_W, n), lambda l, i: (0, i, 0)),
        out_shape=jax.ShapeDtypeStruct((1, k, n), BF16),
        compiler_params=_params("arbitrary", "arbitrary"),
        name="cast_w_o",
    )(w)


def _matmul_body(x_ref, w_ref, o_ref, xb_ref):
    @pl.when(pl.program_id(1) == 0)
    def _():
        xb_ref[...] = x_ref[...].astype(BF16)

    o_ref[...] = _dot(xb_ref[...], w_ref[...])


def _matmul(x, w, layer, tm, tn):
    m, k = x.shape
    n = w.shape[2]
    return pl.pallas_call(
        _matmul_body,
        grid=(m // tm, n // tn),
        in_specs=[pl.BlockSpec((tm, k), lambda i, j: (i, 0)),
                  pl.BlockSpec((None, k, tn), lambda i, j: (layer, 0, j))],
        out_specs=pl.BlockSpec((tm, tn), lambda i, j: (i, j)),
        out_shape=jax.ShapeDtypeStruct((m, n), F32),
        scratch_shapes=[pltpu.VMEM((tm, k), BF16)],
        compiler_params=_params("arbitrary", "arbitrary"),
        name="matmul",
    )(x, w)


def _mixers_body(decode, tq, new_rows, pos0, alpha, cast_tf, n, *refs):
    if decode:
        (q_ref, kc_ref, vc_ref, kp_ref, vp_ref, gb_ref, gc_ref, h_ref, pin_ref, us_ref, ps_state_ref,
         bias_ref, sink_ref, cw_ref, pw_ref, ps_ref,
         cat_ref, uo_ref, ko_ref, vo_ref, ext_u, ext_p) = refs
    else:
        (q_ref, kc_ref, vc_ref, kp_ref, vp_ref, gb_ref, gc_ref, h_ref, pin_ref,
         gcp_ref, hp_ref, pinp_ref,
         bias_ref, sink_ref, cw_ref, pw_ref, ps_ref, wo_ref, x_ref, lng_ref, lnb_ref) = refs[:21]
        ext_u, ext_p = refs[-2:]
        cast_src = refs[21:-4 - (len(refs) - 25) // 2]
        x1_ref, uo_ref = refs[21 + len(cast_src):23 + len(cast_src)]
        cast_dst = refs[23 + len(cast_src):-2]
        for k, (src, dst) in enumerate(zip(cast_src, cast_dst)):
            w = src[...]
            if k == 0:
                w = jnp.concatenate(
                    [w[:, (j % 2) * D_FF + (j // 2) * cast_tf:(j % 2) * D_FF + (j // 2 + 1) * cast_tf]
                     for j in range(2 * (D_FF // cast_tf))], axis=1)
            elif k == 2:
                order = sorted(range(N_HEADS), key=_head_block_perm)
                w = jnp.concatenate([w[:, h * HEAD_DIM:(h + 1) * HEAD_DIM] for h in order] + [w[:, ATTN_W:]], axis=1)
            dst[...] = w.astype(BF16)
    rows = GROUP * tq

    def new_rows_of(ref, nrows=tq):
        x = ref[...]
        if x.shape[0] == nrows:
            return x
        real = lax.broadcasted_iota(jnp.int32, (nrows, x.shape[1]), 0) < x.shape[0]
        return jnp.where(real, jnp.broadcast_to(x, (nrows, x.shape[1])), 0.0)

    nk = 2 * WINDOW
    q = new_rows_of(q_ref)
    qs = jnp.concatenate([q[:, g * KV_W:(g + 1) * KV_W] for g in range(GROUP)], axis=0)
    qs = (qs * ATTN_SCALE).astype(BF16)
    k_all = jnp.concatenate([kp_ref[...], new_rows_of(kc_ref, WINDOW)], axis=0).astype(BF16)
    v_all = jnp.concatenate([vp_ref[...], new_rows_of(vc_ref, WINDOW)], axis=0)
    if decode:
        for cache_ref, new_ref, out_ref in ((kp_ref, kc_ref, ko_ref), (vp_ref, vc_ref, vo_ref)):
            out_ref[0:WINDOW - new_rows] = cache_ref[new_rows:WINDOW]
            out_ref[WINDOW - new_rows:WINDOW] = new_ref[...]
    lane_head = lax.broadcasted_iota(jnp.int32, (nk, KV_W), 1) // HEAD_DIM
    k_cat = jnp.concatenate([jnp.where(lane_head == kvh, k_all, jnp.zeros((), BF16)) for kvh in range(KV_HEADS)],
                            axis=0)
    st_all = lax.dot_general(k_cat, qs, (((1,), (1,)), ((), ())), preferred_element_type=F32)
    kj = lax.broadcasted_iota(jnp.int32, (nk, rows), 0)
    qi = lax.broadcasted_iota(jnp.int32, (nk, rows), 1) & (tq - 1)
    mask = jnp.logical_and(kj > qi, kj <= qi + WINDOW)
    if not decode:
        mask = jnp.logical_and(mask, jnp.logical_or(n > 0, kj >= WINDOW))
    probs = []
    for kvh in range(KV_HEADS):
        s = jnp.where(mask, st_all[kvh * nk:(kvh + 1) * nk] + bias_ref[kvh], NEG_INF)
        sink = sink_ref[kvh]
        m = jnp.maximum(jnp.max(s, 0, keepdims=True), sink)
        p = jnp.exp(s - m)
        den = jnp.sum(p, 0, keepdims=True) + jnp.exp(sink - m)
        probs.append((p * (1.0 / den)).astype(BF16))
    pt_all = jnp.concatenate(probs, axis=0)
    vt = v_all.T
    row_head = lax.broadcasted_iota(jnp.int32, (KV_W, KV_HEADS * nk), 0) // HEAD_DIM
    col_head = lax.broadcasted_iota(jnp.int32, (KV_W, KV_HEADS * nk), 1) // nk
    vt_cat = jnp.where(row_head == col_head, jnp.concatenate([vt] * KV_HEADS, axis=1), 0.0).astype(BF16)
    o = _dot(vt_cat, pt_all).T
    attn = jnp.concatenate([o[g * tq:(g + 1) * tq] for g in range(GROUP)], axis=1)

    u = new_rows_of(gc_ref) * new_rows_of(h_ref)
    if decode:
        ext_u[HALO - (CONV_K - 1):HALO] = us_ref[...]
    else:
        ext_u[0:HALO] = jnp.where(n > 0, gcp_ref[...] * hp_ref[...], 0.0)
    ext_u[HALO:HALO + tq] = u
    cw = cw_ref[...]
    conv = ext_u[HALO - 2:HALO - 2 + tq] * cw[0:1]
    conv = conv + ext_u[HALO - 1:HALO - 1 + tq] * cw[1:2]
    conv = conv + u * cw[2:3]
    c = new_rows_of(gb_ref) * conv
    uo_ref[...] = ext_u[new_rows + HALO - (CONV_K - 1):new_rows + HALO]

    pin = new_rows_of(pin_ref)
    if decode:
        ext_p[0:HALO - POOL_PREV] = jnp.zeros((HALO - POOL_PREV, POOL_W), F32)
        ext_p[HALO - POOL_PREV:HALO] = ps_state_ref[...]
    else:
        ext_p[0:HALO] = jnp.where(n > 0, pinp_ref[...], 0.0)
    ext_p[HALO:HALO + tq] = pin
    pos = pos0 + n * tq + lax.broadcasted_iota(jnp.int32, (tq, 1), 0)
    assert POOL_WINDOWS == tuple(2 << g for g in range(len(POOL_WINDOWS)))
    sums = ext_p[...]
    trailing = []
    for w in POOL_WINDOWS:
        sums = sums + pltpu.roll(sums, w // 2, 0)
        trailing.append(sums[HALO:HALO + tq, 0:POOL_GROUP])
        sums = sums[:, POOL_GROUP:]
    pooled = []
    for g, w in enumerate(POOL_WINDOWS):
        lo = g * POOL_GROUP
        cur = pin[:, lo:lo + POOL_GROUP]
        win = trailing[g]
        cnt = jnp.minimum(pos + 1, w).astype(F32)
        d = (win / cnt - cur).astype(BF16)
        pooled.append(_dot(d, pw_ref[g]))
    pm = jnp.concatenate(pooled, axis=1) * ps_ref[...]

    cat = jnp.concatenate([attn, c, pm], axis=1).astype(BF16)
    if decode:
        cat_ref[...] = cat[0:new_rows]
    else:
        x1_ref[...] = _layer_norm(alpha * x_ref[...] + _dot(cat, wo_ref[...]), lng_ref[...], lnb_ref[...])


def _mixers_decode_body(nseq, tq, *refs):
    n_in, n_shared, n_out = 11, 5, 4
    for s in range(nseq):
        own = lambda group: [r.at[s] for r in group]
        _mixers_body(True, tq, 1, PAST_LEN, None, None, 0, *own(refs[:n_in]), *refs[n_in:n_in + n_shared],
                     *own(refs[n_in + n_shared:n_in + n_shared + n_out]), *own(refs[n_in + n_shared + n_out:]))


def _mixers_prompt_body(ntile, tq, alpha, cast_tf, *refs):
    x_ref, win_ref = refs[:2]
    shared = refs[2:10]
    ncast = (len(refs) - 17) // 2
    cast_src = refs[10:10 + ncast]
    x1, uo, kvo, pino = refs[10 + ncast:14 + ncast]
    cast_dst = refs[14 + ncast:-3]
    z_ref, ext_u, ext_p = refs[-3:]
    n = pl.program_id(1)
    rows = ntile * tq

    @pl.when(n == 0)
    def _():
        z_ref[0:WINDOW] = jnp.zeros((WINDOW, IN_W), F32)

    @pl.when(n > 0)
    def _():
        kv = slice(ATTN_W, ATTN_W + 2 * KV_W)
        halo = slice(IN_W - 2 * CONV_W - POOL_W, IN_W)
        z_ref[0:WINDOW, kv] = z_ref[rows:rows + WINDOW, kv]
        z_ref[WINDOW - HALO:WINDOW, halo] = z_ref[rows + WINDOW - HALO:rows + WINDOW, halo]

    z_ref[WINDOW:WINDOW + rows] = _dot(x_ref[...].astype(BF16), win_ref[...])
    kvo[...] = z_ref[rows:rows + WINDOW, ATTN_W:ATTN_W + 2 * KV_W]
    pino[...] = z_ref[rows + WINDOW - HALO:rows + WINDOW, IN_W - POOL_W:IN_W]

    def cols(blk, width):
        return pl.ds(blk * width, width)

    for j in range(ntile):
        base = WINDOW + j * tq
        own = lambda blk, width, base=base: z_ref.at[pl.ds(base, tq), cols(blk, width)]
        above = lambda nrows, blk, width, base=base: z_ref.at[pl.ds(base - nrows, nrows), cols(blk, width)]
        tile_shared = shared[:6] + (x_ref.at[pl.ds(j * tq, tq)],) + shared[6:]
        _mixers_body(False, tq, tq, 0, alpha, cast_tf if j == 0 else None, n * ntile + j,
                     own(Q_BLK, ATTN_W), own(K_BLK, KV_W), own(V_BLK, KV_W),
                     above(WINDOW, K_BLK, KV_W), above(WINDOW, V_BLK, KV_W),
                     own(GB_BLK, CONV_W), own(GC_BLK, CONV_W), own(H_BLK, CONV_W), own(PIN_BLK, POOL_W),
                     above(HALO, GC_BLK, CONV_W), above(HALO, H_BLK, CONV_W), above(HALO, PIN_BLK, POOL_W),
                     *tile_shared, *(cast_src if j == 0 else ()), x1.at[pl.ds(j * tq, tq)], uo,
                     *(cast_dst if j == 0 else ()), ext_u.at[j], ext_p.at[j])


def _mixers_prompt(x, w_in, tables, cw, pw, ps, w_o, layer, ln_g, ln_b, alpha, cast_next=None):
    b, s, _ = x.shape
    tq = WINDOW
    ntile = PROMPT_TILES_PER_STEP
    rows = ntile * tq
    steps = s // rows
    bias, sink = tables
    const3 = lambda i, n: (0, 0, 0)
    const2 = lambda i, n: (0, 0)
    per_seq = lambda nrows, width: pl.BlockSpec((None, nrows, width), lambda i, n: (i, 0, 0))
    resident = lambda k, width: pl.BlockSpec((None, k, width), lambda i, n: (0, 0, 0),
                                             pipeline_mode=pl.Buffered(1))
    in_specs = [
        pl.BlockSpec((None, rows, D_MODEL), lambda i, n: (i, n, 0)),
        resident(D_MODEL, IN_W),
        pl.BlockSpec(bias.shape, const3),
        pl.BlockSpec(sink.shape, const3),
        pl.BlockSpec(cw.shape, const2),
        pl.BlockSpec(pw.shape, const3),
        pl.BlockSpec(ps.shape, const2),
        resident(D_MODEL, D_MODEL),
        pl.BlockSpec((None, 1, D_MODEL), lambda i, n: (layer, 0, 0)),
        pl.BlockSpec((None, 1, D_MODEL), lambda i, n: (layer, 0, 0)),
    ]
    operands = [x, w_in, bias, sink, cw, pw, ps, w_o, ln_g, ln_b]
    out_specs = [pl.BlockSpec((None, rows, D_MODEL), lambda i, n: (i, n, 0)),
                 per_seq(CONV_K - 1, CONV_W), per_seq(WINDOW, 2 * KV_W), per_seq(HALO, POOL_W)]
    out_shape = [jax.ShapeDtypeStruct((b, s, D_MODEL), F32),
                 jax.ShapeDtypeStruct((b, CONV_K - 1, CONV_W), F32),
                 jax.ShapeDtypeStruct((b, WINDOW, 2 * KV_W), F32),
                 jax.ShapeDtypeStruct((b, HALO, POOL_W), F32)]
    cast_tf = None
    if cast_next is not None:
        weights, cast_tf = cast_next
        for pos, (w, cast_layer) in enumerate(weights):
            _, k, width = w.shape
            wrows = next(r for r in range(2 * SUBLANES, k + 1, 2 * SUBLANES) if k % r == 0 and k // r <= b * steps)
            if pos == 3:
                assert wrows <= HEAD_DIM
                wrows = HEAD_DIM
            nblocks = k // wrows
            block = lambda i, n, nblocks=nblocks: jnp.minimum(i * steps + n, nblocks - 1)
            out_block = block
            if pos == 3:
                out_block = lambda i, n, block=block: jnp.where(block(i, n) < N_HEADS, _head_block_perm(block(i, n)),
                                                                block(i, n))
            in_specs.append(pl.BlockSpec((None, wrows, width),
                                         lambda i, n, block=block, cl=cast_layer: (cl, block(i, n), 0)))
            operands.append(w)
            out_specs.append(pl.BlockSpec((None, wrows, width), lambda i, n, out_block=out_block: (0, out_block(i, n), 0)))
            out_shape.append(jax.ShapeDtypeStruct((1, k, width), BF16))
    return pl.pallas_call(
        functools.partial(_mixers_prompt_body, ntile, tq, alpha, cast_tf),
        grid=(b, steps),
        in_specs=in_specs,
        out_specs=out_specs,
        out_shape=out_shape,
        scratch_shapes=[pltpu.VMEM((WINDOW + rows, IN_W), F32),
                        pltpu.VMEM((ntile, HALO + tq, CONV_W), F32), pltpu.VMEM((ntile, HALO + tq, POOL_W), F32)],
        compiler_params=_params("arbitrary", "arbitrary"),
        name="mixers_prompt",
    )(*operands)


def _mixers_decode(zs, ck, cv, conv_state, pool_state, tables, cw, pw, ps):
    b = zs.shape[0]
    tq = DECODE_ROWS
    nseq = DECODE_SEQS_PER_STEP
    bias, sink = tables
    new = lambda width, blk: pl.BlockSpec((nseq, 1, width), lambda i, n: (i, 0, blk))
    whole = lambda rows, width: pl.BlockSpec((nseq, rows, width), lambda i, n: (i, 0, 0))
    const3 = lambda i, n: (0, 0, 0)
    const2 = lambda i, n: (0, 0)
    in_specs = [
        new(ATTN_W, Q_BLK), new(KV_W, K_BLK), new(KV_W, V_BLK),
        whole(WINDOW, KV_W), whole(WINDOW, KV_W),
        new(CONV_W, GB_BLK), new(CONV_W, GC_BLK), new(CONV_W, H_BLK), new(POOL_W, PIN_BLK),
        whole(CONV_K - 1, CONV_W), whole(POOL_PREV, POOL_W),
        pl.BlockSpec(bias.shape, const3),
        pl.BlockSpec(sink.shape, const3),
        pl.BlockSpec(cw.shape, const2),
        pl.BlockSpec(pw.shape, const3),
        pl.BlockSpec(ps.shape, const2),
    ]
    return pl.pallas_call(
        functools.partial(_mixers_decode_body, nseq, tq),
        grid=(b // nseq, 1),
        in_specs=in_specs,
        out_specs=[whole(1, D_MODEL), whole(CONV_K - 1, CONV_W), whole(WINDOW, KV_W), whole(WINDOW, KV_W)],
        out_shape=[jax.ShapeDtypeStruct((b, 1, D_MODEL), BF16),
                   jax.ShapeDtypeStruct((b, CONV_K - 1, CONV_W), F32),
                   jax.ShapeDtypeStruct((b, WINDOW, KV_W), F32),
                   jax.ShapeDtypeStruct((b, WINDOW, KV_W), F32)],
        scratch_shapes=[pltpu.VMEM((nseq, HALO + tq, CONV_W), F32), pltpu.VMEM((nseq, HALO + tq, POOL_W), F32)],
        compiler_params=_params("arbitrary", "arbitrary"),
        name="mixers_decode",
    )(zs, zs, zs, ck, cv, zs, zs, zs, zs, conv_state, pool_state, bias, sink, cw, pw, ps)


def _bias_table(bias_by_dist, tq):
    nk = 2 * WINDOW
    by_offset = jnp.zeros((N_HEADS, nk + 1), F32).at[:, 1:WINDOW + 1].set(bias_by_dist[::-1].T)
    rows = jnp.tile(by_offset, (1, tq))[:, :tq * nk].reshape(N_HEADS, tq, nk)
    return jnp.swapaxes(rows.reshape(KV_HEADS, GROUP * tq, nk), 1, 2)


def _sink_table(sinks_l, tq):
    return jnp.broadcast_to(sinks_l.astype(F32).reshape(KV_HEADS, GROUP, 1, 1),
                            (KV_HEADS, GROUP, 1, tq)).reshape(KV_HEADS, 1, GROUP * tq)


def _t5_bucket(n):
    max_exact = N_BUCKETS // 2
    nf = jnp.maximum(n, 1).astype(F32)
    large = max_exact + (jnp.log(nf / max_exact) / math.log(MAX_DISTANCE / max_exact)
                         * (N_BUCKETS - max_exact)).astype(jnp.int32)
    large = jnp.minimum(large, N_BUCKETS - 1)
    return jnp.where(n < max_exact, n, large)


def _proj_ln_body(alpha, a_ref, w_ref, x_ref, g_ref, b_ref, o_ref):
    o_ref[...] = _layer_norm(alpha * x_ref[...] + _dot(a_ref[...], w_ref[...]), g_ref[...], b_ref[...])


def _proj_ln(a, w, w_layer, x, g, b, layer, alpha, tm):
    m, k = a.shape
    n = w.shape[2]
    return pl.pallas_call(
        functools.partial(_proj_ln_body, alpha),
        grid=(m // tm,),
        in_specs=[pl.BlockSpec((tm, k), lambda i: (i, 0)),
                  pl.BlockSpec((None, k, n), lambda i: (w_layer, 0, 0)),
                  pl.BlockSpec((tm, n), lambda i: (i, 0)),
                  pl.BlockSpec((None, 1, n), lambda i: (layer, 0, 0)),
                  pl.BlockSpec((None, 1, n), lambda i: (layer, 0, 0))],
        out_specs=pl.BlockSpec((tm, n), lambda i: (i, 0)),
        out_shape=jax.ShapeDtypeStruct((m, n), F32),
        compiler_params=_params("arbitrary"),
        name="proj_ln",
    )(a, w, x, g, b)


def _ffn_body(tm, tf, ns, tiles_per_seq, nc, alpha, x_ref, wu_ref, wd_ref, cwg_ref, cwv_ref, g_ref, b_ref,
              xs_ref, sg_ref, sv_ref, o_ref, tail_g_ref, tail_v_ref, xso_ref, ugo_ref, uvo_ref,
              xb_ref, ext, carry, xsb_ref, accs_ref):
    i = pl.program_id(0)
    c = pl.program_id(1)
    ts = tm // ns

    @pl.when(c == 0)
    def _():
        xb_ref[...] = x_ref[...].astype(BF16)
        o_ref[...] = jnp.zeros_like(o_ref)

    seq_start = (i % tiles_per_seq) == 0

    def up_proj(h, prev_tail):
        up = _dot(xb_ref[pl.ds(h * ts, ts)], wu_ref[...])
        if h == 0:
            @pl.when(seq_start)
            def _():
                ext[0, 0:SUBLANES] = jnp.zeros((SUBLANES, 2 * tf), F32)

            @pl.when(jnp.logical_not(seq_start))
            def _():
                ext[0, 0:SUBLANES] = carry[c]
        else:
            ext[h, 0:SUBLANES] = prev_tail
        ext[h, SUBLANES:SUBLANES + ts] = up
        tail = up[ts - SUBLANES:ts]
        if h == ns - 1:
            carry[c] = tail
            tail_g_ref[...] = tail[:, :tf]
            tail_v_ref[...] = tail[:, tf:]
        return tail

    def gate(h):
        cw = jnp.concatenate([cwg_ref[...], cwv_ref[...]], axis=1)
        hc = ext[h, SUBLANES - 2:SUBLANES - 2 + ts] * cw[0:1]
        hc = hc + ext[h, SUBLANES - 1:SUBLANES - 1 + ts] * cw[1:2]
        hc = hc + ext[h, SUBLANES:SUBLANES + ts] * cw[2:3]
        return (jax.nn.silu(hc[:, :tf]) * hc[:, tf:]).astype(BF16)

    def down_proj(h, act):
        o_ref[pl.ds(h * ts, ts)] += _dot(act, wd_ref[...])

    tail = up_proj(0, None)
    for h in range(ns):
        act = gate(h)
        if h + 1 < ns:
            tail = up_proj(h + 1, tail)
        down_proj(h, act)

    @pl.when(c == nc - 1)
    def _():
        o_ref[...] = _layer_norm(alpha * x_ref[...] + o_ref[...], g_ref[...], b_ref[...])

    @pl.when(i == pl.num_programs(0) - 1)
    def _():
        @pl.when(c == 0)
        def _():
            xsb_ref[...] = xs_ref[...].astype(BF16)
            accs_ref[...] = jnp.zeros_like(accs_ref)

        up = _dot(xsb_ref[...], wu_ref[...])
        ugo_ref[...] = up[:, :tf]
        uvo_ref[...] = up[:, tf:]

        def conv(u, s_ref, cw_ref):
            cw = cw_ref[...]
            y = s_ref[:, 0, :] * cw[0:1]
            y = y + s_ref[:, 1, :] * cw[1:2]
            return y + u * cw[2:3]

        act = (jax.nn.silu(conv(up[:, :tf], sg_ref, cwg_ref)) * conv(up[:, tf:], sv_ref, cwv_ref)).astype(BF16)
        accs_ref[...] += _dot(act, wd_ref[...])

        @pl.when(c == nc - 1)
        def _():
            xso_ref[...] = _layer_norm(alpha * xs_ref[...] + accs_ref[...], g_ref[...], b_ref[...])


def _ffn_prompt(x, xs, state, w_up_il, w_down, w_layer, cw, g, b, layer, alpha, seq, tm, tf, ns):
    m = x.shape[0]
    bs = xs.shape[0]
    nc = D_FF // tf
    tiles_per_seq = seq // tm
    last_tile = m // tm - 1
    tail_spec = pl.BlockSpec((None, SUBLANES, tf), lambda i, c: (i, 0, c))
    tail_shape = jax.ShapeDtypeStruct((m // tm, SUBLANES, D_FF), F32)
    dec_chunk = lambda i, c: jnp.where(i == last_tile, c, 0)
    dec_spec = pl.BlockSpec((bs, tf), lambda i, c: (0, dec_chunk(i, c)))
    dec_shape = jax.ShapeDtypeStruct((bs, D_FF), F32)
    out, tail_g, tail_v, xs_new, up_g, up_v = pl.pallas_call(
        functools.partial(_ffn_body, tm, tf, ns, tiles_per_seq, nc, alpha),
        grid=(m // tm, nc),
        in_specs=[pl.BlockSpec((tm, D_MODEL), lambda i, c: (i, 0), pipeline_mode=pl.Buffered(1)),
                  pl.BlockSpec((None, D_MODEL, 2 * tf), lambda i, c: (w_layer, 0, c)),
                  pl.BlockSpec((None, tf, D_MODEL), lambda i, c: (w_layer, c, 0)),
                  pl.BlockSpec((None, CONV_K, tf), lambda i, c: (layer, 0, c)),
                  pl.BlockSpec((None, CONV_K, tf), lambda i, c: (layer, 0, nc + c)),
                  pl.BlockSpec((None, 1, D_MODEL), lambda i, c: (layer, 0, 0)),
                  pl.BlockSpec((None, 1, D_MODEL), lambda i, c: (layer, 0, 0)),
                  pl.BlockSpec((bs, D_MODEL), lambda i, c: (0, 0)),
                  pl.BlockSpec((bs, CONV_K - 1, tf), lambda i, c: (0, 0, dec_chunk(i, c))),
                  pl.BlockSpec((bs, CONV_K - 1, tf), lambda i, c: (0, 0, nc + dec_chunk(i, c)))],
        out_specs=[pl.BlockSpec((tm, D_MODEL), lambda i, c: (i, 0)), tail_spec, tail_spec,
                   pl.BlockSpec((bs, D_MODEL), lambda i, c: (0, 0)), dec_spec, dec_spec],
        out_shape=[jax.ShapeDtypeStruct((m, D_MODEL), F32), tail_shape, tail_shape,
                   jax.ShapeDtypeStruct((bs, D_MODEL), F32), dec_shape, dec_shape],
        scratch_shapes=[pltpu.VMEM((tm, D_MODEL), BF16),
                        pltpu.VMEM((ns, SUBLANES + tm // ns, 2 * tf), F32),
                        pltpu.VMEM((nc, SUBLANES, 2 * tf), F32),
                        pltpu.VMEM((bs, D_MODEL), BF16),
                        pltpu.VMEM((bs, D_MODEL), F32)],
        compiler_params=_params("arbitrary", "arbitrary"),
        name="ffn_prompt",
    )(x, w_up_il, w_down, cw, cw, g, b, xs, state, state)
    last = slice(tiles_per_seq - 1, None, tiles_per_seq)
    return out, jnp.concatenate([tail_g[last], tail_v[last]], -1), xs_new, jnp.concatenate([up_g, up_v], -1)


def kernel(x_prompt, x_sample, cache_k, cache_v, state_conv, state_pool, state_ffn, rel_table, w_in, conv_w,
           pool_w, pool_scale, sinks, w_o, ln1_g, ln1_b, w_up, ffn_conv_w, w_down, ln2_g, ln2_b):
    depth = w_in.shape[0]
    bp_, seq, _ = x_prompt.shape
    bs = x_sample.shape[0]
    assert x_sample.shape[1] == 1 and cache_k.shape[2] == WINDOW
    alpha = (2 * depth) ** 0.25

    w_in_b, w_o_b = [_cast_w_in(w_in, 0, CAST_ROWS)], [_cast_w_o(w_o, 0)]
    w_up_il, w_down_b = [], []
    pool_w_b = pool_w.astype(BF16)

    bias_by_dist = rel_table.astype(F32)[_t5_bucket(jnp.arange(WINDOW))]
    bias_p = _bias_table(bias_by_dist, WINDOW)
    bias_s = _bias_table(bias_by_dist, DECODE_ROWS)

    ln1_g, ln1_b, ln2_g, ln2_b = (t[:, None, :] for t in (ln1_g, ln1_b, ln2_g, ln2_b))

    xp = x_prompt.reshape(bp_ * seq, D_MODEL)
    xs = x_sample.reshape(bs, D_MODEL)

    outs_p, outs_s = [], []
    for l in range(depth):
        tab_p = (bias_p, _sink_table(sinks[l], WINDOW))
        tab_s = (bias_s, _sink_table(sinks[l], DECODE_ROWS))
        ps = pool_scale[l][None, :]

        side = [(w_up, l), (w_down, l)] + ([(w_in, l + 1), (w_o, l + 1)] if l + 1 < depth else [])
        x1, u_tail, kv_tail, pin_tail, *converted = _mixers_prompt(
            xp.reshape(bp_, seq, D_MODEL), w_in_b[l], tab_p, conv_w[l], pool_w_b[l], ps, w_o_b[l], l, ln1_g, ln1_b,
            alpha, (side, FFN_TF))
        for store, wb in zip((w_up_il, w_down_b, w_in_b, w_o_b), converted):
            store.append(wb)
        x1 = x1.reshape(bp_ * seq, D_MODEL)

        zs = _matmul(xs, w_in_b[l], 0, bs, DECODE_TN)
        ck = cache_k[l].reshape(bs, WINDOW, KV_W)
        cv = cache_v[l].reshape(bs, WINDOW, KV_W)
        cat_s, u_new, k_s, v_s = _mixers_decode(zs[:, None, :], ck, cv, state_conv[l], state_pool[l], tab_s,
                                                conv_w[l], pool_w_b[l], ps)
        x1s = _proj_ln(cat_s.reshape(bs, D_MODEL), w_o_b[l], 0, xs, ln1_g, ln1_b, l, alpha, bs)

        xp, up_tail, xs, up_s = _ffn_prompt(x1, x1s, state_ffn[l], w_up_il[l], w_down_b[l], 0, ffn_conv_w, ln2_g, ln2_b,
                                            l, alpha, seq, FFN_TM, FFN_TF, FFN_NS)
        outs_p.append((
            kv_tail[:, :, :KV_W].reshape(bp_, WINDOW, KV_HEADS, HEAD_DIM),
            kv_tail[:, :, KV_W:].reshape(bp_, WINDOW, KV_HEADS, HEAD_DIM),
            u_tail,
            pin_tail[:, HALO - POOL_PREV:],
            up_tail[:, SUBLANES - (CONV_K - 1):],
        ))

        outs_s.append((
            k_s.reshape(bs, WINDOW, KV_HEADS, HEAD_DIM),
            v_s.reshape(bs, WINDOW, KV_HEADS, HEAD_DIM),
            u_new,
            jnp.concatenate([state_pool[l][:, 1:], zs[:, None, IN_W - POOL_W:]], 1),
            jnp.concatenate([state_ffn[l][:, 1:], up_s[:, None, :]], 1),
        ))

    st = lambda lst, i: jnp.stack([e[i] for e in lst], 0)
    return (xp.reshape(bp_, seq, D_MODEL), xs.reshape(bs, 1, D_MODEL),
            st(outs_p, 0), st(outs_p, 1), st(outs_p, 2), st(outs_p, 3), st(outs_p, 4),
            st(outs_s, 0), st(outs_s, 1), st(outs_s, 2), st(outs_s, 3), st(outs_s, 4))
```

```python
import functools
import math

import jax
import jax.numpy as jnp
from jax import lax
from jax.experimental import pallas as pl
from jax.experimental.pallas import tpu as pltpu

D_MODEL = 2048
N_HEADS = 16
KV_HEADS = 4
HEAD_DIM = 64
GROUP = N_HEADS // KV_HEADS
ATTN_W = N_HEADS * HEAD_DIM
KV_W = KV_HEADS * HEAD_DIM
WINDOW = 128
ATTN_SCALE = HEAD_DIM ** -0.5
NEG_INF = -1e30
N_BUCKETS = 32
MAX_DISTANCE = 128
CONV_W = D_MODEL // 4
CONV_K = 3
POOL_W = D_MODEL // 4
POOL_WINDOWS = (2, 4, 8, 16)
POOL_GROUP = POOL_W // len(POOL_WINDOWS)
POOL_PREV = 15
IN_W = ATTN_W + 2 * KV_W + 3 * CONV_W + POOL_W
D_FF = 5632
PAST_LEN = 16384
LN_EPS = 1e-5

HALO = 16
SUBLANES = 8
DECODE_ROWS = 128 // GROUP
DECODE_SEQS_PER_STEP = 8
DECODE_TN = IN_W // 2
CAST_ROWS = 256
PROMPT_TILES_PER_STEP = 2
Q_BLK = 0
K_BLK = ATTN_W // KV_W
V_BLK = K_BLK + 1
GB_BLK = (ATTN_W + 2 * KV_W) // CONV_W
GC_BLK = GB_BLK + 1
H_BLK = GB_BLK + 2
PIN_BLK = GB_BLK + 3

FFN_TM = 1024
FFN_TF = 512
FFN_NS = 4

VMEM_LIMIT = 60 * 1024 * 1024

BF16 = jnp.bfloat16
F32 = jnp.float32


def _params(*sem):
    return pltpu.CompilerParams(dimension_semantics=sem, vmem_limit_bytes=VMEM_LIMIT)


def _dot(a, b):
    return jnp.dot(a, b, preferred_element_type=F32)


def _layer_norm(y, g, b):
    mu = jnp.mean(y, -1, keepdims=True)
    yc = y - mu
    var = jnp.mean(yc * yc, -1, keepdims=True)
    return yc * lax.rsqrt(var + LN_EPS) * g + b


def _head_block_perm(blk):
    return (blk % GROUP) * KV_HEADS + blk // GROUP


def _cast_w_in_body(w_ref, o_ref):
    w = w_ref[...]
    order = sorted(range(N_HEADS), key=_head_block_perm)
    q = jnp.concatenate([w[:, h * HEAD_DIM:(h + 1) * HEAD_DIM] for h in order], axis=1)
    o_ref[...] = jnp.concatenate([q, w[:, ATTN_W:]], axis=1).astype(BF16)


def _cast_w_in(w, layer, rows):
    _, k, n = w.shape
    return pl.pallas_call(
        _cast_w_in_body,
        grid=(1, k // rows),
        in_specs=[pl.BlockSpec((None, rows, n), lambda l, i: (layer, i, 0))],
        out_specs=pl.BlockSpec((None, rows, n), lambda l, i: (0, i, 0)),
        out_shape=jax.ShapeDtypeStruct((1, k, n), BF16),
        compiler_params=_params("arbitrary", "arbitrary"),
        name="cast_w_in",
    )(w)


def _cast_w_o_body(w_ref, o_ref):
    @pl.when(pl.program_id(1) == 0)
    def _():
        w = w_ref[...]
        order = sorted(range(N_HEADS), key=_head_block_perm)
        o_ref[...] = jnp.concatenate([w[h * HEAD_DIM:(h + 1) * HEAD_DIM] for h in order], axis=0).astype(BF16)

    @pl.when(pl.program_id(1) > 0)
    def _():
        o_ref[...] = w_ref[...].astype(BF16)


def _cast_w_o(w, layer):
    _, k, n = w.shape
    return pl.pallas_call(
        _cast_w_o_body,
        grid=(1, k // ATTN_W),
        in_specs=[pl.BlockSpec((None, ATTN_W, n), lambda l, i: (layer, i, 0))],
        out_specs=pl.BlockSpec((None, ATTN_W, n), lambda l, i: (0, i, 0)),
        out_shape=jax.ShapeDtypeStruct((1, k, n), BF16),
        compiler_params=_params("arbitrary", "arbitrary"),
        name="cast_w_o",
    )(w)


def _matmul_body(x_ref, w_ref, o_ref, xb_ref):
    @pl.when(pl.program_id(1) == 0)
    def _():
        xb_ref[...] = x_ref[...].astype(BF16)

    o_ref[...] = _dot(xb_ref[...], w_ref[...])


def _matmul(x, w, layer, tm, tn):
    m, k = x.shape
    n = w.shape[2]
    return pl.pallas_call(
        _matmul_body,
        grid=(m // tm, n // tn),
        in_specs=[pl.BlockSpec((tm, k), lambda i, j: (i, 0)),
                  pl.BlockSpec((None, k, tn), lambda i, j: (layer, 0, j))],
        out_specs=pl.BlockSpec((tm, tn), lambda i, j: (i, j)),
        out_shape=jax.ShapeDtypeStruct((m, n), F32),
        scratch_shapes=[pltpu.VMEM((tm, k), BF16)],
        compiler_params=_params("arbitrary", "arbitrary"),
        name="matmul",
    )(x, w)


def _mixers_body(decode, tq, new_rows, pos0, alpha, cast_tf, n, *refs):
    if decode:
        (q_ref, kc_ref, vc_ref, kp_ref, vp_ref, gb_ref, gc_ref, h_ref, pin_ref, us_ref, ps_state_ref,
         bias_ref, sink_ref, cw_ref, pw_ref, ps_ref,
         cat_ref, uo_ref, ko_ref, vo_ref, ext_u, ext_p) = refs
    else:
        (q_ref, kc_ref, vc_ref, kp_ref, vp_ref, gb_ref, gc_ref, h_ref, pin_ref,
         gcp_ref, hp_ref, pinp_ref,
         bias_ref, sink_ref, cw_ref, pw_ref, ps_ref, wo_ref, x_ref, lng_ref, lnb_ref) = refs[:21]
        ext_u, ext_p = refs[-2:]
        cast_src = refs[21:-4 - (len(refs) - 25) // 2]
        x1_ref, uo_ref = refs[21 + len(cast_src):23 + len(cast_src)]
        cast_dst = refs[23 + len(cast_src):-2]
        for k, (src, dst) in enumerate(zip(cast_src, cast_dst)):
            w = src[...]
            if k == 0:
                w = jnp.concatenate(
                    [w[:, (j % 2) * D_FF + (j // 2) * cast_tf:(j % 2) * D_FF + (j // 2 + 1) * cast_tf]
                     for j in range(2 * (D_FF // cast_tf))], axis=1)
            elif k == 2:
                order = sorted(range(N_HEADS), key=_head_block_perm)
                w = jnp.concatenate([w[:, h * HEAD_DIM:(h + 1) * HEAD_DIM] for h in order] + [w[:, ATTN_W:]], axis=1)
            dst[...] = w.astype(BF16)
    rows = GROUP * tq

    def new_rows_of(ref, nrows=tq):
        x = ref[...]
        if x.shape[0] == nrows:
            return x
        real = lax.broadcasted_iota(jnp.int32, (nrows, x.shape[1]), 0) < x.shape[0]
        return jnp.where(real, jnp.broadcast_to(x, (nrows, x.shape[1])), 0.0)

    nk = 2 * WINDOW
    q = new_rows_of(q_ref)
    qs = jnp.concatenate([q[:, g * KV_W:(g + 1) * KV_W] for g in range(GROUP)], axis=0)
    qs = (qs * ATTN_SCALE).astype(BF16)
    k_all = jnp.concatenate([kp_ref[...], new_rows_of(kc_ref, WINDOW)], axis=0).astype(BF16)
    v_all = jnp.concatenate([vp_ref[...], new_rows_of(vc_ref, WINDOW)], axis=0)
    if decode:
        for cache_ref, new_ref, out_ref in ((kp_ref, kc_ref, ko_ref), (vp_ref, vc_ref, vo_ref)):
            out_ref[0:WINDOW - new_rows] = cache_ref[new_rows:WINDOW]
            out_ref[WINDOW - new_rows:WINDOW] = new_ref[...]
    lane_head = lax.broadcasted_iota(jnp.int32, (nk, KV_W), 1) // HEAD_DIM
    k_cat = jnp.concatenate([jnp.where(lane_head == kvh, k_all, jnp.zeros((), BF16)) for kvh in range(KV_HEADS)],
                            axis=0)
    st_all = lax.dot_general(k_cat, qs, (((1,), (1,)), ((), ())), preferred_element_type=F32)
    kj = lax.broadcasted_iota(jnp.int32, (nk, rows), 0)
    qi = lax.broadcasted_iota(jnp.int32, (nk, rows), 1) & (tq - 1)
    mask = jnp.logical_and(kj > qi, kj <= qi + WINDOW)
    if not decode:
        mask = jnp.logical_and(mask, jnp.logical_or(n > 0, kj >= WINDOW))
    probs = []
    for kvh in range(KV_HEADS):
        s = jnp.where(mask, st_all[kvh * nk:(kvh + 1) * nk] + bias_ref[kvh], NEG_INF)
        sink = sink_ref[kvh]
        m = jnp.maximum(jnp.max(s, 0, keepdims=True), sink)
        p = jnp.exp(s - m)
        den = jnp.sum(p, 0, keepdims=True) + jnp.exp(sink - m)
        probs.append((p * (1.0 / den)).astype(BF16))
    pt_all = jnp.concatenate(probs, axis=0)
    vt = v_all.T
    row_head = lax.broadcasted_iota(jnp.int32, (KV_W, KV_HEADS * nk), 0) // HEAD_DIM
    col_head = lax.broadcasted_iota(jnp.int32, (KV_W, KV_HEADS * nk), 1) // nk
    vt_cat = jnp.where(row_head == col_head, jnp.concatenate([vt] * KV_HEADS, axis=1), 0.0).astype(BF16)
    o = _dot(vt_cat, pt_all).T
    attn = jnp.concatenate([o[g * tq:(g + 1) * tq] for g in range(GROUP)], axis=1)

    u = new_rows_of(gc_ref) * new_rows_of(h_ref)
    if decode:
        ext_u[HALO - (CONV_K - 1):HALO] = us_ref[...]
    else:
        ext_u[0:HALO] = jnp.where(n > 0, gcp_ref[...] * hp_ref[...], 0.0)
    ext_u[HALO:HALO + tq] = u
    cw = cw_ref[...]
    conv = ext_u[HALO - 2:HALO - 2 + tq] * cw[0:1]
    conv = conv + ext_u[HALO - 1:HALO - 1 + tq] * cw[1:2]
    conv = conv + u * cw[2:3]
    c = new_rows_of(gb_ref) * conv
    uo_ref[...] = ext_u[new_rows + HALO - (CONV_K - 1):new_rows + HALO]

    pin = new_rows_of(pin_ref)
    if decode:
        ext_p[0:HALO - POOL_PREV] = jnp.zeros((HALO - POOL_PREV, POOL_W), F32)
        ext_p[HALO - POOL_PREV:HALO] = ps_state_ref[...]
    else:
        ext_p[0:HALO] = jnp.where(n > 0, pinp_ref[...], 0.0)
    ext_p[HALO:HALO + tq] = pin
    pos = pos0 + n * tq + lax.broadcasted_iota(jnp.int32, (tq, 1), 0)
    assert POOL_WINDOWS == tuple(2 << g for g in range(len(POOL_WINDOWS)))
    sums = ext_p[...]
    trailing = []
    for w in POOL_WINDOWS:
        sums = sums + pltpu.roll(sums, w // 2, 0)
        trailing.append(sums[HALO:HALO + tq, 0:POOL_GROUP])
        sums = sums[:, POOL_GROUP:]
    pooled = []
    for g, w in enumerate(POOL_WINDOWS):
        lo = g * POOL_GROUP
        cur = pin[:, lo:lo + POOL_GROUP]
        win = trailing[g]
        cnt = jnp.minimum(pos + 1, w).astype(F32)
        d = (win / cnt - cur).astype(BF16)
        pooled.append(_dot(d, pw_ref[g]))
    pm = jnp.concatenate(pooled, axis=1) * ps_ref[...]

    cat = jnp.concatenate([attn, c, pm], axis=1).astype(BF16)
    if decode:
        cat_ref[...] = cat[0:new_rows]
    else:
        x1_ref[...] = _layer_norm(alpha * x_ref[...] + _dot(cat, wo_ref[...]), lng_ref[...], lnb_ref[...])


def _mixers_decode_body(nseq, tq, *refs):
    n_in, n_shared, n_out = 11, 5, 4
    for s in range(nseq):
        own = lambda group: [r.at[s] for r in group]
        _mixers_body(True, tq, 1, PAST_LEN, None, None, 0, *own(refs[:n_in]), *refs[n_in:n_in + n_shared],
                     *own(refs[n_in + n_shared:n_in + n_shared + n_out]), *own(refs[n_in + n_shared + n_out:]))


def _mixers_prompt_body(ntile, tq, alpha, cast_tf, *refs):
    x_ref, win_ref = refs[:2]
    shared = refs[2:10]
    ncast = (len(refs) - 17) // 2
    cast_src = refs[10:10 + ncast]
    x1, uo, kvo, pino = refs[10 + ncast:14 + ncast]
    cast_dst = refs[14 + ncast:-3]
    z_ref, ext_u, ext_p = refs[-3:]
    n = pl.program_id(1)
    rows = ntile * tq

    @pl.when(n == 0)
    def _():
        z_ref[0:WINDOW] = jnp.zeros((WINDOW, IN_W), F32)

    @pl.when(n > 0)
    def _():
        kv = slice(ATTN_W, ATTN_W + 2 * KV_W)
        halo = slice(IN_W - 2 * CONV_W - POOL_W, IN_W)
        z_ref[0:WINDOW, kv] = z_ref[rows:rows + WINDOW, kv]
        z_ref[WINDOW - HALO:WINDOW, halo] = z_ref[rows + WINDOW - HALO:rows + WINDOW, halo]

    z_ref[WINDOW:WINDOW + rows] = _dot(x_ref[...].astype(BF16), win_ref[...])
    kvo[...] = z_ref[rows:rows + WINDOW, ATTN_W:ATTN_W + 2 * KV_W]
    pino[...] = z_ref[rows + WINDOW - HALO:rows + WINDOW, IN_W - POOL_W:IN_W]

    def cols(blk, width):
        return pl.ds(blk * width, width)

    for j in range(ntile):
        base = WINDOW + j * tq
        own = lambda blk, width, base=base: z_ref.at[pl.ds(base, tq), cols(blk, width)]
        above = lambda nrows, blk, width, base=base: z_ref.at[pl.ds(base - nrows, nrows), cols(blk, width)]
        tile_shared = shared[:6] + (x_ref.at[pl.ds(j * tq, tq)],) + shared[6:]
        _mixers_body(False, tq, tq, 0, alpha, cast_tf if j == 0 else None, n * ntile + j,
                     own(Q_BLK, ATTN_W), own(K_BLK, KV_W), own(V_BLK, KV_W),
                     above(WINDOW, K_BLK, KV_W), above(WINDOW, V_BLK, KV_W),
                     own(GB_BLK, CONV_W), own(GC_BLK, CONV_W), own(H_BLK, CONV_W), own(PIN_BLK, POOL_W),
                     above(HALO, GC_BLK, CONV_W), above(HALO, H_BLK, CONV_W), above(HALO, PIN_BLK, POOL_W),
                     *tile_shared, *(cast_src if j == 0 else ()), x1.at[pl.ds(j * tq, tq)], uo,
                     *(cast_dst if j == 0 else ()), ext_u.at[j], ext_p.at[j])


def _mixers_prompt(x, w_in, tables, cw, pw, ps, w_o, layer, ln_g, ln_b, alpha, cast_next=None):
    b, s, _ = x.shape
    tq = WINDOW
    ntile = PROMPT_TILES_PER_STEP
    rows = ntile * tq
    steps = s // rows
    bias, sink = tables
    const3 = lambda i, n: (0, 0, 0)
    const2 = lambda i, n: (0, 0)
    per_seq = lambda nrows, width: pl.BlockSpec((None, nrows, width), lambda i, n: (i, 0, 0))
    resident = lambda k, width: pl.BlockSpec((None, k, width), lambda i, n: (0, 0, 0),
                                             pipeline_mode=pl.Buffered(1))
    in_specs = [
        pl.BlockSpec((None, rows, D_MODEL), lambda i, n: (i, n, 0)),
        resident(D_MODEL, IN_W),
        pl.BlockSpec(bias.shape, const3),
        pl.BlockSpec(sink.shape, const3),
        pl.BlockSpec(cw.shape, const2),
        pl.BlockSpec(pw.shape, const3),
        pl.BlockSpec(ps.shape, const2),
        resident(D_MODEL, D_MODEL),
        pl.BlockSpec((None, 1, D_MODEL), lambda i, n: (layer, 0, 0)),
        pl.BlockSpec((None, 1, D_MODEL), lambda i, n: (layer, 0, 0)),
    ]
    operands = [x, w_in, bias, sink, cw, pw, ps, w_o, ln_g, ln_b]
    out_specs = [pl.BlockSpec((None, rows, D_MODEL), lambda i, n: (i, n, 0)),
                 per_seq(CONV_K - 1, CONV_W), per_seq(WINDOW, 2 * KV_W), per_seq(HALO, POOL_W)]
    out_shape = [jax.ShapeDtypeStruct((b, s, D_MODEL), F32),
                 jax.ShapeDtypeStruct((b, CONV_K - 1, CONV_W), F32),
                 jax.ShapeDtypeStruct((b, WINDOW, 2 * KV_W), F32),
                 jax.ShapeDtypeStruct((b, HALO, POOL_W), F32)]
    cast_tf = None
    if cast_next is not None:
        weights, cast_tf = cast_next
        for pos, (w, cast_layer) in enumerate(weights):
            _, k, width = w.shape
            wrows = next(r for r in range(2 * SUBLANES, k + 1, 2 * SUBLANES) if k % r == 0 and k // r <= b * steps)
            if pos == 3:
                assert wrows <= HEAD_DIM
                wrows = HEAD_DIM
            nblocks = k // wrows
            block = lambda i, n, nblocks=nblocks: jnp.minimum(i * steps + n, nblocks - 1)
            out_block = block
            if pos == 3:
                out_block = lambda i, n, block=block: jnp.where(block(i, n) < N_HEADS, _head_block_perm(block(i, n)),
                                                                block(i, n))
            in_specs.append(pl.BlockSpec((None, wrows, width),
                                         lambda i, n, block=block, cl=cast_layer: (cl, block(i, n), 0)))
            operands.append(w)
            out_specs.append(pl.BlockSpec((None, wrows, width), lambda i, n, out_block=out_block: (0, out_block(i, n), 0)))
            out_shape.append(jax.ShapeDtypeStruct((1, k, width), BF16))
    return pl.pallas_call(
        functools.partial(_mixers_prompt_body, ntile, tq, alpha, cast_tf),
        grid=(b, steps),
        in_specs=in_specs,
        out_specs=out_specs,
        out_shape=out_shape,
        scratch_shapes=[pltpu.VMEM((WINDOW + rows, IN_W), F32),
                        pltpu.VMEM((ntile, HALO + tq, CONV_W), F32), pltpu.VMEM((ntile, HALO + tq, POOL_W), F32)],
        compiler_params=_params("arbitrary", "arbitrary"),
        name="mixers_prompt",
    )(*operands)


def _mixers_decode(zs, ck, cv, conv_state, pool_state, tables, cw, pw, ps):
    b = zs.shape[0]
    tq = DECODE_ROWS
    nseq = DECODE_SEQS_PER_STEP
    bias, sink = tables
    new = lambda width, blk: pl.BlockSpec((nseq, 1, width), lambda i, n: (i, 0, blk))
    whole = lambda rows, width: pl.BlockSpec((nseq, rows, width), lambda i, n: (i, 0, 0))
    const3 = lambda i, n: (0, 0, 0)
    const2 = lambda i, n: (0, 0)
    in_specs = [
        new(ATTN_W, Q_BLK), new(KV_W, K_BLK), new(KV_W, V_BLK),
        whole(WINDOW, KV_W), whole(WINDOW, KV_W),
        new(CONV_W, GB_BLK), new(CONV_W, GC_BLK), new(CONV_W, H_BLK), new(POOL_W, PIN_BLK),
        whole(CONV_K - 1, CONV_W), whole(POOL_PREV, POOL_W),
        pl.BlockSpec(bias.shape, const3),
        pl.BlockSpec(sink.shape, const3),
        pl.BlockSpec(cw.shape, const2),
        pl.BlockSpec(pw.shape, const3),
        pl.BlockSpec(ps.shape, const2),
    ]
    return pl.pallas_call(
        functools.partial(_mixers_decode_body, nseq, tq),
        grid=(b // nseq, 1),
        in_specs=in_specs,
        out_specs=[whole(1, D_MODEL), whole(CONV_K - 1, CONV_W), whole(WINDOW, KV_W), whole(WINDOW, KV_W)],
        out_shape=[jax.ShapeDtypeStruct((b, 1, D_MODEL), BF16),
                   jax.ShapeDtypeStruct((b, CONV_K - 1, CONV_W), F32),
                   jax.ShapeDtypeStruct((b, WINDOW, KV_W), F32),
                   jax.ShapeDtypeStruct((b, WINDOW, KV_W), F32)],
        scratch_shapes=[pltpu.VMEM((nseq, HALO + tq, CONV_W), F32), pltpu.VMEM((nseq, HALO + tq, POOL_W), F32)],
        compiler_params=_params("arbitrary", "arbitrary"),
        name="mixers_decode",
    )(zs, zs, zs, ck, cv, zs, zs, zs, zs, conv_state, pool_state, bias, sink, cw, pw, ps)


def _bias_table(bias_by_dist, tq):
    nk = 2 * WINDOW
    by_offset = jnp.zeros((N_HEADS, nk + 1), F32).at[:, 1:WINDOW + 1].set(bias_by_dist[::-1].T)
    rows = jnp.tile(by_offset, (1, tq))[:, :tq * nk].reshape(N_HEADS, tq, nk)
    return jnp.swapaxes(rows.reshape(KV_HEADS, GROUP * tq, nk), 1, 2)


def _sink_table(sinks_l, tq):
    return jnp.broadcast_to(sinks_l.astype(F32).reshape(KV_HEADS, GROUP, 1, 1),
                            (KV_HEADS, GROUP, 1, tq)).reshape(KV_HEADS, 1, GROUP * tq)


def _t5_bucket(n):
    max_exact = N_BUCKETS // 2
    nf = jnp.maximum(n, 1).astype(F32)
    large = max_exact + (jnp.log(nf / max_exact) / math.log(MAX_DISTANCE / max_exact)
                         * (N_BUCKETS - max_exact)).astype(jnp.int32)
    large = jnp.minimum(large, N_BUCKETS - 1)
    return jnp.where(n < max_exact, n, large)


def _proj_ln_body(alpha, a_ref, w_ref, x_ref, g_ref, b_ref, o_ref):
    o_ref[...] = _layer_norm(alpha * x_ref[...] + _dot(a_ref[...], w_ref[...]), g_ref[...], b_ref[...])


def _proj_ln(a, w, w_layer, x, g, b, layer, alpha, tm):
    m, k = a.shape
    n = w.shape[2]
    return pl.pallas_call(
        functools.partial(_proj_ln_body, alpha),
        grid=(m // tm,),
        in_specs=[pl.BlockSpec((tm, k), lambda i: (i, 0)),
                  pl.BlockSpec((None, k, n), lambda i: (w_layer, 0, 0)),
                  pl.BlockSpec((tm, n), lambda i: (i, 0)),
                  pl.BlockSpec((None, 1, n), lambda i: (layer, 0, 0)),
                  pl.BlockSpec((None, 1, n), lambda i: (layer, 0, 0))],
        out_specs=pl.BlockSpec((tm, n), lambda i: (i, 0)),
        out_shape=jax.ShapeDtypeStruct((m, n), F32),
        compiler_params=_params("arbitrary"),
        name="proj_ln",
    )(a, w, x, g, b)


def _ffn_body(tm, tf, ns, tiles_per_seq, nc, alpha, x_ref, wu_ref, wd_ref, cwg_ref, cwv_ref, g_ref, b_ref,
              xs_ref, sg_ref, sv_ref, o_ref, tail_g_ref, tail_v_ref, xso_ref, ugo_ref, uvo_ref,
              xb_ref, ext, carry, xsb_ref, accs_ref):
    i = pl.program_id(0)
    c = pl.program_id(1)
    ts = tm // ns

    @pl.when(c == 0)
    def _():
        xb_ref[...] = x_ref[...].astype(BF16)
        o_ref[...] = jnp.zeros_like(o_ref)

    seq_start = (i % tiles_per_seq) == 0

    def up_proj(h, prev_tail):
        up = _dot(xb_ref[pl.ds(h * ts, ts)], wu_ref[...])
        if h == 0:
            @pl.when(seq_start)
            def _():
                ext[0, 0:SUBLANES] = jnp.zeros((SUBLANES, 2 * tf), F32)

            @pl.when(jnp.logical_not(seq_start))
            def _():
                ext[0, 0:SUBLANES] = carry[c]
        else:
            ext[h, 0:SUBLANES] = prev_tail
        ext[h, SUBLANES:SUBLANES + ts] = up
        tail = up[ts - SUBLANES:ts]
        if h == ns - 1:
            carry[c] = tail
            tail_g_ref[...] = tail[:, :tf]
            tail_v_ref[...] = tail[:, tf:]
        return tail

    def gate(h):
        cw = jnp.concatenate([cwg_ref[...], cwv_ref[...]], axis=1)
        hc = ext[h, SUBLANES - 2:SUBLANES - 2 + ts] * cw[0:1]
        hc = hc + ext[h, SUBLANES - 1:SUBLANES - 1 + ts] * cw[1:2]
        hc = hc + ext[h, SUBLANES:SUBLANES + ts] * cw[2:3]
        return (jax.nn.silu(hc[:, :tf]) * hc[:, tf:]).astype(BF16)

    def down_proj(h, act):
        o_ref[pl.ds(h * ts, ts)] += _dot(act, wd_ref[...])

    def chunk_step(last_chunk):
        tail = up_proj(0, None)
        for h in range(ns):
            act = gate(h)
            if h + 1 < ns:
                tail = up_proj(h + 1, tail)
            down_proj(h, act)
            if last_chunk:
                rows = pl.ds(h * ts, ts)
                o_ref[rows] = _layer_norm(alpha * x_ref[rows] + o_ref[rows], g_ref[...], b_ref[...])

    pl.when(c < nc - 1)(functools.partial(chunk_step, False))
    pl.when(c == nc - 1)(functools.partial(chunk_step, True))

    @pl.when(i == pl.num_programs(0) - 1)
    def _():
        @pl.when(c == 0)
        def _():
            xsb_ref[...] = xs_ref[...].astype(BF16)
            accs_ref[...] = jnp.zeros_like(accs_ref)

        up = _dot(xsb_ref[...], wu_ref[...])
        ugo_ref[...] = up[:, :tf]
        uvo_ref[...] = up[:, tf:]

        def conv(u, s_ref, cw_ref):
            cw = cw_ref[...]
            y = s_ref[:, 0, :] * cw[0:1]
            y = y + s_ref[:, 1, :] * cw[1:2]
            return y + u * cw[2:3]

        act = (jax.nn.silu(conv(up[:, :tf], sg_ref, cwg_ref)) * conv(up[:, tf:], sv_ref, cwv_ref)).astype(BF16)
        accs_ref[...] += _dot(act, wd_ref[...])

        @pl.when(c == nc - 1)
        def _():
            xso_ref[...] = _layer_norm(alpha * xs_ref[...] + accs_ref[...], g_ref[...], b_ref[...])


def _ffn_prompt(x, xs, state, w_up_il, w_down, w_layer, cw, g, b, layer, alpha, seq, tm, tf, ns):
    m = x.shape[0]
    bs = xs.shape[0]
    nc = D_FF // tf
    tiles_per_seq = seq // tm
    last_tile = m // tm - 1
    tail_spec = pl.BlockSpec((None, SUBLANES, tf), lambda i, c: (i, 0, c))
    tail_shape = jax.ShapeDtypeStruct((m // tm, SUBLANES, D_FF), F32)
    dec_chunk = lambda i, c: jnp.where(i == last_tile, c, 0)
    dec_spec = pl.BlockSpec((bs, tf), lambda i, c: (0, dec_chunk(i, c)))
    dec_shape = jax.ShapeDtypeStruct((bs, D_FF), F32)
    out, tail_g, tail_v, xs_new, up_g, up_v = pl.pallas_call(
        functools.partial(_ffn_body, tm, tf, ns, tiles_per_seq, nc, alpha),
        grid=(m // tm, nc),
        in_specs=[pl.BlockSpec((tm, D_MODEL), lambda i, c: (i, 0), pipeline_mode=pl.Buffered(1)),
                  pl.BlockSpec((None, D_MODEL, 2 * tf), lambda i, c: (w_layer, 0, c)),
                  pl.BlockSpec((None, tf, D_MODEL), lambda i, c: (w_layer, c, 0)),
                  pl.BlockSpec((None, CONV_K, tf), lambda i, c: (layer, 0, c)),
                  pl.BlockSpec((None, CONV_K, tf), lambda i, c: (layer, 0, nc + c)),
                  pl.BlockSpec((None, 1, D_MODEL), lambda i, c: (layer, 0, 0)),
                  pl.BlockSpec((None, 1, D_MODEL), lambda i, c: (layer, 0, 0)),
                  pl.BlockSpec((bs, D_MODEL), lambda i, c: (0, 0)),
                  pl.BlockSpec((bs, CONV_K - 1, tf), lambda i, c: (0, 0, dec_chunk(i, c))),
                  pl.BlockSpec((bs, CONV_K - 1, tf), lambda i, c: (0, 0, nc + dec_chunk(i, c)))],
        out_specs=[pl.BlockSpec((tm, D_MODEL), lambda i, c: (i, 0)), tail_spec, tail_spec,
                   pl.BlockSpec((bs, D_MODEL), lambda i, c: (0, 0)), dec_spec, dec_spec],
        out_shape=[jax.ShapeDtypeStruct((m, D_MODEL), F32), tail_shape, tail_shape,
                   jax.ShapeDtypeStruct((bs, D_MODEL), F32), dec_shape, dec_shape],
        scratch_shapes=[pltpu.VMEM((tm, D_MODEL), BF16),
                        pltpu.VMEM((ns, SUBLANES + tm // ns, 2 * tf), F32),
                        pltpu.VMEM((nc, SUBLANES, 2 * tf), F32),
                        pltpu.VMEM((bs, D_MODEL), BF16),
                        pltpu.VMEM((bs, D_MODEL), F32)],
        compiler_params=_params("arbitrary", "arbitrary"),
        name="ffn_prompt",
    )(x, w_up_il, w_down, cw, cw, g, b, xs, state, state)
    last = slice(tiles_per_seq - 1, None, tiles_per_seq)
    return out, jnp.concatenate([tail_g[last], tail_v[last]], -1), xs_new, jnp.concatenate([up_g, up_v], -1)


def kernel(x_prompt, x_sample, cache_k, cache_v, state_conv, state_pool, state_ffn, rel_table, w_in, conv_w,
           pool_w, pool_scale, sinks, w_o, ln1_g, ln1_b, w_up, ffn_conv_w, w_down, ln2_g, ln2_b):
    depth = w_in.shape[0]
    bp_, seq, _ = x_prompt.shape
    bs = x_sample.shape[0]
    assert x_sample.shape[1] == 1 and cache_k.shape[2] == WINDOW
    alpha = (2 * depth) ** 0.25

    w_in_b, w_o_b = [_cast_w_in(w_in, 0, CAST_ROWS)], [_cast_w_o(w_o, 0)]
    w_up_il, w_down_b = [], []
    pool_w_b = pool_w.astype(BF16)

    bias_by_dist = rel_table.astype(F32)[_t5_bucket(jnp.arange(WINDOW))]
    bias_p = _bias_table(bias_by_dist, WINDOW)
    bias_s = _bias_table(bias_by_dist, DECODE_ROWS)

    ln1_g, ln1_b, ln2_g, ln2_b = (t[:, None, :] for t in (ln1_g, ln1_b, ln2_g, ln2_b))

    xp = x_prompt.reshape(bp_ * seq, D_MODEL)
    xs = x_sample.reshape(bs, D_MODEL)

    outs_p, outs_s = [], []
    for l in range(depth):
        tab_p = (bias_p, _sink_table(sinks[l], WINDOW))
        tab_s = (bias_s, _sink_table(sinks[l], DECODE_ROWS))
        ps = pool_scale[l][None, :]

        side = [(w_up, l), (w_down, l)] + ([(w_in, l + 1), (w_o, l + 1)] if l + 1 < depth else [])
        x1, u_tail, kv_tail, pin_tail, *converted = _mixers_prompt(
            xp.reshape(bp_, seq, D_MODEL), w_in_b[l], tab_p, conv_w[l], pool_w_b[l], ps, w_o_b[l], l, ln1_g, ln1_b,
            alpha, (side, FFN_TF))
        for store, wb in zip((w_up_il, w_down_b, w_in_b, w_o_b), converted):
            store.append(wb)
        x1 = x1.reshape(bp_ * seq, D_MODEL)

        zs = _matmul(xs, w_in_b[l], 0, bs, DECODE_TN)
        ck = cache_k[l].reshape(bs, WINDOW, KV_W)
        cv = cache_v[l].reshape(bs, WINDOW, KV_W)
        cat_s, u_new, k_s, v_s = _mixers_decode(zs[:, None, :], ck, cv, state_conv[l], state_pool[l], tab_s,
                                                conv_w[l], pool_w_b[l], ps)
        x1s = _proj_ln(cat_s.reshape(bs, D_MODEL), w_o_b[l], 0, xs, ln1_g, ln1_b, l, alpha, bs)

        xp, up_tail, xs, up_s = _ffn_prompt(x1, x1s, state_ffn[l], w_up_il[l], w_down_b[l], 0, ffn_conv_w, ln2_g, ln2_b,
                                            l, alpha, seq, FFN_TM, FFN_TF, FFN_NS)
        outs_p.append((
            kv_tail[:, :, :KV_W].reshape(bp_, WINDOW, KV_HEADS, HEAD_DIM),
            kv_tail[:, :, KV_W:].reshape(bp_, WINDOW, KV_HEADS, HEAD_DIM),
            u_tail,
            pin_tail[:, HALO - POOL_PREV:],
            up_tail[:, SUBLANES - (CONV_K - 1):],
        ))

        outs_s.append((
            k_s.reshape(bs, WINDOW, KV_HEADS, HEAD_DIM),
            v_s.reshape(bs, WINDOW, KV_HEADS, HEAD_DIM),
            u_new,
            jnp.concatenate([state_pool[l][:, 1:], zs[:, None, IN_W - POOL_W:]], 1),
            jnp.concatenate([state_ffn[l][:, 1:], up_s[:, None, :]], 1),
        ))

    st = lambda lst, i: jnp.stack([e[i] for e in lst], 0)
    return (xp.reshape(bp_, seq, D_MODEL), xs.reshape(bs, 1, D_MODEL),
            st(outs_p, 0), st(outs_p, 1), st(outs_p, 2), st(outs_p, 3), st(outs_p, 4),
            st(outs_s, 0), st(outs_s, 1), st(outs_s, 2), st(outs_s, 3), st(outs_s, 4))
```
